```python
import math
import jax, jax.numpy as jnp
from jax import lax
import numpy as np

D_MODEL = 1024
BATCH = 8
SEQ = 2048
DEPTH = 1
DEC_BATCH = 32
DEC_SEQ = 4
PAST_LEN = 16384
PAGE_SIZE = 128

HG_HEADS = 4
HG_DK = 128
HG_DV = 128
HG_CHUNK = 64
MLA_HEADS = 4
QK_NOPE = 128
QK_ROPE = 64
V_DIM = 128
Q_LORA = 768
KV_LORA = 256
ROPE_THETA = 10000.0
Q_BLOCK = 128
D_FF = 2816
EPS = 1e-6

HG_KW = HG_HEADS * HG_DK
HG_WIDTH = HG_HEADS * HG_DV
MLA_WIDTH = MLA_HEADS * V_DIM
MIX_WIDTH = HG_WIDTH + MLA_WIDTH
IN_SIZES = (HG_KW, HG_KW, HG_WIDTH, HG_WIDTH, Q_LORA, KV_LORA, QK_ROPE)
IN_WIDTH = sum(IN_SIZES)
IN_OFFSETS = [int(v) for v in np.cumsum(IN_SIZES)[:-1]]
N_NORMS = 6

kernel_name = "hymba_hgrn2_mla_macaron_step"


def rmsnorm(x, g):
    xf = x.astype(jnp.float32)
    y = xf * lax.rsqrt(jnp.mean(xf * xf, axis=-1, keepdims=True) + EPS)
    return (y * g.astype(jnp.float32)).astype(x.dtype)


def rope(x, pos):
    half = x.shape[-1] // 2
    inv = ROPE_THETA ** (-jnp.arange(half, dtype=jnp.float32) / half)
    ang = pos.astype(jnp.float32)[:, None] * inv[None, :]
    cos = jnp.cos(ang)[:, None, :]
    sin = jnp.sin(ang)[:, None, :]
    xf = x.astype(jnp.float32)
    x1, x2 = xf[..., :half], xf[..., half:]
    out = jnp.concatenate([x1 * cos - x2 * sin, x1 * sin + x2 * cos], axis=-1)
    return out.astype(x.dtype)


def swiglu(x, wg, wu, wd):
    return (jax.nn.silu(x @ wg) * (x @ wu)) @ wd


def hgrn2_chunked(q, k, v, logf, s0):
    B, L, H, DK = q.shape
    DV = v.shape[-1]
    C = math.gcd(L, HG_CHUNK)
    n = L // C

    def chunks(a):
        return a.astype(jnp.float32).reshape(B, n, C, H, a.shape[-1]).transpose(1, 0, 3, 2, 4)

    tril = jnp.tril(jnp.ones((C, C), dtype=bool))

    def step(S, inp):
        qc, kc, vc, gc = inp
        b = jnp.cumsum(gc, axis=-2)
        diff = b[..., :, None, :] - b[..., None, :, :]
        decay = jnp.exp(jnp.where(tril[:, :, None], diff, -jnp.inf))
        A = jnp.einsum('bhtsk,bhsk->bhts', decay * qc[..., :, None, :], kc)
        o = jnp.einsum('bhts,bhsv->bhtv', A, vc) + jnp.einsum('bhtk,bhkv->bhtv', qc * jnp.exp(b), S)
        bl = b[..., -1:, :]
        S_new = jnp.exp(bl[..., 0, :])[..., None] * S + jnp.einsum('bhsk,bhsv->bhkv', kc * jnp.exp(bl - b), vc)
        return S_new, o

    S_fin, o = lax.scan(step, s0.astype(jnp.float32), (chunks(q), chunks(k), chunks(v), chunks(logf)))
    o = o.transpose(1, 0, 3, 2, 4).reshape(B, L, H, DV)
    return o.astype(v.dtype), S_fin.astype(s0.dtype)


def mla_attend(q_lat, q_rope, c_kv, k_rope, q_pos, k_pos):
    B, T = q_lat.shape[:2]
    blk = math.gcd(T, Q_BLOCK)
    nb = T // blk
    scale = 1.0 / math.sqrt(QK_NOPE + QK_ROPE)

    def blocks(a):
        return jnp.moveaxis(a.reshape(B, nb, blk, *a.shape[2:]), 1, 0)

    def one(args):
        ql, qr, qp = args
        s = (jnp.einsum('bqhr,bsr->bhqs', ql, c_kv) + jnp.einsum('bqhd,bsd->bhqs', qr, k_rope)).astype(jnp.float32) * scale
        s = jnp.where(k_pos[None, None, None, :] <= qp[None, None, :, None], s, -jnp.inf)
        p = jax.nn.softmax(s, axis=-1).astype(c_kv.dtype)
        return jnp.einsum('bhqs,bsr->bqhr', p, c_kv)

    out = lax.map(one, (blocks(q_lat), blocks(q_rope), q_pos.reshape(nb, blk)))
    return jnp.moveaxis(out, 0, 1).reshape(q_lat.shape)


def mixer(u, pos, past_ckv, past_kr, past_pos, s0, lb, w_in, q_norm_g, w_q_up, kv_norm_g, w_kv_up, hg_norm_g, w_out):
    B, T, _ = u.shape
    z = u @ w_in
    hq, hf, hi, hg, cq, ckv, kr = jnp.split(z, IN_OFFSETS, axis=-1)
    f = lb + (1.0 - lb) * jax.nn.sigmoid(hf.astype(jnp.float32))
    logf = jnp.log(f)
    k = 1.0 - f
    q = jax.nn.silu(hq)
    heads = lambda a, d: a.reshape(B, T, HG_HEADS, d)
    o_h, s_fin = hgrn2_chunked(heads(q, HG_DK), heads(k, HG_DK), heads(hi, HG_DV), heads(logf, HG_DK), s0)
    o_h = rmsnorm(o_h, hg_norm_g) * jax.nn.silu(heads(hg, HG_DV))
    cq = rmsnorm(cq, q_norm_g)
    qh = jnp.einsum('btc,chd->bthd', cq, w_q_up)
    q_nope = qh[..., :QK_NOPE]
    q_r = rope(qh[..., QK_NOPE:], pos)
    c_kv = rmsnorm(ckv, kv_norm_g)
    k_r = rope(kr[:, :, None, :], pos)[:, :, 0, :]
    w_uk = w_kv_up[..., :QK_NOPE]
    w_uv = w_kv_up[..., QK_NOPE:]
    q_lat = jnp.einsum('bthd,rhd->bthr', q_nope, w_uk)
    keys_c = jnp.concatenate([past_ckv, c_kv], axis=1)
    keys_r = jnp.concatenate([past_kr, k_r], axis=1)
    k_pos = jnp.concatenate([past_pos, pos])
    o_lat = mla_attend(q_lat, q_r, keys_c, keys_r, pos, k_pos)
    o_m = jnp.einsum('bthr,rhv->bthv', o_lat, w_uv)
    o = jnp.concatenate([o_h.reshape(B, T, HG_WIDTH), o_m.reshape(B, T, MLA_WIDTH)], axis=-1)
    return o @ w_out, c_kv, k_r, s_fin


def layer(h, pos, past_ckv, past_kr, past_pos, s0, lb, norm_g, wg, wu, wd,
          w_in, q_norm_g, w_q_up, kv_norm_g, w_kv_up, hg_norm_g, w_out):
    h = h + 0.5 * rmsnorm(swiglu(rmsnorm(h, norm_g[0]), wg[0], wu[0], wd[0]), norm_g[1])
    m, c_kv, k_r, s_fin = mixer(rmsnorm(h, norm_g[2]), pos, past_ckv, past_kr, past_pos, s0, lb,
                                w_in, q_norm_g, w_q_up, kv_norm_g, w_kv_up, hg_norm_g, w_out)
    h = h + rmsnorm(m, norm_g[3])
    h = h + 0.5 * rmsnorm(swiglu(rmsnorm(h, norm_g[4]), wg[1], wu[1], wd[1]), norm_g[5])
    return h, c_kv, k_r, s_fin


def setup_inputs(seed: int = 0) -> dict:
    key = jax.random.key(seed)
    ks = jax.random.split(key, 20)
    n_pages = PAST_LEN // PAGE_SIZE
    n_used = DEC_BATCH * n_pages
    n_pool = (5 * n_used) // 4
    nrm = lambda k, shape, s: jax.random.normal(k, shape, dtype=jnp.float32) * s
    page_table = jax.random.permutation(ks[5], n_pool)[:n_used].reshape(DEC_BATCH, n_pages).astype(jnp.int32)
    return {
        "x_prompt": nrm(ks[0], (BATCH, SEQ, D_MODEL), 1.0),
        "x_sample": nrm(ks[1], (DEC_BATCH, DEC_SEQ, D_MODEL), 1.0),
        "cache_kv_latent": nrm(ks[2], (DEPTH, n_pool, PAGE_SIZE, KV_LORA), 1.0),
        "cache_k_rope": nrm(ks[3], (DEPTH, n_pool, PAGE_SIZE, QK_ROPE), 1.0),
        "state_hgrn": nrm(ks[4], (DEPTH, DEC_BATCH, HG_HEADS, HG_DK, HG_DV), 0.5),
        "page_table": page_table,
        "hgrn_lb_logits": nrm(ks[6], (DEPTH + 1, HG_KW), 0.5),
        "norm_g": 1.0 + nrm(ks[7], (DEPTH, N_NORMS, D_MODEL), 0.1),
        "w_ffn_gate": nrm(ks[8], (DEPTH, 2, D_MODEL, D_FF), D_MODEL ** -0.5),
        "w_ffn_up": nrm(ks[9], (DEPTH, 2, D_MODEL, D_FF), D_MODEL ** -0.5),
        "w_ffn_down": nrm(ks[10], (DEPTH, 2, D_FF, D_MODEL), D_FF ** -0.5),
        "w_in": nrm(ks[11], (DEPTH, D_MODEL, IN_WIDTH), D_MODEL ** -0.5),
        "q_norm_g": 1.0 + nrm(ks[12], (DEPTH, Q_LORA), 0.1),
        "w_q_up": nrm(ks[13], (DEPTH, Q_LORA, MLA_HEADS, QK_NOPE + QK_ROPE), Q_LORA ** -0.5),
        "kv_norm_g": 1.0 + nrm(ks[14], (DEPTH, KV_LORA), 0.1),
        "w_kv_up": nrm(ks[15], (DEPTH, KV_LORA, MLA_HEADS, QK_NOPE + V_DIM), KV_LORA ** -0.5),
        "hg_norm_g": 1.0 + nrm(ks[16], (DEPTH, HG_DV), 0.1),
        "w_out": nrm(ks[17], (DEPTH, MIX_WIDTH, D_MODEL), MIX_WIDTH ** -0.5),
    }


def reference(x_prompt, x_sample, cache_kv_latent, cache_k_rope, state_hgrn, page_table,
              hgrn_lb_logits, norm_g, w_ffn_gate, w_ffn_up, w_ffn_down, w_in, q_norm_g, w_q_up,
              kv_norm_g, w_kv_up, hg_norm_g, w_out):
    Bp, Tp, _ = x_prompt.shape
    Bs, Ts, _ = x_sample.shape
    past_len = page_table.shape[1] * cache_kv_latent.shape[2]
    pos_p = jnp.arange(Tp, dtype=jnp.int32)
    pos_s = past_len + jnp.arange(Ts, dtype=jnp.int32)
    past_pos_s = jnp.arange(past_len, dtype=jnp.int32)
    empty_pos = jnp.zeros((0,), dtype=jnp.int32)
    lb_all = jnp.cumsum(jax.nn.softmax(hgrn_lb_logits.astype(jnp.float32), axis=0), axis=0)

    hp, hs = x_prompt, x_sample
    ckv_p, kr_p, st_p, ckv_s, kr_s, st_s = [], [], [], [], [], []
    for l in range(DEPTH):
        w = (lb_all[l], norm_g[l], w_ffn_gate[l], w_ffn_up[l], w_ffn_down[l], w_in[l], q_norm_g[l],
             w_q_up[l], kv_norm_g[l], w_kv_up[l], hg_norm_g[l], w_out[l])
        hp, c1, r1, s1 = layer(hp, pos_p,
                               jnp.zeros((Bp, 0, KV_LORA), hp.dtype), jnp.zeros((Bp, 0, QK_ROPE), hp.dtype),
                               empty_pos, jnp.zeros((Bp, HG_HEADS, HG_DK, HG_DV), hp.dtype), *w)
        past_c = cache_kv_latent[l][page_table].reshape(Bs, past_len, KV_LORA)
        past_r = cache_k_rope[l][page_table].reshape(Bs, past_len, QK_ROPE)
        hs, c2, r2, s2 = layer(hs, pos_s, past_c, past_r, past_pos_s, state_hgrn[l], *w)
        ckv_p.append(c1); kr_p.append(r1); st_p.append(s1)
        ckv_s.append(c2); kr_s.append(r2); st_s.append(s2)

    return (hp, hs, jnp.stack(ckv_p), jnp.stack(kr_p), jnp.stack(st_p),
            jnp.stack(ckv_s), jnp.stack(kr_s), jnp.stack(st_s))
```

```python
import functools
import math

import jax
import jax.numpy as jnp
from jax import lax
from jax.experimental import pallas as pl
from jax.experimental.pallas import tpu as pltpu

D_MODEL = 1024
D_FF = 2816
HG_HEADS = 4
HG_DK = 128
HG_DV = 128
MLA_HEADS = 4
QK_NOPE = 128
QK_ROPE = 64
V_DIM = 128
Q_LORA = 768
KV_LORA = 256
ROPE_THETA = 10000.0
EPS = 1e-6
HG_W = HG_HEADS * HG_DK

VMEM_LIMIT_BYTES = 56 * 1024 * 1024
SUBLANES = 8
NEG_BIG = -1e30

F32 = jnp.float32
BF16 = jnp.bfloat16


def _dot(a, b):
    return jnp.dot(a, b, preferred_element_type=F32)


def _dot_nt(a, b):
    return lax.dot_general(a, b, (((1,), (1,)), ((), ())), preferred_element_type=F32)


def _dot_tn(a, b):
    return lax.dot_general(a, b, (((0,), (0,)), ((), ())), preferred_element_type=F32)


def _rms(x, g):
    return x * lax.rsqrt(jnp.mean(x * x, axis=-1, keepdims=True) + EPS) * g


def _sigmoid(x):
    return 1.0 / (1.0 + jnp.exp(-x))


def _silu(x):
    return x * _sigmoid(x)


def _params(*sem):
    return pltpu.CompilerParams(dimension_semantics=sem, vmem_limit_bytes=VMEM_LIMIT_BYTES)


def _ffn_kernel(x_ref, gpre_ref, gpost_ref, wg_ref, wu_ref, wd_ref, o_ref, xn_ref, acc_ref):
    f = pl.program_id(1)

    @pl.when(f == 0)
    def _():
        xn_ref[...] = _rms(x_ref[...], gpre_ref[...]).astype(BF16)
        acc_ref[...] = jnp.zeros_like(acc_ref)

    xn = xn_ref[...]
    a = _dot(xn, wg_ref[...])
    b = _dot(xn, wu_ref[...])
    h = (_silu(a) * b).astype(BF16)
    acc_ref[...] += _dot(h, wd_ref[...])

    @pl.when(f == pl.num_programs(1) - 1)
    def _():
        o_ref[...] = x_ref[...] + 0.5 * _rms(acc_ref[...], gpost_ref[...])


def _ffn_block(x, g_pre, g_post, wg, wu, wd):
    m = x.shape[0]
    tm = min(m, 1024)
    tf = 256
    return pl.pallas_call(
        _ffn_kernel,
        grid=(m // tm, D_FF // tf),
        in_specs=[
            pl.BlockSpec((tm, D_MODEL), lambda i, f: (i, 0)),
            pl.BlockSpec((1, D_MODEL), lambda i, f: (0, 0)),
            pl.BlockSpec((1, D_MODEL), lambda i, f: (0, 0)),
            pl.BlockSpec((D_MODEL, tf), lambda i, f: (0, f)),
            pl.BlockSpec((D_MODEL, tf), lambda i, f: (0, f)),
            pl.BlockSpec((tf, D_MODEL), lambda i, f: (f, 0)),
        ],
        out_specs=pl.BlockSpec((tm, D_MODEL), lambda i, f: (i, 0)),
        out_shape=jax.ShapeDtypeStruct((m, D_MODEL), F32),
        scratch_shapes=[pltpu.VMEM((tm, D_MODEL), BF16), pltpu.VMEM((tm, D_MODEL), F32)],
        compiler_params=_params("parallel", "arbitrary"),
        name="ffn_block",
    )(x, g_pre, g_post, wg, wu, wd)


def _inproj_kernel(x_ref, g_ref, csk_ref, cosq_ref, sinq_ref, wh_ref, wcq_ref, wckv_ref, wkr_ref,
                   qg_ref, kvg_ref, wqn_ref, wqr_ref, wqrr_ref, wuk_ref,
                   zh_ref, ckv_ref, kr_ref, ckvb_ref, krb_ref, qlat_ref, qrope_ref):
    u = _rms(x_ref[...], g_ref[...]).astype(BF16)
    zh_ref[...] = _dot(u, wh_ref[...])
    ckv = _rms(_dot(u, wckv_ref[...]), kvg_ref[...])
    ckv_ref[...] = ckv
    ckvb_ref[...] = ckv.astype(BF16)
    kz = _dot(u, wkr_ref[...])
    csk = csk_ref[...]
    kr = kz[:, :QK_ROPE] * csk[:, :QK_ROPE] + kz[:, QK_ROPE:] * csk[:, QK_ROPE:]
    kr_ref[...] = kr
    krb_ref[...] = kr.astype(BF16)
    cqn = _rms(_dot(u, wcq_ref[...]), qg_ref[...]).astype(BF16)
    qn = _dot(cqn, wqn_ref[...]).astype(BF16)
    for h in range(MLA_HEADS):
        qlat_ref[:, h * KV_LORA:(h + 1) * KV_LORA] = _dot(
            qn[:, h * QK_NOPE:(h + 1) * QK_NOPE], wuk_ref[h]).astype(BF16)
    qr = _dot(cqn, wqr_ref[...]) * cosq_ref[...] + _dot(cqn, wqrr_ref[...]) * sinq_ref[...]
    qrope_ref[...] = qr.astype(BF16)


def _in_proj(x, g, csk, cosq, sinq, w):
    m = x.shape[0]
    tm = min(m, 256)
    nrep = csk.shape[0] // tm
    full = lambda a: pl.BlockSpec(a.shape, lambda i: (0,) * a.ndim)
    row = lambda width: pl.BlockSpec((tm, width), lambda i: (i, 0))
    tab = lambda width: pl.BlockSpec((tm, width), lambda i: (i % nrep, 0))
    rope_w = MLA_HEADS * QK_ROPE
    return pl.pallas_call(
        _inproj_kernel,
        grid=(m // tm,),
        in_specs=[row(D_MODEL), full(g), tab(2 * QK_ROPE), tab(rope_w), tab(rope_w),
                  full(w["w_h"]), full(w["w_cq"]), full(w["w_ckv"]), full(w["w_kr"]),
                  full(w["q_norm_g"]), full(w["kv_norm_g"]), full(w["w_qn"]), full(w["w_qr"]),
                  full(w["w_qrr"]), full(w["w_uk"])],
        out_specs=[row(4 * HG_W), row(KV_LORA), row(QK_ROPE), row(KV_LORA), row(QK_ROPE),
                   row(MLA_HEADS * KV_LORA), row(rope_w)],
        out_shape=[
            jax.ShapeDtypeStruct((m, 4 * HG_W), F32),
            jax.ShapeDtypeStruct((m, KV_LORA), F32),
            jax.ShapeDtypeStruct((m, QK_ROPE), F32),
            jax.ShapeDtypeStruct((m, KV_LORA), BF16),
            jax.ShapeDtypeStruct((m, QK_ROPE), BF16),
            jax.ShapeDtypeStruct((m, MLA_HEADS * KV_LORA), BF16),
            jax.ShapeDtypeStruct((m, rope_w), BF16),
        ],
        compiler_params=_params("parallel"),
        name="in_proj",
    )(x, g, csk, cosq, sinq, w["w_h"], w["w_cq"], w["w_ckv"], w["w_kr"], w["q_norm_g"],
      w["kv_norm_g"], w["w_qn"], w["w_qr"], w["w_qrr"], w["w_uk"])


def _split3(g):
    g1 = g.astype(BF16).astype(F32)
    r1 = g - g1
    g2 = r1.astype(BF16).astype(F32)
    g3 = (r1 - g2).astype(BF16).astype(F32)
    return g1, g2, g3


def _hgrn_kernel(*refs, chunk, t_valid, has_s0):
    if has_s0:
        hq_ref, hf_ref, hi_ref, hg_ref, lbl_ref, ng_ref, s0_ref = refs[:7]
        refs = refs[7:]
    else:
        hq_ref, hf_ref, hi_ref, hg_ref, lbl_ref, ng_ref = refs[:6]
        s0_ref = None
        refs = refs[6:]
    o_ref, sfin_ref, st_ref, b_ref, k_ref, v_ref = refs
    t_total = hq_ref.shape[0]
    n_chunks = t_total // chunk
    n_tiles = chunk // SUBLANES

    lbl = lbl_ref[...]
    e = jnp.exp(lbl - jnp.max(lbl, axis=0, keepdims=True))
    lb = e[0:1, :] / jnp.sum(e, axis=0, keepdims=True)

    st_ref[...] = s0_ref[...].T if has_s0 else jnp.zeros_like(st_ref)

    r_i = lax.broadcasted_iota(jnp.int32, (chunk, chunk), 0)
    c_i = lax.broadcasted_iota(jnp.int32, (chunk, chunk), 1)
    tril = (c_i <= r_i).astype(F32)
    row8 = lax.broadcasted_iota(jnp.int32, (SUBLANES, HG_DK), 0)
    ng = ng_ref[...]

    def chunk_body(c, carry):
        start = pl.multiple_of(c * chunk, chunk)
        sl = pl.ds(start, chunk)
        q = _silu(hq_ref[sl, :])
        f = lb + (1.0 - lb) * _sigmoid(hf_ref[sl, :])
        g = jnp.log(f)
        k = 1.0 - f
        v = hi_ref[sl, :]
        if t_valid < t_total:
            valid = (lax.broadcasted_iota(jnp.int32, (chunk, HG_DK), 0) + start) < t_valid
            g = jnp.where(valid, g, 0.0)
            k = jnp.where(valid, k, 0.0)
            v = jnp.where(valid, v, 0.0)
        g1, g2, g3 = _split3(g)
        b = _dot(tril, g1) + _dot(tril, g2) + _dot(tril, g3)
        b_ref[...] = b
        k_ref[...] = k
        v_ref[...] = v

        q_t = [q[i * SUBLANES:(i + 1) * SUBLANES, :] for i in range(n_tiles)]
        b_t = [b[i * SUBLANES:(i + 1) * SUBLANES, :] for i in range(n_tiles)]
        o_t = [jnp.zeros((SUBLANES, HG_DV), F32) for _ in range(n_tiles)]
        for s in range(chunk):
            b_s = b_ref[s:s + 1, :]
            k_s = k_ref[s:s + 1, :]
            v_s = v_ref[s:s + 1, :]
            for i in range(s // SUBLANES, n_tiles):
                d = b_t[i] - b_s
                if i == s // SUBLANES and s % SUBLANES:
                    d = jnp.where(row8 >= s % SUBLANES, d, NEG_BIG)
                p = q_t[i] * jnp.exp(d) * k_s
                o_t[i] = o_t[i] + jnp.sum(p, axis=-1, keepdims=True) * v_s
        o = jnp.concatenate(o_t, axis=0) if n_tiles > 1 else o_t[0]

        st = st_ref[...]
        o = o + _dot_nt((q * jnp.exp(b)).astype(BF16), st.astype(BF16))
        bl = b[chunk - 1:chunk, :]
        kd = (k * jnp.exp(bl - b)).astype(BF16)
        st_ref[...] = st * jnp.exp(bl) + _dot_tn(v.astype(BF16), kd)

        o_ref[sl, :] = _rms(o, ng) * _silu(hg_ref[sl, :])
        return carry

    lax.fori_loop(0, n_chunks, chunk_body, 0)
    sfin_ref[...] = st_ref[...].T


def _hgrn(zh, lb_logits, ng, s0, chunk, t_valid):
    bsz, t, _ = zh.shape
    has_s0 = s0 is not None
    col = lambda j: pl.BlockSpec((None, t, HG_DK), lambda b, h: (b, 0, j * HG_HEADS + h))
    in_specs = [col(0), col(1), col(2), col(3),
                pl.BlockSpec((lb_logits.shape[0], HG_DK), lambda b, h: (0, h)),
                pl.BlockSpec((1, HG_DV), lambda b, h: (0, 0))]
    args = [zh, zh, zh, zh, lb_logits, ng]
    st_spec = pl.BlockSpec((None, None, HG_DK, HG_DV), lambda b, h: (b, h, 0, 0))
    if has_s0:
        in_specs.append(st_spec)
        args.append(s0)
    return pl.pallas_call(
        functools.partial(_hgrn_kernel, chunk=chunk, t_valid=t_valid, has_s0=has_s0),
        grid=(bsz, HG_HEADS),
        in_specs=in_specs,
        out_specs=[pl.BlockSpec((None, t, HG_DV), lambda b, h: (b, 0, h)), st_spec],
        out_shape=[jax.ShapeDtypeStruct((bsz, t, HG_W), F32),
                   jax.ShapeDtypeStruct((bsz, HG_HEADS, HG_DK, HG_DV), F32)],
        scratch_shapes=[pltpu.VMEM((HG_DV, HG_DK), F32), pltpu.VMEM((chunk, HG_DK), F32),
                        pltpu.VMEM((chunk, HG_DK), F32), pltpu.VMEM((chunk, HG_DV), F32)],
        compiler_params=_params("parallel", "parallel"),
        name="hgrn",
    )(*args)


ATTN_SCALE = 1.0 / math.sqrt(QK_NOPE + QK_ROPE)


def _softmax_update(s, v_fn, m_ref, l_ref, acc_ref):
    m_prev = m_ref[...]
    m_new = jnp.maximum(m_prev, jnp.max(s, axis=-1, keepdims=True))
    alpha = jnp.exp(m_prev - m_new)
    p = jnp.exp(s - m_new)
    l_ref[...] = alpha * l_ref[...] + jnp.sum(p, axis=-1, keepdims=True)
    acc_ref[...] = alpha * acc_ref[...] + v_fn(p.astype(BF16))
    m_ref[...] = m_new


def _attn_prompt_kernel(ql_ref, qr_ref, kc_ref, kr_ref, wuv_ref, o_ref, m_ref, l_ref, acc_ref, *, tq):
    qi = pl.program_id(1)
    ql = jnp.concatenate([ql_ref[:, h * KV_LORA:(h + 1) * KV_LORA] for h in range(MLA_HEADS)], axis=0)
    qr = jnp.concatenate([qr_ref[:, h * QK_ROPE:(h + 1) * QK_ROPE] for h in range(MLA_HEADS)], axis=0)
    m_ref[...] = jnp.full_like(m_ref, NEG_BIG)
    l_ref[...] = jnp.zeros_like(l_ref)
    acc_ref[...] = jnp.zeros_like(acc_ref)

    def step(kb, masked):
        sl = pl.ds(pl.multiple_of(kb * tq, tq), tq)
        kc = kc_ref[sl, :]
        s = (_dot_nt(ql, kc) + _dot_nt(qr, kr_ref[sl, :])) * ATTN_SCALE
        if masked:
            rows = lax.broadcasted_iota(jnp.int32, s.shape, 0) & (tq - 1)
            cols = lax.broadcasted_iota(jnp.int32, s.shape, 1)
            s = jnp.where(cols <= rows, s, NEG_BIG)
        _softmax_update(s, lambda p: _dot(p, kc), m_ref, l_ref, acc_ref)

    def body(kb, carry):
        step(kb, False)
        return carry

    lax.fori_loop(0, qi, body, 0)
    step(qi, True)
    o_lat = (acc_ref[...] / l_ref[...]).astype(BF16)
    for h in range(MLA_HEADS):
        o_ref[:, h * V_DIM:(h + 1) * V_DIM] = _dot(o_lat[h * tq:(h + 1) * tq, :], wuv_ref[h])


def _attn_prompt(qlat, qrope, ckv_b, kr_b, w_uv, bsz, t):
    tq = 256
    assert tq & (tq - 1) == 0 and t % tq == 0
    nq = t // tq
    rows = MLA_HEADS * tq
    return pl.pallas_call(
        functools.partial(_attn_prompt_kernel, tq=tq),
        grid=(bsz, nq),
        in_specs=[
            pl.BlockSpec((tq, MLA_HEADS * KV_LORA), lambda b, i: (b * nq + i, 0)),
            pl.BlockSpec((tq, MLA_HEADS * QK_ROPE), lambda b, i: (b * nq + i, 0)),
            pl.BlockSpec((t, KV_LORA), lambda b, i: (b, 0)),
            pl.BlockSpec((t, QK_ROPE), lambda b, i: (b, 0)),
            pl.BlockSpec(w_uv.shape, lambda b, i: (0, 0, 0)),
        ],
        out_specs=pl.BlockSpec((tq, MLA_HEADS * V_DIM), lambda b, i: (b * nq + i, 0)),
        out_shape=jax.ShapeDtypeStruct((bsz * t, MLA_HEADS * V_DIM), F32),
        scratch_shapes=[pltpu.VMEM((rows, 1), F32), pltpu.VMEM((rows, 1), F32),
                        pltpu.VMEM((rows, KV_LORA), F32)],
        compiler_params=_params("parallel", "parallel"),
        name="attn_prompt",
    )(qlat, qrope, ckv_b, kr_b, w_uv)


PAGES_PER_STEP = 16
NEW_KEY_ROWS = 16


def _attn_sample_kernel(pt_ref, ql_ref, qr_ref, nc_ref, nr_ref, wuv_ref, *rest, t_new):
    del pt_ref
    pages_c = rest[:PAGES_PER_STEP]
    pages_r = rest[PAGES_PER_STEP:2 * PAGES_PER_STEP]
    o_ref, m_ref, l_ref, acc_ref = rest[2 * PAGES_PER_STEP:]
    j = pl.program_id(1)

    @pl.when(j == 0)
    def _():
        m_ref[...] = jnp.full_like(m_ref, NEG_BIG)
        l_ref[...] = jnp.zeros_like(l_ref)
        acc_ref[...] = jnp.zeros_like(acc_ref)

    ql = ql_ref[...]
    qr = qr_ref[...]
    kcs = [p[...].astype(BF16) for p in pages_c]
    s = jnp.concatenate(
        [_dot_nt(ql, kc) + _dot_nt(qr, pr[...].astype(BF16)) for kc, pr in zip(kcs, pages_r)],
        axis=1) * ATTN_SCALE
    page = kcs[0].shape[0]

    def pv(p):
        out = _dot(p[:, :page], kcs[0])
        for i in range(1, PAGES_PER_STEP):
            out = out + _dot(p[:, i * page:(i + 1) * page], kcs[i])
        return out

    _softmax_update(s, pv, m_ref, l_ref, acc_ref)

    @pl.when(j == pl.num_programs(1) - 1)
    def _():
        nc = nc_ref[...]
        s2 = (_dot_nt(ql, nc) + _dot_nt(qr, nr_ref[...])) * ATTN_SCALE
        shift = t_new.bit_length() - 1
        t_q = lax.broadcasted_iota(jnp.int32, s2.shape, 0) & (t_new - 1)
        t_k = lax.broadcasted_iota(jnp.int32, s2.shape, 1)
        s2 = jnp.where(t_k <= t_q, s2, NEG_BIG)
        _softmax_update(s2, lambda p: _dot(p, nc), m_ref, l_ref, acc_ref)
        o_lat = (acc_ref[...] / l_ref[...]).astype(BF16)
        head = lax.broadcasted_iota(jnp.int32, (o_lat.shape[0], V_DIM), 0) >> shift
        out = jnp.zeros((o_lat.shape[0], V_DIM), F32)
        for h in range(MLA_HEADS):
            out = out + jnp.where(head == h, _dot(o_lat, wuv_ref[h]), 0.0)
        o_ref[...] = out


def _attn_sample(ql, qr, new_c, new_r, w_uv, cache_c, cache_r, page_table, t_new):
    bsz, rows, _ = ql.shape
    n_pages = page_table.shape[1]
    assert n_pages % PAGES_PER_STEP == 0
    assert t_new & (t_new - 1) == 0 and t_new <= NEW_KEY_ROWS
    page = cache_c.shape[1]
    per_b = lambda w: pl.BlockSpec((None, rows, w), lambda b, j, pt: (b, 0, 0))
    new = lambda w: pl.BlockSpec((None, NEW_KEY_ROWS, w), lambda b, j, pt: (b, 0, 0))

    def page_spec(i, w):
        return pl.BlockSpec((None, page, w), lambda b, j, pt: (pt[b, j * PAGES_PER_STEP + i], 0, 0))

    grid_spec = pltpu.PrefetchScalarGridSpec(
        num_scalar_prefetch=1,
        grid=(bsz, n_pages // PAGES_PER_STEP),
        in_specs=[per_b(KV_LORA), per_b(QK_ROPE), new(KV_LORA), new(QK_ROPE),
                  pl.BlockSpec(w_uv.shape, lambda b, j, pt: (0, 0, 0))]
        + [page_spec(i, KV_LORA) for i in range(PAGES_PER_STEP)]
        + [page_spec(i, QK_ROPE) for i in range(PAGES_PER_STEP)],
        out_specs=pl.BlockSpec((None, rows, V_DIM), lambda b, j, pt: (b, 0, 0)),
        scratch_shapes=[pltpu.VMEM((rows, 1), F32), pltpu.VMEM((rows, 1), F32),
                        pltpu.VMEM((rows, KV_LORA), F32)],
    )
    return pl.pallas_call(
        functools.partial(_attn_sample_kernel, t_new=t_new),
        grid_spec=grid_spec,
        out_shape=jax.ShapeDtypeStruct((bsz, rows, V_DIM), F32),
        compiler_params=_params("parallel", "arbitrary"),
        name="attn_sample",
    )(page_table, ql, qr, new_c, new_r, w_uv, *([cache_c] * PAGES_PER_STEP), *([cache_r] * PAGES_PER_STEP))


def _outproj_kernel(h_ref, oh_ref, om_ref, w1_ref, w2_ref, g_ref, o_ref):
    m = _dot(oh_ref[...].astype(BF16), w1_ref[...]) + _dot(om_ref[...].astype(BF16), w2_ref[...])
    o_ref[...] = h_ref[...] + _rms(m, g_ref[...])


def _out_proj(h, o_h, o_m, w1, w2, g):
    m = h.shape[0]
    tm = min(m, 512)
    full = lambda a: pl.BlockSpec(a.shape, lambda i: (0,) * a.ndim)
    row = lambda width: pl.BlockSpec((tm, width), lambda i: (i, 0))
    return pl.pallas_call(
        _outproj_kernel,
        grid=(m // tm,),
        in_specs=[row(D_MODEL), row(HG_W), row(MLA_HEADS * V_DIM), full(w1), full(w2), full(g)],
        out_specs=row(D_MODEL),
        out_shape=jax.ShapeDtypeStruct((m, D_MODEL), F32),
        compiler_params=_params("parallel"),
        name="out_proj",
    )(h, o_h, o_m, w1, w2, g)


def _rot_cols(w):
    half = w.shape[-1] // 2
    return jnp.concatenate([-w[..., half:], w[..., :half]], axis=-1)


def _prep_weights(norm_g, w_ffn_gate, w_ffn_up, w_ffn_down, w_in, q_norm_g, w_q_up, kv_norm_g,
                  w_kv_up, hg_norm_g, w_out):
    bf = lambda a: a.astype(BF16)
    w_h, w_cq, w_ckv, w_kr = jnp.split(w_in, [4 * HG_W, 4 * HG_W + Q_LORA, 4 * HG_W + Q_LORA + KV_LORA], axis=-1)
    w_q_rope = w_q_up[..., QK_NOPE:]
    return {
        "norm_g": norm_g.reshape(-1, 1, D_MODEL),
        "wg": bf(w_ffn_gate), "wu": bf(w_ffn_up), "wd": bf(w_ffn_down),
        "w_h": bf(w_h), "w_cq": bf(w_cq), "w_ckv": bf(w_ckv),
        "w_kr": bf(jnp.concatenate([w_kr, _rot_cols(w_kr)], axis=-1)),
        "q_norm_g": q_norm_g.reshape(1, Q_LORA), "kv_norm_g": kv_norm_g.reshape(1, KV_LORA),
        "w_qn": bf(w_q_up[..., :QK_NOPE].reshape(Q_LORA, MLA_HEADS * QK_NOPE)),
        "w_qr": bf(w_q_rope.reshape(Q_LORA, MLA_HEADS * QK_ROPE)),
        "w_qrr": bf(_rot_cols(w_q_rope).reshape(Q_LORA, MLA_HEADS * QK_ROPE)),
        "w_uk": bf(jnp.transpose(w_kv_up[..., :QK_NOPE], (1, 2, 0))),
        "w_uv": bf(jnp.transpose(w_kv_up[..., QK_NOPE:], (1, 0, 2))),
        "hg_norm_g": hg_norm_g.reshape(1, HG_DV),
        "w_out1": bf(w_out[:HG_W]), "w_out2": bf(w_out[HG_W:]),
    }


def _rope_tables(pos):
    half = QK_ROPE // 2
    inv = ROPE_THETA ** (-jnp.arange(half, dtype=F32) / half)
    ang = pos.astype(F32)[:, None] * inv[None, :]
    cos = jnp.tile(jnp.cos(ang), (1, 2))
    sin = jnp.tile(jnp.sin(ang), (1, 2))
    return (jnp.concatenate([cos, sin], axis=-1), jnp.tile(cos, (1, MLA_HEADS)), jnp.tile(sin, (1, MLA_HEADS)))


def kernel(x_prompt, x_sample, cache_kv_latent, cache_k_rope, state_hgrn, page_table, hgrn_lb_logits,
           norm_g, w_ffn_gate, w_ffn_up, w_ffn_down, w_in, q_norm_g, w_q_up, kv_norm_g, w_kv_up,
           hg_norm_g, w_out):
    bp, tp, _ = x_prompt.shape
    bs, ts, _ = x_sample.shape
    depth = norm_g.shape[0]
    assert depth == 1
    past_len = page_table.shape[1] * cache_kv_latent.shape[2]
    w = _prep_weights(norm_g[0], w_ffn_gate[0], w_ffn_up[0], w_ffn_down[0], w_in[0], q_norm_g[0],
                      w_q_up[0], kv_norm_g[0], w_kv_up[0], hg_norm_g[0], w_out[0])
    ng = w["norm_g"]
    lb_logits = hgrn_lb_logits.astype(F32)

    def pre(x, tables):
        h1 = _ffn_block(x, ng[0], ng[1], w["wg"][0], w["wu"][0], w["wd"][0])
        return (h1,) + tuple(_in_proj(h1, ng[2], *tables, w))

    def post(h1, o_h, o_m):
        h2 = _out_proj(h1, o_h, o_m, w["w_out1"], w["w_out2"], ng[3])
        return _ffn_block(h2, ng[4], ng[5], w["wg"][1], w["wu"][1], w["wd"][1])

    h1, zh, ckv, kr, ckv_b, kr_b, qlat, qrope = pre(
        x_prompt.reshape(bp * tp, D_MODEL), _rope_tables(jnp.arange(tp, dtype=jnp.int32)))
    o_h, st_p = _hgrn(zh.reshape(bp, tp, 4 * HG_W), lb_logits, w["hg_norm_g"], None, chunk=64, t_valid=tp)
    o_m = _attn_prompt(qlat, qrope, ckv_b, kr_b, w["w_uv"], bp, tp)
    y_p = post(h1, o_h.reshape(bp * tp, HG_W), o_m).reshape(bp, tp, D_MODEL)
    ckv_p = ckv.reshape(1, bp, tp, KV_LORA)
    kr_p = kr.reshape(1, bp, tp, QK_ROPE)

    pos_s = past_len + jnp.arange(ts, dtype=jnp.int32)
    tabs = tuple(jnp.tile(a, (bs, 1)) for a in _rope_tables(pos_s))
    h1s, zhs, ckvs, krs, ckvs_b, krs_b, qlats, qropes = pre(x_sample.reshape(bs * ts, D_MODEL), tabs)
    t_pad = -(-ts // SUBLANES) * SUBLANES
    zhs_pad = jnp.pad(zhs.reshape(bs, ts, 4 * HG_W), ((0, 0), (0, t_pad - ts), (0, 0)))
    o_hs, st_s = _hgrn(zhs_pad, lb_logits, w["hg_norm_g"], state_hgrn[0], chunk=SUBLANES, t_valid=ts)
    heads_first = lambda a, wd: a.reshape(bs, ts, MLA_HEADS, wd).transpose(0, 2, 1, 3).reshape(bs, MLA_HEADS * ts, wd)
    pad_new = lambda a, wd: jnp.pad(a.reshape(bs, ts, wd), ((0, 0), (0, NEW_KEY_ROWS - ts), (0, 0)))
    o_ms = _attn_sample(heads_first(qlats, KV_LORA), heads_first(qropes, QK_ROPE),
                        pad_new(ckvs_b, KV_LORA), pad_new(krs_b, QK_ROPE), w["w_uv"],
                        cache_kv_latent[0], cache_k_rope[0], page_table, ts)
    o_ms = o_ms.reshape(bs, MLA_HEADS, ts, V_DIM).transpose(0, 2, 1, 3).reshape(bs * ts, MLA_HEADS * V_DIM)
    y_s = post(h1s, o_hs[:, :ts].reshape(bs * ts, HG_W), o_ms).reshape(bs, ts, D_MODEL)

    return (y_p, y_s, ckv_p, kr_p, st_p[None], ckvs.reshape(1, bs, ts, KV_LORA),
            krs.reshape(1, bs, ts, QK_ROPE), st_s[None])
```

```python
import functools
import math

import jax
import jax.numpy as jnp
from jax import lax
from jax.experimental import pallas as pl
from jax.experimental.pallas import tpu as pltpu

D_MODEL = 1024
D_FF = 2816
HG_HEADS = 4
HG_DK = 128
HG_DV = 128
MLA_HEADS = 4
QK_NOPE = 128
QK_ROPE = 64
V_DIM = 128
Q_LORA = 768
KV_LORA = 256
ROPE_THETA = 10000.0
EPS = 1e-6
HG_W = HG_HEADS * HG_DK

VMEM_LIMIT_BYTES = 56 * 1024 * 1024
SUBLANES = 8
NEG_BIG = -1e30
LOG2E = math.log2(math.e)

F32 = jnp.float32
BF16 = jnp.bfloat16


def _dot(a, b):
    return jnp.dot(a, b, preferred_element_type=F32)


def _dot_nt(a, b):
    return lax.dot_general(a, b, (((1,), (1,)), ((), ())), preferred_element_type=F32)


def _dot_tn(a, b):
    return lax.dot_general(a, b, (((0,), (0,)), ((), ())), preferred_element_type=F32)


def _rms(x, g):
    return x * lax.rsqrt(jnp.mean(x * x, axis=-1, keepdims=True) + EPS) * g


def _sigmoid(x):
    return 1.0 / (1.0 + jnp.exp(-x))


def _silu(x):
    return x * _sigmoid(x)


def _params(*sem):
    return pltpu.CompilerParams(dimension_semantics=sem, vmem_limit_bytes=VMEM_LIMIT_BYTES)


def _ffn_kernel(x_ref, gpre_ref, gpost_ref, wg_ref, wu_ref, wd_ref, o_ref, xn_ref, acc_ref):
    f = pl.program_id(1)

    @pl.when(f == 0)
    def _():
        xn_ref[...] = _rms(x_ref[...], gpre_ref[...]).astype(BF16)
        acc_ref[...] = jnp.zeros_like(acc_ref)

    xn = xn_ref[...]
    a = _dot(xn, wg_ref[...])
    b = _dot(xn, wu_ref[...])
    h = (_silu(a) * b).astype(BF16)
    acc_ref[...] += _dot(h, wd_ref[...])

    @pl.when(f == pl.num_programs(1) - 1)
    def _():
        o_ref[...] = x_ref[...] + 0.5 * _rms(acc_ref[...], gpost_ref[...])


def _ffn_block(x, g_pre, g_post, wg, wu, wd):
    m = x.shape[0]
    tm = min(m, 1024)
    tf = 256
    return pl.pallas_call(
        _ffn_kernel,
        grid=(m // tm, D_FF // tf),
        in_specs=[
            pl.BlockSpec((tm, D_MODEL), lambda i, f: (i, 0)),
            pl.BlockSpec((1, D_MODEL), lambda i, f: (0, 0)),
            pl.BlockSpec((1, D_MODEL), lambda i, f: (0, 0)),
            pl.BlockSpec((D_MODEL, tf), lambda i, f: (0, f)),
            pl.BlockSpec((D_MODEL, tf), lambda i, f: (0, f)),
            pl.BlockSpec((tf, D_MODEL), lambda i, f: (f, 0)),
        ],
        out_specs=pl.BlockSpec((tm, D_MODEL), lambda i, f: (i, 0)),
        out_shape=jax.ShapeDtypeStruct((m, D_MODEL), F32),
        scratch_shapes=[pltpu.VMEM((tm, D_MODEL), BF16), pltpu.VMEM((tm, D_MODEL), F32)],
        compiler_params=_params("parallel", "arbitrary"),
        name="ffn_block",
    )(x, g_pre, g_post, wg, wu, wd)


def _inproj_kernel(x_ref, g_ref, csk_ref, cosq_ref, sinq_ref, wh_ref, wcq_ref, wckv_ref, wkr_ref,
                   qg_ref, kvg_ref, wqn_ref, wqr_ref, wqrr_ref, wuk_ref,
                   zh_ref, ckv_ref, kr_ref, ckvb_ref, krb_ref, qlat_ref, qrope_ref):
    u = _rms(x_ref[...], g_ref[...]).astype(BF16)
    zh_ref[...] = _dot(u, wh_ref[...])
    ckv = _rms(_dot(u, wckv_ref[...]), kvg_ref[...])
    ckv_ref[...] = ckv
    ckvb_ref[...] = ckv.astype(BF16)
    kz = _dot(u, wkr_ref[...])
    csk = csk_ref[...]
    kr = kz[:, :QK_ROPE] * csk[:, :QK_ROPE] + kz[:, QK_ROPE:] * csk[:, QK_ROPE:]
    kr_ref[...] = kr
    krb_ref[...] = kr.astype(BF16)
    cqn = _rms(_dot(u, wcq_ref[...]), qg_ref[...]).astype(BF16)
    qn = _dot(cqn, wqn_ref[...]).astype(BF16)
    for h in range(MLA_HEADS):
        qlat_ref[:, h * KV_LORA:(h + 1) * KV_LORA] = _dot(
            qn[:, h * QK_NOPE:(h + 1) * QK_NOPE], wuk_ref[h]).astype(BF16)
    qr = _dot(cqn, wqr_ref[...]) * cosq_ref[...] + _dot(cqn, wqrr_ref[...]) * sinq_ref[...]
    qrope_ref[...] = qr.astype(BF16)


def _in_proj(x, g, csk, cosq, sinq, w):
    m = x.shape[0]
    tm = min(m, 256)
    nrep = csk.shape[0] // tm
    full = lambda a: pl.BlockSpec(a.shape, lambda i: (0,) * a.ndim)
    row = lambda width: pl.BlockSpec((tm, width), lambda i: (i, 0))
    tab = lambda width: pl.BlockSpec((tm, width), lambda i: (i % nrep, 0))
    rope_w = MLA_HEADS * QK_ROPE
    return pl.pallas_call(
        _inproj_kernel,
        grid=(m // tm,),
        in_specs=[row(D_MODEL), full(g), tab(2 * QK_ROPE), tab(rope_w), tab(rope_w),
                  full(w["w_h"]), full(w["w_cq"]), full(w["w_ckv"]), full(w["w_kr"]),
                  full(w["q_norm_g"]), full(w["kv_norm_g"]), full(w["w_qn"]), full(w["w_qr"]),
                  full(w["w_qrr"]), full(w["w_uk"])],
        out_specs=[row(4 * HG_W), row(KV_LORA), row(QK_ROPE), row(KV_LORA), row(QK_ROPE),
                   row(MLA_HEADS * KV_LORA), row(rope_w)],
        out_shape=[
            jax.ShapeDtypeStruct((m, 4 * HG_W), F32),
            jax.ShapeDtypeStruct((m, KV_LORA), F32),
            jax.ShapeDtypeStruct((m, QK_ROPE), F32),
            jax.ShapeDtypeStruct((m, KV_LORA), BF16),
            jax.ShapeDtypeStruct((m, QK_ROPE), BF16),
            jax.ShapeDtypeStruct((m, MLA_HEADS * KV_LORA), BF16),
            jax.ShapeDtypeStruct((m, rope_w), BF16),
        ],
        compiler_params=_params("parallel"),
        name="in_proj",
    )(x, g, csk, cosq, sinq, w["w_h"], w["w_cq"], w["w_ckv"], w["w_kr"], w["q_norm_g"],
      w["kv_norm_g"], w["w_qn"], w["w_qr"], w["w_qrr"], w["w_uk"])


def _split3(g):
    g1 = g.astype(BF16).astype(F32)
    r1 = g - g1
    g2 = r1.astype(BF16).astype(F32)
    g3 = (r1 - g2).astype(BF16).astype(F32)
    return g1, g2, g3


def _hgrn_kernel(*refs, chunk, t_valid, has_s0):
    if has_s0:
        hq_ref, hf_ref, hi_ref, hg_ref, lbl_ref, ng_ref, s0_ref = refs[:7]
        refs = refs[7:]
    else:
        hq_ref, hf_ref, hi_ref, hg_ref, lbl_ref, ng_ref = refs[:6]
        s0_ref = None
        refs = refs[6:]
    o_ref, sfin_ref, st_ref, c_ref, v_ref = refs
    ti = pl.program_id(1)
    t_tile = hq_ref.shape[0]
    n_chunks = t_tile // chunk
    n_tiles = chunk // SUBLANES

    lbl = lbl_ref[...]
    e = jnp.exp(lbl - jnp.max(lbl, axis=0, keepdims=True))
    lb_all = e[0:1, :] / jnp.sum(e, axis=0, keepdims=True)

    @pl.when(ti == 0)
    def _():
        for h in range(HG_HEADS):
            st_ref[h] = s0_ref[h].T if has_s0 else jnp.zeros((HG_DV, HG_DK), F32)

    r_i = lax.broadcasted_iota(jnp.int32, (chunk, chunk), 0)
    c_i = lax.broadcasted_iota(jnp.int32, (chunk, chunk), 1)
    tril = (c_i <= r_i).astype(F32)
    row8 = lax.broadcasted_iota(jnp.int32, (SUBLANES, HG_DK), 0)
    ng = ng_ref[...]
    ones = jnp.ones((HG_DK, HG_DV), BF16)
    pairs = [(s, i) for s in range(chunk) for i in range(s // SUBLANES, n_tiles)]

    def head_chunk(h, start):
        sl = pl.ds(start, chunk)
        hl = slice(h * HG_DK, (h + 1) * HG_DK)
        lb = lb_all[:, hl]
        q = _silu(hq_ref[sl, hl])
        f = lb + (1.0 - lb) * _sigmoid(hf_ref[sl, hl])
        g = jnp.log(f)
        k = 1.0 - f
        v = hi_ref[sl, hl]
        if t_valid is not None:
            pos = lax.broadcasted_iota(jnp.int32, (chunk, HG_DK), 0) + (ti * t_tile + start)
            valid = pos < t_valid
            g = jnp.where(valid, g, 0.0)
            k = jnp.where(valid, k, 0.0)
            v = jnp.where(valid, v, 0.0)
        g1, g2, g3 = _split3(g)
        b = _dot(tril, g1) + _dot(tril, g2) + _dot(tril, g3)
        b2 = b * LOG2E
        c_ref[h] = b2 - jnp.log(k) * LOG2E
        v_ref[h] = v

        q_t = [q[i * SUBLANES:(i + 1) * SUBLANES, :] for i in range(n_tiles)]
        b_t = [b2[i * SUBLANES:(i + 1) * SUBLANES, :] for i in range(n_tiles)]
        p_tiles = []
        for s, i in pairs:
            d = b_t[i] - c_ref[h, s:s + 1, :]
            if i == s // SUBLANES and s % SUBLANES:
                d = jnp.where(row8 >= s % SUBLANES, d, NEG_BIG)
            p_tiles.append(q_t[i] * jnp.exp2(d))
        a_rep = _dot(jnp.concatenate(p_tiles, axis=0).astype(BF16), ones)
        o_t = [jnp.zeros((SUBLANES, HG_DV), F32) for _ in range(n_tiles)]
        for j, (s, i) in enumerate(pairs):
            o_t[i] = o_t[i] + a_rep[j * SUBLANES:(j + 1) * SUBLANES, :] * v_ref[h, s:s + 1, :]
        o = jnp.concatenate(o_t, axis=0) if n_tiles > 1 else o_t[0]

        st = st_ref[h]
        o = o + _dot_nt((q * jnp.exp(b)).astype(BF16), st.astype(BF16))
        bl = b[chunk - 1:chunk, :]
        kd = (k * jnp.exp(bl - b)).astype(BF16)
        st_ref[h] = st * jnp.exp(bl) + _dot_tn(v.astype(BF16), kd)

        o_ref[sl, hl] = _rms(o, ng) * _silu(hg_ref[sl, hl])

    def chunk_body(c, carry):
        start = pl.multiple_of(c * chunk, chunk)
        for h in range(HG_HEADS):
            head_chunk(h, start)
        return carry

    lax.fori_loop(0, n_chunks, chunk_body, 0)

    @pl.when(ti == pl.num_programs(1) - 1)
    def _():
        for h in range(HG_HEADS):
            sfin_ref[h] = st_ref[h].T


def _hgrn(zh, lb_logits, ng, s0, chunk, t_valid):
    bsz, t, _ = zh.shape
    has_s0 = s0 is not None
    t_tile = min(t, 512)
    assert t % t_tile == 0 and t_tile % chunk == 0
    col = lambda j: pl.BlockSpec((None, t_tile, HG_W), lambda b, i: (b, i, j))
    in_specs = [col(0), col(1), col(2), col(3),
                pl.BlockSpec(lb_logits.shape, lambda b, i: (0, 0)),
                pl.BlockSpec((1, HG_DV), lambda b, i: (0, 0))]
    args = [zh, zh, zh, zh, lb_logits, ng]
    st_spec = pl.BlockSpec((None, HG_HEADS, HG_DK, HG_DV), lambda b, i: (b, 0, 0, 0))
    if has_s0:
        in_specs.append(st_spec)
        args.append(s0)
    return pl.pallas_call(
        functools.partial(_hgrn_kernel, chunk=chunk, t_valid=t_valid, has_s0=has_s0),
        grid=(bsz, t // t_tile),
        in_specs=in_specs,
        out_specs=[pl.BlockSpec((None, t_tile, HG_W), lambda b, i: (b, i, 0)), st_spec],
        out_shape=[jax.ShapeDtypeStruct((bsz, t, HG_W), F32),
                   jax.ShapeDtypeStruct((bsz, HG_HEADS, HG_DK, HG_DV), F32)],
        scratch_shapes=[pltpu.VMEM((HG_HEADS, HG_DV, HG_DK), F32),
                        pltpu.VMEM((HG_HEADS, chunk, HG_DK), F32),
                        pltpu.VMEM((HG_HEADS, chunk, HG_DV), F32)],
        compiler_params=_params("parallel", "arbitrary"),
        name="hgrn",
    )(*args)


ATTN_SCALE = 1.0 / math.sqrt(QK_NOPE + QK_ROPE)


LANES = 128


def _attn_prompt_kernel(ql_ref, qr_ref, kc_ref, kr_ref, wuv_ref, o_ref, m_ref, l_ref, acc_ref, *, tq):
    qi = pl.program_id(1)
    ql = jnp.concatenate([ql_ref[:, h * KV_LORA:(h + 1) * KV_LORA] for h in range(MLA_HEADS)], axis=0)
    qr = jnp.concatenate([qr_ref[:, h * QK_ROPE:(h + 1) * QK_ROPE] for h in range(MLA_HEADS)], axis=0)
    m_ref[...] = jnp.full_like(m_ref, NEG_BIG)
    l_ref[...] = jnp.zeros_like(l_ref)
    acc_ref[...] = jnp.zeros_like(acc_ref)

    def step(kb, masked):
        sl = pl.ds(pl.multiple_of(kb * tq, tq), tq)
        kc = kc_ref[sl, :]
        s = (_dot_nt(ql, kc) + _dot_nt(qr, kr_ref[sl, :])) * ATTN_SCALE
        if masked:
            rows = lax.broadcasted_iota(jnp.int32, s.shape, 0) & (tq - 1)
            cols = lax.broadcasted_iota(jnp.int32, s.shape, 1)
            s = jnp.where(cols <= rows, s, NEG_BIG)
        m_prev = m_ref[...]
        m_new = jnp.maximum(m_prev, jnp.max(s, axis=-1, keepdims=True))
        alpha = jnp.exp(m_prev - m_new)
        p = jnp.exp(s - jnp.tile(m_new, (1, tq // LANES)))
        l_ref[...] = alpha * l_ref[...] + jnp.sum(p, axis=-1, keepdims=True)
        acc_ref[...] = jnp.tile(alpha, (1, KV_LORA // LANES)) * acc_ref[...] + _dot(p.astype(BF16), kc)
        m_ref[...] = m_new

    def body(kb, carry):
        step(kb, False)
        return carry

    lax.fori_loop(0, qi, body, 0)
    step(qi, True)
    o_lat = (acc_ref[...] / jnp.tile(l_ref[...], (1, KV_LORA // LANES))).astype(BF16)
    for h in range(MLA_HEADS):
        o_ref[:, h * V_DIM:(h + 1) * V_DIM] = _dot(o_lat[h * tq:(h + 1) * tq, :], wuv_ref[h])


def _attn_prompt(qlat, qrope, ckv_b, kr_b, w_uv, bsz, t):
    tq = 256
    assert tq & (tq - 1) == 0 and t % tq == 0
    nq = t // tq
    rows = MLA_HEADS * tq
    return pl.pallas_call(
        functools.partial(_attn_prompt_kernel, tq=tq),
        grid=(bsz, nq),
        in_specs=[
            pl.BlockSpec((tq, MLA_HEADS * KV_LORA), lambda b, i: (b * nq + i, 0)),
            pl.BlockSpec((tq, MLA_HEADS * QK_ROPE), lambda b, i: (b * nq + i, 0)),
            pl.BlockSpec((t, KV_LORA), lambda b, i: (b, 0)),
            pl.BlockSpec((t, QK_ROPE), lambda b, i: (b, 0)),
            pl.BlockSpec(w_uv.shape, lambda b, i: (0, 0, 0)),
        ],
        out_specs=pl.BlockSpec((tq, MLA_HEADS * V_DIM), lambda b, i: (b * nq + i, 0)),
        out_shape=jax.ShapeDtypeStruct((bsz * t, MLA_HEADS * V_DIM), F32),
        scratch_shapes=[pltpu.VMEM((rows, LANES), F32), pltpu.VMEM((rows, LANES), F32),
                        pltpu.VMEM((rows, KV_LORA), F32)],
        compiler_params=_params("parallel", "parallel"),
        name="attn_prompt",
    )(qlat, qrope, ckv_b, kr_b, w_uv)


PAGES_PER_STEP = 16
NEW_KEY_ROWS = 16


def _attn_sample_kernel(pt_ref, ql_ref, qr_ref, nc_ref, nr_ref, wuv_ref, *rest, t_new):
    del pt_ref
    pages_c = rest[:PAGES_PER_STEP]
    pages_r = rest[PAGES_PER_STEP:2 * PAGES_PER_STEP]
    o_ref, m_ref, l_ref, acc_ref, kc_ref, kr_ref = rest[2 * PAGES_PER_STEP:]
    j = pl.program_id(1)

    @pl.when(j == 0)
    def _():
        m_ref[...] = jnp.full_like(m_ref, NEG_BIG)
        l_ref[...] = jnp.zeros_like(l_ref)
        acc_ref[...] = jnp.zeros_like(acc_ref)

    ql_t = ql_ref[...]
    qr_t = qr_ref[...]
    page = pages_c[0].shape[0]
    for i in range(PAGES_PER_STEP):
        kc_ref[i * page:(i + 1) * page, :] = pages_c[i][...].astype(BF16)
        kr_ref[:, i * page:(i + 1) * page] = pages_r[i][...].astype(BF16)

    def update(s, values):
        m_prev = m_ref[...]
        m_new = jnp.maximum(m_prev, jnp.max(s, axis=0, keepdims=True))
        alpha = jnp.exp(m_prev - m_new)
        p = jnp.exp(s - m_new)
        l_ref[...] = alpha * l_ref[...] + jnp.sum(p, axis=0, keepdims=True)
        acc_ref[...] = alpha * acc_ref[...] + _dot_tn(values, p.astype(BF16))
        m_ref[...] = m_new

    kc = kc_ref[...]
    update((_dot(kc, ql_t) + _dot_tn(kr_ref[...], qr_t)) * ATTN_SCALE, kc)

    @pl.when(j == pl.num_programs(1) - 1)
    def _():
        nc = nc_ref[...]
        s2 = (_dot(nc, ql_t) + _dot(nr_ref[...], qr_t)) * ATTN_SCALE
        shift = t_new.bit_length() - 1
        t_k = lax.broadcasted_iota(jnp.int32, s2.shape, 0)
        t_q = lax.broadcasted_iota(jnp.int32, s2.shape, 1) & (t_new - 1)
        update(jnp.where(t_k <= t_q, s2, NEG_BIG), nc)
        o_lat_t = (acc_ref[...] / l_ref[...]).astype(BF16)
        n_q = o_lat_t.shape[1]
        head = lax.broadcasted_iota(jnp.int32, (n_q, V_DIM), 0) >> shift
        out = jnp.zeros((n_q, V_DIM), F32)
        for h in range(MLA_HEADS):
            out = out + jnp.where(head == h, _dot_tn(o_lat_t, wuv_ref[h]), 0.0)
        o_ref[...] = out


def _attn_sample(ql_t, qr_t, new_c, new_r, w_uv, cache_c, cache_r_t, page_table, t_new):
    bsz, _, n_q = ql_t.shape
    n_pages = page_table.shape[1]
    assert n_pages % PAGES_PER_STEP == 0
    assert t_new & (t_new - 1) == 0 and t_new <= NEW_KEY_ROWS
    page = cache_c.shape[1]
    keys = PAGES_PER_STEP * page
    per_b = lambda a: pl.BlockSpec((None,) + a.shape[1:], lambda b, j, pt: (b, 0, 0))

    def page_spec(i, a):
        return pl.BlockSpec((None,) + a.shape[1:], lambda b, j, pt: (pt[b, j * PAGES_PER_STEP + i], 0, 0))

    grid_spec = pltpu.PrefetchScalarGridSpec(
        num_scalar_prefetch=1,
        grid=(bsz, n_pages // PAGES_PER_STEP),
        in_specs=[per_b(ql_t), per_b(qr_t), per_b(new_c), per_b(new_r),
                  pl.BlockSpec(w_uv.shape, lambda b, j, pt: (0, 0, 0))]
        + [page_spec(i, cache_c) for i in range(PAGES_PER_STEP)]
        + [page_spec(i, cache_r_t) for i in range(PAGES_PER_STEP)],
        out_specs=pl.BlockSpec((None, n_q, V_DIM), lambda b, j, pt: (b, 0, 0)),
        scratch_shapes=[pltpu.VMEM((1, n_q), F32), pltpu.VMEM((1, n_q), F32),
                        pltpu.VMEM((KV_LORA, n_q), F32),
                        pltpu.VMEM((keys, KV_LORA), BF16), pltpu.VMEM((QK_ROPE, keys), BF16)],
    )
    return pl.pallas_call(
        functools.partial(_attn_sample_kernel, t_new=t_new),
        grid_spec=grid_spec,
        out_shape=jax.ShapeDtypeStruct((bsz, n_q, V_DIM), F32),
        compiler_params=_params("parallel", "arbitrary"),
        name="attn_sample",
    )(page_table, ql_t, qr_t, new_c, new_r, w_uv, *([cache_c] * PAGES_PER_STEP),
      *([cache_r_t] * PAGES_PER_STEP))


def _outproj_kernel(h_ref, oh_ref, om_ref, w1_ref, w2_ref, g_ref, o_ref):
    m = _dot(oh_ref[...].astype(BF16), w1_ref[...]) + _dot(om_ref[...].astype(BF16), w2_ref[...])
    o_ref[...] = h_ref[...] + _rms(m, g_ref[...])


def _out_proj(h, o_h, o_m, w1, w2, g):
    m = h.shape[0]
    tm = min(m, 512)
    full = lambda a: pl.BlockSpec(a.shape, lambda i: (0,) * a.ndim)
    row = lambda width: pl.BlockSpec((tm, width), lambda i: (i, 0))
    return pl.pallas_call(
        _outproj_kernel,
        grid=(m // tm,),
        in_specs=[row(D_MODEL), row(HG_W), row(MLA_HEADS * V_DIM), full(w1), full(w2), full(g)],
        out_specs=row(D_MODEL),
        out_shape=jax.ShapeDtypeStruct((m, D_MODEL), F32),
        compiler_params=_params("parallel"),
        name="out_proj",
    )(h, o_h, o_m, w1, w2, g)


def _rot_cols(w):
    half = w.shape[-1] // 2
    return jnp.concatenate([-w[..., half:], w[..., :half]], axis=-1)


def _prep_weights(norm_g, w_ffn_gate, w_ffn_up, w_ffn_down, w_in, q_norm_g, w_q_up, kv_norm_g,
                  w_kv_up, hg_norm_g, w_out):
    bf = lambda a: a.astype(BF16)
    w_h, w_cq, w_ckv, w_kr = jnp.split(w_in, [4 * HG_W, 4 * HG_W + Q_LORA, 4 * HG_W + Q_LORA + KV_LORA], axis=-1)
    w_q_rope = w_q_up[..., QK_NOPE:]
    return {
        "norm_g": norm_g.reshape(-1, 1, D_MODEL),
        "wg": bf(w_ffn_gate), "wu": bf(w_ffn_up), "wd": bf(w_ffn_down),
        "w_h": bf(w_h), "w_cq": bf(w_cq), "w_ckv": bf(w_ckv),
        "w_kr": bf(jnp.concatenate([w_kr, _rot_cols(w_kr)], axis=-1)),
        "q_norm_g": q_norm_g.reshape(1, Q_LORA), "kv_norm_g": kv_norm_g.reshape(1, KV_LORA),
        "w_qn": bf(w_q_up[..., :QK_NOPE].reshape(Q_LORA, MLA_HEADS * QK_NOPE)),
        "w_qr": bf(w_q_rope.reshape(Q_LORA, MLA_HEADS * QK_ROPE)),
        "w_qrr": bf(_rot_cols(w_q_rope).reshape(Q_LORA, MLA_HEADS * QK_ROPE)),
        "w_uk": bf(jnp.transpose(w_kv_up[..., :QK_NOPE], (1, 2, 0))),
        "w_uv": bf(jnp.transpose(w_kv_up[..., QK_NOPE:], (1, 0, 2))),
        "hg_norm_g": hg_norm_g.reshape(1, HG_DV),
        "w_out1": bf(w_out[:HG_W]), "w_out2": bf(w_out[HG_W:]),
    }


def _rope_tables(pos):
    half = QK_ROPE // 2
    inv = ROPE_THETA ** (-jnp.arange(half, dtype=F32) / half)
    ang = pos.astype(F32)[:, None] * inv[None, :]
    cos = jnp.tile(jnp.cos(ang), (1, 2))
    sin = jnp.tile(jnp.sin(ang), (1, 2))
    return (jnp.concatenate([cos, sin], axis=-1), jnp.tile(cos, (1, MLA_HEADS)), jnp.tile(sin, (1, MLA_HEADS)))


def kernel(x_prompt, x_sample, cache_kv_latent, cache_k_rope, state_hgrn, page_table, hgrn_lb_logits,
           norm_g, w_ffn_gate, w_ffn_up, w_ffn_down, w_in, q_norm_g, w_q_up, kv_norm_g, w_kv_up,
           hg_norm_g, w_out):
    bp, tp, _ = x_prompt.shape
    bs, ts, _ = x_sample.shape
    depth = norm_g.shape[0]
    assert depth == 1
    past_len = page_table.shape[1] * cache_kv_latent.shape[2]
    w = _prep_weights(norm_g[0], w_ffn_gate[0], w_ffn_up[0], w_ffn_down[0], w_in[0], q_norm_g[0],
                      w_q_up[0], kv_norm_g[0], w_kv_up[0], hg_norm_g[0], w_out[0])
    ng = w["norm_g"]
    lb_logits = hgrn_lb_logits.astype(F32)

    def pre(x, tables):
        h1 = _ffn_block(x, ng[0], ng[1], w["wg"][0], w["wu"][0], w["wd"][0])
        return (h1,) + tuple(_in_proj(h1, ng[2], *tables, w))

    def post(h1, o_h, o_m):
        h2 = _out_proj(h1, o_h, o_m, w["w_out1"], w["w_out2"], ng[3])
        return _ffn_block(h2, ng[4], ng[5], w["wg"][1], w["wu"][1], w["wd"][1])

    h1, zh, ckv, kr, ckv_b, kr_b, qlat, qrope = pre(
        x_prompt.reshape(bp * tp, D_MODEL), _rope_tables(jnp.arange(tp, dtype=jnp.int32)))
    o_h, st_p = _hgrn(zh.reshape(bp, tp, 4 * HG_W), lb_logits, w["hg_norm_g"], None, chunk=64, t_valid=None)
    o_m = _attn_prompt(qlat, qrope, ckv_b, kr_b, w["w_uv"], bp, tp)
    y_p = post(h1, o_h.reshape(bp * tp, HG_W), o_m).reshape(bp, tp, D_MODEL)
    ckv_p = ckv.reshape(1, bp, tp, KV_LORA)
    kr_p = kr.reshape(1, bp, tp, QK_ROPE)

    pos_s = past_len + jnp.arange(ts, dtype=jnp.int32)
    tabs = tuple(jnp.tile(a, (bs, 1)) for a in _rope_tables(pos_s))
    h1s, zhs, ckvs, krs, ckvs_b, krs_b, qlats, qropes = pre(x_sample.reshape(bs * ts, D_MODEL), tabs)
    t_pad = -(-ts // SUBLANES) * SUBLANES
    zhs_pad = jnp.pad(zhs.reshape(bs, ts, 4 * HG_W), ((0, 0), (0, t_pad - ts), (0, 0)))
    o_hs, st_s = _hgrn(zhs_pad, lb_logits, w["hg_norm_g"], state_hgrn[0], chunk=SUBLANES,
                       t_valid=ts if t_pad != ts else None)
    q_cols = lambda a, wd: a.reshape(bs, ts, MLA_HEADS, wd).transpose(0, 3, 2, 1).reshape(bs, wd, MLA_HEADS * ts)
    pad_new = lambda a, wd: jnp.pad(a.reshape(bs, ts, wd), ((0, 0), (0, NEW_KEY_ROWS - ts), (0, 0)))
    o_ms = _attn_sample(q_cols(qlats, KV_LORA), q_cols(qropes, QK_ROPE),
                        pad_new(ckvs_b, KV_LORA), pad_new(krs_b, QK_ROPE), w["w_uv"],
                        cache_kv_latent[0], jnp.swapaxes(cache_k_rope[0], 1, 2), page_table, ts)
    o_ms = o_ms.reshape(bs, MLA_HEADS, ts, V_DIM).transpose(0, 2, 1, 3).reshape(bs * ts, MLA_HEADS * V_DIM)
    y_s = post(h1s, o_hs[:, :ts].reshape(bs * ts, HG_W), o_ms).reshape(bs, ts, D_MODEL)

    return (y_p, y_s, ckv_p, kr_p, st_p[None], ckvs.reshape(1, bs, ts, KV_LORA),
            krs.reshape(1, bs, ts, QK_ROPE), st_s[None])
```

```python
import functools
import math

import jax
import jax.numpy as jnp
from jax import lax
from jax.experimental import pallas as pl
from jax.experimental.pallas import tpu as pltpu

D_MODEL = 1024
D_FF = 2816
HG_HEADS = 4
HG_DK = 128
HG_DV = 128
MLA_HEADS = 4
QK_NOPE = 128
QK_ROPE = 64
V_DIM = 128
Q_LORA = 768
KV_LORA = 256
ROPE_THETA = 10000.0
EPS = 1e-6
HG_W = HG_HEADS * HG_DK

VMEM_LIMIT_BYTES = 56 * 1024 * 1024
SUBLANES = 8
NEG_BIG = -1e30
LOG2E = math.log2(math.e)

F32 = jnp.float32
BF16 = jnp.bfloat16


def _dot(a, b):
    return jnp.dot(a, b, preferred_element_type=F32)


def _dot_nt(a, b):
    return lax.dot_general(a, b, (((1,), (1,)), ((), ())), preferred_element_type=F32)


def _dot_tn(a, b):
    return lax.dot_general(a, b, (((0,), (0,)), ((), ())), preferred_element_type=F32)


def _rms(x, g):
    return x * lax.rsqrt(jnp.mean(x * x, axis=-1, keepdims=True) + EPS) * g


def _sigmoid(x):
    return 1.0 / (1.0 + jnp.exp(-x))


def _silu(x):
    return x * _sigmoid(x)


def _params(*sem):
    return pltpu.CompilerParams(dimension_semantics=sem, vmem_limit_bytes=VMEM_LIMIT_BYTES)


FFN_CHUNK = 256


def _ffn_kernel(x_ref, gpre_ref, gpost_ref, wg_ref, wu_ref, wd_ref, o_ref):
    x = x_ref[...]
    xn = _rms(x, gpre_ref[...]).astype(BF16)
    acc = None
    for f in range(D_FF // FFN_CHUNK):
        cols = slice(f * FFN_CHUNK, (f + 1) * FFN_CHUNK)
        a = _dot(xn, wg_ref[:, cols])
        b = _dot(xn, wu_ref[:, cols])
        part = _dot((_silu(a) * b).astype(BF16), wd_ref[cols, :])
        acc = part if acc is None else acc + part
    o_ref[...] = x + 0.5 * _rms(acc, gpost_ref[...])


def _ffn_block(x, g_pre, g_post, wg, wu, wd):
    m = x.shape[0]
    tm = min(m, 512)
    const = lambda a: pl.BlockSpec(a.shape, lambda i: (0, 0), pipeline_mode=pl.Buffered(1))
    return pl.pallas_call(
        _ffn_kernel,
        grid=(m // tm,),
        in_specs=[pl.BlockSpec((tm, D_MODEL), lambda i: (i, 0)), const(g_pre), const(g_post),
                  const(wg), const(wu), const(wd)],
        out_specs=pl.BlockSpec((tm, D_MODEL), lambda i: (i, 0)),
        out_shape=jax.ShapeDtypeStruct((m, D_MODEL), F32),
        compiler_params=_params("parallel"),
        name="ffn_block",
    )(x, g_pre, g_post, wg, wu, wd)


def _inproj_kernel(x_ref, g_ref, csk_ref, cosq_ref, sinq_ref, wh_ref, wcq_ref, wckv_ref, wkr_ref,
                   qg_ref, kvg_ref, wqn_ref, wqr_ref, wqrr_ref, wuk_ref,
                   zh_ref, ckv_ref, kr_ref, ckvb_ref, krb_ref, qlat_ref, qrope_ref):
    u = _rms(x_ref[...], g_ref[...]).astype(BF16)
    zh_ref[...] = _dot(u, wh_ref[...])
    ckv = _rms(_dot(u, wckv_ref[...]), kvg_ref[...])
    ckv_ref[...] = ckv
    ckvb_ref[...] = ckv.astype(BF16)
    kz = _dot(u, wkr_ref[...])
    csk = csk_ref[...]
    kr = kz[:, :QK_ROPE] * csk[:, :QK_ROPE] + kz[:, QK_ROPE:] * csk[:, QK_ROPE:]
    kr_ref[...] = kr
    krb_ref[...] = kr.astype(BF16)
    cqn = _rms(_dot(u, wcq_ref[...]), qg_ref[...]).astype(BF16)
    qn = _dot(cqn, wqn_ref[...]).astype(BF16)
    for h in range(MLA_HEADS):
        qlat_ref[:, h * KV_LORA:(h + 1) * KV_LORA] = _dot(
            qn[:, h * QK_NOPE:(h + 1) * QK_NOPE], wuk_ref[h]).astype(BF16)
    qr = _dot(cqn, wqr_ref[...]) * cosq_ref[...] + _dot(cqn, wqrr_ref[...]) * sinq_ref[...]
    qrope_ref[...] = qr.astype(BF16)


def _in_proj(x, g, csk, cosq, sinq, w):
    m = x.shape[0]
    tm = min(m, 256)
    nrep = csk.shape[0] // tm
    full = lambda a: pl.BlockSpec(a.shape, lambda i: (0,) * a.ndim)
    row = lambda width: pl.BlockSpec((tm, width), lambda i: (i, 0))
    tab = lambda width: pl.BlockSpec((tm, width), lambda i: (i % nrep, 0))
    rope_w = MLA_HEADS * QK_ROPE
    return pl.pallas_call(
        _inproj_kernel,
        grid=(m // tm,),
        in_specs=[row(D_MODEL), full(g), tab(2 * QK_ROPE), tab(rope_w), tab(rope_w),
                  full(w["w_h"]), full(w["w_cq"]), full(w["w_ckv"]), full(w["w_kr"]),
                  full(w["q_norm_g"]), full(w["kv_norm_g"]), full(w["w_qn"]), full(w["w_qr"]),
                  full(w["w_qrr"]), full(w["w_uk"])],
        out_specs=[row(4 * HG_W), row(KV_LORA), row(QK_ROPE), row(KV_LORA), row(QK_ROPE),
                   row(MLA_HEADS * KV_LORA), row(rope_w)],
        out_shape=[
            jax.ShapeDtypeStruct((m, 4 * HG_W), F32),
            jax.ShapeDtypeStruct((m, KV_LORA), F32),
            jax.ShapeDtypeStruct((m, QK_ROPE), F32),
            jax.ShapeDtypeStruct((m, KV_LORA), BF16),
            jax.ShapeDtypeStruct((m, QK_ROPE), BF16),
            jax.ShapeDtypeStruct((m, MLA_HEADS * KV_LORA), BF16),
            jax.ShapeDtypeStruct((m, rope_w), BF16),
        ],
        compiler_params=_params("parallel"),
        name="in_proj",
    )(x, g, csk, cosq, sinq, w["w_h"], w["w_cq"], w["w_ckv"], w["w_kr"], w["q_norm_g"],
      w["kv_norm_g"], w["w_qn"], w["w_qr"], w["w_qrr"], w["w_uk"])


def _split3(g):
    g1 = g.astype(BF16).astype(F32)
    r1 = g - g1
    g2 = r1.astype(BF16).astype(F32)
    g3 = (r1 - g2).astype(BF16).astype(F32)
    return g1, g2, g3


def _hgrn_kernel(*refs, chunk, t_valid, has_s0):
    if has_s0:
        hq_ref, hf_ref, hi_ref, hg_ref, lbl_ref, ng_ref, s0_ref = refs[:7]
        refs = refs[7:]
    else:
        hq_ref, hf_ref, hi_ref, hg_ref, lbl_ref, ng_ref = refs[:6]
        s0_ref = None
        refs = refs[6:]
    o_ref, sfin_ref, st_ref, c_ref, v_ref = refs
    ti = pl.program_id(1)
    t_tile = hq_ref.shape[0]
    n_chunks = t_tile // chunk
    n_tiles = chunk // SUBLANES

    lbl = lbl_ref[...]
    e = jnp.exp(lbl - jnp.max(lbl, axis=0, keepdims=True))
    lb_all = e[0:1, :] / jnp.sum(e, axis=0, keepdims=True)

    @pl.when(ti == 0)
    def _():
        for h in range(HG_HEADS):
            st_ref[h] = s0_ref[h].T if has_s0 else jnp.zeros((HG_DV, HG_DK), F32)

    r_i = lax.broadcasted_iota(jnp.int32, (chunk, chunk), 0)
    c_i = lax.broadcasted_iota(jnp.int32, (chunk, chunk), 1)
    tril = (c_i <= r_i).astype(F32)
    row8 = lax.broadcasted_iota(jnp.int32, (SUBLANES, HG_DK), 0)
    ng = ng_ref[...]
    ones = jnp.ones((HG_DK, HG_DV), BF16)
    pairs = [(s, i) for s in range(chunk) for i in range(s // SUBLANES, n_tiles)]

    def head_chunk(h, start):
        sl = pl.ds(start, chunk)
        hl = slice(h * HG_DK, (h + 1) * HG_DK)
        lb = lb_all[:, hl]
        q = _silu(hq_ref[sl, hl])
        f = lb + (1.0 - lb) * _sigmoid(hf_ref[sl, hl])
        g = jnp.log(f)
        k = 1.0 - f
        v = hi_ref[sl, hl]
        if t_valid is not None:
            pos = lax.broadcasted_iota(jnp.int32, (chunk, HG_DK), 0) + (ti * t_tile + start)
            valid = pos < t_valid
            g = jnp.where(valid, g, 0.0)
            k = jnp.where(valid, k, 0.0)
            v = jnp.where(valid, v, 0.0)
        g1, g2, g3 = _split3(g)
        b = _dot(tril, g1) + _dot(tril, g2) + _dot(tril, g3)
        b2 = b * LOG2E
        c_ref[h] = b2 - jnp.log(k) * LOG2E
        v_ref[h] = v

        q_t = [q[i * SUBLANES:(i + 1) * SUBLANES, :] for i in range(n_tiles)]
        b_t = [b2[i * SUBLANES:(i + 1) * SUBLANES, :] for i in range(n_tiles)]
        p_tiles = []
        for s, i in pairs:
            d = b_t[i] - c_ref[h, s:s + 1, :]
            if i == s // SUBLANES and s % SUBLANES:
                d = jnp.where(row8 >= s % SUBLANES, d, NEG_BIG)
            p_tiles.append(q_t[i] * jnp.exp2(d))
        a_rep = _dot(jnp.concatenate(p_tiles, axis=0).astype(BF16), ones)
        o_t = [jnp.zeros((SUBLANES, HG_DV), F32) for _ in range(n_tiles)]
        for j, (s, i) in enumerate(pairs):
            o_t[i] = o_t[i] + a_rep[j * SUBLANES:(j + 1) * SUBLANES, :] * v_ref[h, s:s + 1, :]
        o = jnp.concatenate(o_t, axis=0) if n_tiles > 1 else o_t[0]

        st = st_ref[h]
        o = o + _dot_nt((q * jnp.exp(b)).astype(BF16), st.astype(BF16))
        bl = b[chunk - 1:chunk, :]
        kd = (k * jnp.exp(bl - b)).astype(BF16)
        st_ref[h] = st * jnp.exp(bl) + _dot_tn(v.astype(BF16), kd)

        o_ref[sl, hl] = _rms(o, ng) * _silu(hg_ref[sl, hl])

    def chunk_body(c, carry):
        start = pl.multiple_of(c * chunk, chunk)
        for h in range(HG_HEADS):
            head_chunk(h, start)
        return carry

    lax.fori_loop(0, n_chunks, chunk_body, 0)

    @pl.when(ti == pl.num_programs(1) - 1)
    def _():
        for h in range(HG_HEADS):
            sfin_ref[h] = st_ref[h].T


def _hgrn(zh, lb_logits, ng, s0, chunk, t_valid):
    bsz, t, _ = zh.shape
    has_s0 = s0 is not None
    t_tile = min(t, 512)
    assert t % t_tile == 0 and t_tile % chunk == 0
    col = lambda j: pl.BlockSpec((None, t_tile, HG_W), lambda b, i: (b, i, j))
    in_specs = [col(0), col(1), col(2), col(3),
                pl.BlockSpec(lb_logits.shape, lambda b, i: (0, 0)),
                pl.BlockSpec((1, HG_DV), lambda b, i: (0, 0))]
    args = [zh, zh, zh, zh, lb_logits, ng]
    st_spec = pl.BlockSpec((None, HG_HEADS, HG_DK, HG_DV), lambda b, i: (b, 0, 0, 0))
    if has_s0:
        in_specs.append(st_spec)
        args.append(s0)
    return pl.pallas_call(
        functools.partial(_hgrn_kernel, chunk=chunk, t_valid=t_valid, has_s0=has_s0),
        grid=(bsz, t // t_tile),
        in_specs=in_specs,
        out_specs=[pl.BlockSpec((None, t_tile, HG_W), lambda b, i: (b, i, 0)), st_spec],
        out_shape=[jax.ShapeDtypeStruct((bsz, t, HG_W), F32),
                   jax.ShapeDtypeStruct((bsz, HG_HEADS, HG_DK, HG_DV), F32)],
        scratch_shapes=[pltpu.VMEM((HG_HEADS, HG_DV, HG_DK), F32),
                        pltpu.VMEM((HG_HEADS, chunk, HG_DK), F32),
                        pltpu.VMEM((HG_HEADS, chunk, HG_DV), F32)],
        compiler_params=_params("parallel", "arbitrary"),
        name="hgrn",
    )(*args)


ATTN_SCALE = 1.0 / math.sqrt(QK_NOPE + QK_ROPE)


LANES = 128


def _attn_prompt_kernel(ql_ref, qr_ref, kc_ref, kr_ref, wuv_ref, o_ref, m_ref, l_ref, acc_ref, *, tq):
    qi = pl.program_id(1)
    ql = jnp.concatenate([ql_ref[:, h * KV_LORA:(h + 1) * KV_LORA] for h in range(MLA_HEADS)], axis=0)
    qr = jnp.concatenate([qr_ref[:, h * QK_ROPE:(h + 1) * QK_ROPE] for h in range(MLA_HEADS)], axis=0)
    m_ref[...] = jnp.full_like(m_ref, NEG_BIG)
    l_ref[...] = jnp.zeros_like(l_ref)
    acc_ref[...] = jnp.zeros_like(acc_ref)

    def step(kb, masked):
        sl = pl.ds(pl.multiple_of(kb * tq, tq), tq)
        kc = kc_ref[sl, :]
        s = (_dot_nt(ql, kc) + _dot_nt(qr, kr_ref[sl, :])) * ATTN_SCALE
        if masked:
            rows = lax.broadcasted_iota(jnp.int32, s.shape, 0) & (tq - 1)
            cols = lax.broadcasted_iota(jnp.int32, s.shape, 1)
            s = jnp.where(cols <= rows, s, NEG_BIG)
        m_prev = m_ref[...]
        m_new = jnp.maximum(m_prev, jnp.max(s, axis=-1, keepdims=True))
        alpha = jnp.exp(m_prev - m_new)
        p = jnp.exp(s - jnp.tile(m_new, (1, tq // LANES)))
        l_ref[...] = alpha * l_ref[...] + jnp.sum(p, axis=-1, keepdims=True)
        acc_ref[...] = jnp.tile(alpha, (1, KV_LORA // LANES)) * acc_ref[...] + _dot(p.astype(BF16), kc)
        m_ref[...] = m_new

    def body(kb, carry):
        step(kb, False)
        return carry

    lax.fori_loop(0, qi, body, 0)
    step(qi, True)
    o_lat = (acc_ref[...] / jnp.tile(l_ref[...], (1, KV_LORA // LANES))).astype(BF16)
    for h in range(MLA_HEADS):
        o_ref[:, h * V_DIM:(h + 1) * V_DIM] = _dot(o_lat[h * tq:(h + 1) * tq, :], wuv_ref[h])


def _attn_prompt(qlat, qrope, ckv_b, kr_b, w_uv, bsz, t):
    tq = 256
    assert tq & (tq - 1) == 0 and t % tq == 0
    nq = t // tq
    rows = MLA_HEADS * tq
    return pl.pallas_call(
        functools.partial(_attn_prompt_kernel, tq=tq),
        grid=(bsz, nq),
        in_specs=[
            pl.BlockSpec((tq, MLA_HEADS * KV_LORA), lambda b, i: (b * nq + i, 0)),
            pl.BlockSpec((tq, MLA_HEADS * QK_ROPE), lambda b, i: (b * nq + i, 0)),
            pl.BlockSpec((t, KV_LORA), lambda b, i: (b, 0)),
            pl.BlockSpec((t, QK_ROPE), lambda b, i: (b, 0)),
            pl.BlockSpec(w_uv.shape, lambda b, i: (0, 0, 0)),
        ],
        out_specs=pl.BlockSpec((tq, MLA_HEADS * V_DIM), lambda b, i: (b * nq + i, 0)),
        out_shape=jax.ShapeDtypeStruct((bsz * t, MLA_HEADS * V_DIM), F32),
        scratch_shapes=[pltpu.VMEM((rows, LANES), F32), pltpu.VMEM((rows, LANES), F32),
                        pltpu.VMEM((rows, KV_LORA), F32)],
        compiler_params=_params("parallel", "parallel"),
        name="attn_prompt",
    )(qlat, qrope, ckv_b, kr_b, w_uv)


PAGES_PER_STEP = 16
NEW_KEY_ROWS = 128


def _attn_sample_kernel(pt_ref, wq_ref, wr_ref, nc_ref, nr_ref, wuv_ref, *rest, t_new, n_q):
    del pt_ref
    pages_c = rest[:PAGES_PER_STEP]
    pages_r = rest[PAGES_PER_STEP:2 * PAGES_PER_STEP]
    o_ref, m_ref, l_ref, acc_ref, kv_ref, kq_ref, kr_ref = rest[2 * PAGES_PER_STEP:]
    j = pl.program_id(1)
    groups = LANES // n_q
    page = pages_c[0].shape[0]
    ppg = PAGES_PER_STEP // groups
    lat_rep = KV_LORA // LANES

    @pl.when(j == 0)
    def _():
        m_ref[...] = jnp.full_like(m_ref, NEG_BIG)
        l_ref[...] = jnp.zeros_like(l_ref)
        acc_ref[...] = jnp.zeros_like(acc_ref)

    for i in range(PAGES_PER_STEP):
        g, r = divmod(i, ppg)
        x = pages_c[i][...].astype(BF16)
        kv_ref[i * page:(i + 1) * page, :] = x
        kq_ref[r * page:(r + 1) * page, g * KV_LORA:(g + 1) * KV_LORA] = x
        kr_ref[g * QK_ROPE:(g + 1) * QK_ROPE, r * page:(r + 1) * page] = pages_r[i][...].astype(BF16)

    def lanes_to_rows(row):
        return jnp.broadcast_to(row, (LANES, LANES)).T

    def update(s, place, values):
        m_prev = m_ref[...]
        m_new = jnp.maximum(m_prev, jnp.max(s, axis=0, keepdims=True))
        alpha = jnp.exp(m_prev - m_new)
        p = jnp.exp(s - m_new)
        l_ref[...] = alpha * l_ref[...] + jnp.sum(p, axis=0, keepdims=True)
        m_ref[...] = m_new
        pv = _dot(place(p.T), values)
        acc_ref[...] = jnp.tile(lanes_to_rows(alpha), (1, lat_rep)) * acc_ref[...] + pv

    q_shift = n_q.bit_length() - 1

    def block_diag(p_t):
        row_g = lax.broadcasted_iota(jnp.int32, p_t.shape, 0) >> q_shift
        return jnp.concatenate([jnp.where(row_g == g, p_t, 0.0).astype(BF16) for g in range(groups)], axis=1)

    s = (_dot(kq_ref[...], wq_ref[...]) + _dot_tn(kr_ref[...], wr_ref[...])) * ATTN_SCALE
    update(s, block_diag, kv_ref[...])

    @pl.when(j == pl.num_programs(1) - 1)
    def _():
        nc = nc_ref[...]
        s2 = (_dot(nc, wq_ref[:KV_LORA, :]) + _dot(nr_ref[...], wr_ref[:QK_ROPE, :])) * ATTN_SCALE
        t_k = lax.broadcasted_iota(jnp.int32, s2.shape, 0)
        lane = lax.broadcasted_iota(jnp.int32, s2.shape, 1)
        ok = (lane < n_q) & (t_k <= (lane & (t_new - 1)))
        update(jnp.where(ok, s2, NEG_BIG), lambda p_t: p_t.astype(BF16), nc)

        grp = lambda a, g: a[g * n_q:(g + 1) * n_q]
        total = lambda a: functools.reduce(lambda x, y: x + y, [grp(a, g) for g in range(groups)])
        m_t = lanes_to_rows(m_ref[...])
        l_t = lanes_to_rows(l_ref[...])
        m_q = functools.reduce(jnp.maximum, [grp(m_t, g) for g in range(groups)])
        w = jnp.exp(m_t - jnp.tile(m_q, (groups, 1)))
        l_q = total(w * l_t)
        o_lat = total(jnp.tile(w, (1, lat_rep)) * acc_ref[...]) / jnp.tile(l_q, (1, lat_rep))
        o_lat = o_lat.astype(BF16)
        head = lax.broadcasted_iota(jnp.int32, (n_q, V_DIM), 0) >> (t_new.bit_length() - 1)
        out = jnp.zeros((n_q, V_DIM), F32)
        for h in range(MLA_HEADS):
            out = out + jnp.where(head == h, _dot(o_lat, wuv_ref[h]), 0.0)
        o_ref[...] = out


def _attn_sample(ql_t, qr_t, new_c, new_r, w_uv, cache_c, cache_r_t, page_table, t_new):
    bsz, _, n_q = ql_t.shape
    n_pages = page_table.shape[1]
    groups = LANES // n_q
    assert LANES % n_q == 0 and PAGES_PER_STEP % groups == 0 and n_pages % PAGES_PER_STEP == 0
    assert t_new & (t_new - 1) == 0 and n_q & (n_q - 1) == 0 and t_new <= NEW_KEY_ROWS and n_q >= SUBLANES
    page = cache_c.shape[1]
    keys = PAGES_PER_STEP * page

    eye = jnp.eye(groups, dtype=ql_t.dtype)
    place = lambda a: (eye[None, :, None, :, None] * a[:, None, :, None, :]).reshape(
        bsz, groups * a.shape[1], groups * n_q)
    wq, wr = place(ql_t), place(qr_t)

    per_b = lambda a: pl.BlockSpec((None,) + a.shape[1:], lambda b, j, pt: (b, 0, 0))

    def page_spec(i, a):
        return pl.BlockSpec((None,) + a.shape[1:], lambda b, j, pt: (pt[b, j * PAGES_PER_STEP + i], 0, 0))

    grid_spec = pltpu.PrefetchScalarGridSpec(
        num_scalar_prefetch=1,
        grid=(bsz, n_pages // PAGES_PER_STEP),
        in_specs=[per_b(wq), per_b(wr), per_b(new_c), per_b(new_r),
                  pl.BlockSpec(w_uv.shape, lambda b, j, pt: (0, 0, 0))]
        + [page_spec(i, cache_c) for i in range(PAGES_PER_STEP)]
        + [page_spec(i, cache_r_t) for i in range(PAGES_PER_STEP)],
        out_specs=pl.BlockSpec((None, n_q, V_DIM), lambda b, j, pt: (b, 0, 0)),
        scratch_shapes=[pltpu.VMEM((1, LANES), F32), pltpu.VMEM((1, LANES), F32),
                        pltpu.VMEM((LANES, KV_LORA), F32),
                        pltpu.VMEM((keys, KV_LORA), BF16),
                        pltpu.VMEM((keys // groups, groups * KV_LORA), BF16),
                        pltpu.VMEM((groups * QK_ROPE, keys // groups), BF16)],
    )
    return pl.pallas_call(
        functools.partial(_attn_sample_kernel, t_new=t_new, n_q=n_q),
        grid_spec=grid_spec,
        out_shape=jax.ShapeDtypeStruct((bsz, n_q, V_DIM), F32),
        compiler_params=_params("parallel", "arbitrary"),
        name="attn_sample",
    )(page_table, wq, wr, new_c, new_r, w_uv, *([cache_c] * PAGES_PER_STEP),
      *([cache_r_t] * PAGES_PER_STEP))


def _outproj_kernel(h_ref, oh_ref, om_ref, w1_ref, w2_ref, g_ref, o_ref):
    m = _dot(oh_ref[...].astype(BF16), w1_ref[...]) + _dot(om_ref[...].astype(BF16), w2_ref[...])
    o_ref[...] = h_ref[...] + _rms(m, g_ref[...])


def _out_proj(h, o_h, o_m, w1, w2, g):
    m = h.shape[0]
    tm = min(m, 512)
    full = lambda a: pl.BlockSpec(a.shape, lambda i: (0,) * a.ndim)
    row = lambda width: pl.BlockSpec((tm, width), lambda i: (i, 0))
    return pl.pallas_call(
        _outproj_kernel,
        grid=(m // tm,),
        in_specs=[row(D_MODEL), row(HG_W), row(MLA_HEADS * V_DIM), full(w1), full(w2), full(g)],
        out_specs=row(D_MODEL),
        out_shape=jax.ShapeDtypeStruct((m, D_MODEL), F32),
        compiler_params=_params("parallel"),
        name="out_proj",
    )(h, o_h, o_m, w1, w2, g)


def _rot_cols(w):
    half = w.shape[-1] // 2
    return jnp.concatenate([-w[..., half:], w[..., :half]], axis=-1)


def _prep_weights(norm_g, w_ffn_gate, w_ffn_up, w_ffn_down, w_in, q_norm_g, w_q_up, kv_norm_g,
                  w_kv_up, hg_norm_g, w_out):
    bf = lambda a: a.astype(BF16)
    w_h, w_cq, w_ckv, w_kr = jnp.split(w_in, [4 * HG_W, 4 * HG_W + Q_LORA, 4 * HG_W + Q_LORA + KV_LORA], axis=-1)
    w_q_rope = w_q_up[..., QK_NOPE:]
    return {
        "norm_g": norm_g.reshape(-1, 1, D_MODEL),
        "wg": bf(w_ffn_gate), "wu": bf(w_ffn_up), "wd": bf(w_ffn_down),
        "w_h": bf(w_h), "w_cq": bf(w_cq), "w_ckv": bf(w_ckv),
        "w_kr": bf(jnp.concatenate([w_kr, _rot_cols(w_kr)], axis=-1)),
        "q_norm_g": q_norm_g.reshape(1, Q_LORA), "kv_norm_g": kv_norm_g.reshape(1, KV_LORA),
        "w_qn": bf(w_q_up[..., :QK_NOPE].reshape(Q_LORA, MLA_HEADS * QK_NOPE)),
        "w_qr": bf(w_q_rope.reshape(Q_LORA, MLA_HEADS * QK_ROPE)),
        "w_qrr": bf(_rot_cols(w_q_rope).reshape(Q_LORA, MLA_HEADS * QK_ROPE)),
        "w_uk": bf(jnp.transpose(w_kv_up[..., :QK_NOPE], (1, 2, 0))),
        "w_uv": bf(jnp.transpose(w_kv_up[..., QK_NOPE:], (1, 0, 2))),
        "hg_norm_g": hg_norm_g.reshape(1, HG_DV),
        "w_out1": bf(w_out[:HG_W]), "w_out2": bf(w_out[HG_W:]),
    }


def _rope_tables(pos):
    half = QK_ROPE // 2
    inv = ROPE_THETA ** (-jnp.arange(half, dtype=F32) / half)
    ang = pos.astype(F32)[:, None] * inv[None, :]
    cos = jnp.tile(jnp.cos(ang), (1, 2))
    sin = jnp.tile(jnp.sin(ang), (1, 2))
    return (jnp.concatenate([cos, sin], axis=-1), jnp.tile(cos, (1, MLA_HEADS)), jnp.tile(sin, (1, MLA_HEADS)))


def kernel(x_prompt, x_sample, cache_kv_latent, cache_k_rope, state_hgrn, page_table, hgrn_lb_logits,
           norm_g, w_ffn_gate, w_ffn_up, w_ffn_down, w_in, q_norm_g, w_q_up, kv_norm_g, w_kv_up,
           hg_norm_g, w_out):
    bp, tp, _ = x_prompt.shape
    bs, ts, _ = x_sample.shape
    depth = norm_g.shape[0]
    assert depth == 1
    past_len = page_table.shape[1] * cache_kv_latent.shape[2]
    w = _prep_weights(norm_g[0], w_ffn_gate[0], w_ffn_up[0], w_ffn_down[0], w_in[0], q_norm_g[0],
                      w_q_up[0], kv_norm_g[0], w_kv_up[0], hg_norm_g[0], w_out[0])
    ng = w["norm_g"]
    lb_logits = hgrn_lb_logits.astype(F32)

    def pre(x, tables):
        h1 = _ffn_block(x, ng[0], ng[1], w["wg"][0], w["wu"][0], w["wd"][0])
        return (h1,) + tuple(_in_proj(h1, ng[2], *tables, w))

    def post(h1, o_h, o_m):
        h2 = _out_proj(h1, o_h, o_m, w["w_out1"], w["w_out2"], ng[3])
        return _ffn_block(h2, ng[4], ng[5], w["wg"][1], w["wu"][1], w["wd"][1])

    h1, zh, ckv, kr, ckv_b, kr_b, qlat, qrope = pre(
        x_prompt.reshape(bp * tp, D_MODEL), _rope_tables(jnp.arange(tp, dtype=jnp.int32)))
    o_h, st_p = _hgrn(zh.reshape(bp, tp, 4 * HG_W), lb_logits, w["hg_norm_g"], None, chunk=64, t_valid=None)
    o_m = _attn_prompt(qlat, qrope, ckv_b, kr_b, w["w_uv"], bp, tp)
    y_p = post(h1, o_h.reshape(bp * tp, HG_W), o_m).reshape(bp, tp, D_MODEL)
    ckv_p = ckv.reshape(1, bp, tp, KV_LORA)
    kr_p = kr.reshape(1, bp, tp, QK_ROPE)

    pos_s = past_len + jnp.arange(ts, dtype=jnp.int32)
    tabs = tuple(jnp.tile(a, (bs, 1)) for a in _rope_tables(pos_s))
    h1s, zhs, ckvs, krs, ckvs_b, krs_b, qlats, qropes = pre(x_sample.reshape(bs * ts, D_MODEL), tabs)
    t_pad = -(-ts // SUBLANES) * SUBLANES
    zhs_pad = jnp.pad(zhs.reshape(bs, ts, 4 * HG_W), ((0, 0), (0, t_pad - ts), (0, 0)))
    o_hs, st_s = _hgrn(zhs_pad, lb_logits, w["hg_norm_g"], state_hgrn[0], chunk=SUBLANES,
                       t_valid=ts if t_pad != ts else None)
    q_cols = lambda a, wd: a.reshape(bs, ts, MLA_HEADS, wd).transpose(0, 3, 2, 1).reshape(bs, wd, MLA_HEADS * ts)
    pad_new = lambda a, wd: jnp.pad(a.reshape(bs, ts, wd), ((0, 0), (0, NEW_KEY_ROWS - ts), (0, 0)))
    o_ms = _attn_sample(q_cols(qlats, KV_LORA), q_cols(qropes, QK_ROPE),
                        pad_new(ckvs_b, KV_LORA), pad_new(krs_b, QK_ROPE), w["w_uv"],
                        cache_kv_latent[0], jnp.swapaxes(cache_k_rope[0], 1, 2), page_table, ts)
    o_ms = o_ms.reshape(bs, MLA_HEADS, ts, V_DIM).transpose(0, 2, 1, 3).reshape(bs * ts, MLA_HEADS * V_DIM)
    y_s = post(h1s, o_hs[:, :ts].reshape(bs * ts, HG_W), o_ms).reshape(bs, ts, D_MODEL)

    return (y_p, y_s, ckv_p, kr_p, st_p[None], ckvs.reshape(1, bs, ts, KV_LORA),
            krs.reshape(1, bs, ts, QK_ROPE), st_s[None])
```

```python
import functools
import math

import jax
import jax.numpy as jnp
from jax import lax
from jax.experimental import pallas as pl
from jax.experimental.pallas import tpu as pltpu

D_MODEL = 1024
D_FF = 2816
HG_HEADS = 4
HG_DK = 128
HG_DV = 128
MLA_HEADS = 4
QK_NOPE = 128
QK_ROPE = 64
V_DIM = 128
Q_LORA = 768
KV_LORA = 256
ROPE_THETA = 10000.0
EPS = 1e-6
HG_W = HG_HEADS * HG_DK

VMEM_LIMIT_BYTES = 56 * 1024 * 1024
SUBLANES = 8
NEG_BIG = -1e30
LOG2E = math.log2(math.e)

F32 = jnp.float32
BF16 = jnp.bfloat16


def _dot(a, b):
    return jnp.dot(a, b, preferred_element_type=F32)


def _dot_nt(a, b):
    return lax.dot_general(a, b, (((1,), (1,)), ((), ())), preferred_element_type=F32)


def _dot_tn(a, b):
    return lax.dot_general(a, b, (((0,), (0,)), ((), ())), preferred_element_type=F32)


def _rms(x, g):
    return x * lax.rsqrt(jnp.mean(x * x, axis=-1, keepdims=True) + EPS) * g


def _sigmoid(x):
    return 1.0 / (1.0 + jnp.exp(-x))


def _silu(x):
    return x * _sigmoid(x)


def _params(*sem):
    return pltpu.CompilerParams(dimension_semantics=sem, vmem_limit_bytes=VMEM_LIMIT_BYTES)


FFN_CHUNK = 256


def _ffn_kernel(*refs, mix_in):
    if mix_in:
        h_ref, oh_ref, om_ref, w1_ref, w2_ref, gmix_ref = refs[:6]
        refs = refs[6:]
        mixed = _dot(oh_ref[...], w1_ref[...]) + _dot(om_ref[...], w2_ref[...])
        x = h_ref[...] + _rms(mixed, gmix_ref[...])
    else:
        x = refs[0][...]
        refs = refs[1:]
    gpre_ref, gpost_ref, wg_ref, wu_ref, wd_ref, o_ref = refs
    xn = _rms(x, gpre_ref[...]).astype(BF16)
    acc = None
    for f in range(D_FF // FFN_CHUNK):
        cols = slice(f * FFN_CHUNK, (f + 1) * FFN_CHUNK)
        a = _dot(xn, wg_ref[:, cols])
        b = _dot(xn, wu_ref[:, cols])
        part = _dot((_silu(a) * b).astype(BF16), wd_ref[cols, :])
        acc = part if acc is None else acc + part
    o_ref[...] = x + 0.5 * _rms(acc, gpost_ref[...])


def _ffn_block(x, g_pre, g_post, wg, wu, wd, mix=None):
    m = x.shape[0]
    tm = min(m, 512)
    const = lambda a: pl.BlockSpec(a.shape, lambda i: (0, 0), pipeline_mode=pl.Buffered(1))
    row = lambda a: pl.BlockSpec((tm, a.shape[1]), lambda i: (i, 0))
    args, in_specs = [x], [row(x)]
    if mix is not None:
        o_h, o_m, w1, w2, g_mix = mix
        args += [o_h, o_m, w1, w2, g_mix]
        in_specs += [row(o_h), row(o_m), const(w1), const(w2), const(g_mix)]
    args += [g_pre, g_post, wg, wu, wd]
    in_specs += [const(g_pre), const(g_post), const(wg), const(wu), const(wd)]
    return pl.pallas_call(
        functools.partial(_ffn_kernel, mix_in=mix is not None),
        grid=(m // tm,),
        in_specs=in_specs,
        out_specs=row(x),
        out_shape=jax.ShapeDtypeStruct((m, D_MODEL), F32),
        compiler_params=_params("parallel"),
        name="ffn_mix_block" if mix is not None else "ffn_block",
    )(*args)


def _inproj_kernel(x_ref, g_ref, csk_ref, cosq_ref, sinq_ref, wh_ref, wcq_ref, wckv_ref, wkr_ref,
                   qg_ref, kvg_ref, wqn_ref, wqr_ref, wqrr_ref, wuk_ref,
                   zh_ref, ckv_ref, kr_ref, ckvb_ref, krb_ref, qlat_ref, qrope_ref):
    u = _rms(x_ref[...], g_ref[...]).astype(BF16)
    zh_ref[...] = _dot(u, wh_ref[...])
    ckv = _rms(_dot(u, wckv_ref[...]), kvg_ref[...])
    ckv_ref[...] = ckv
    ckvb_ref[...] = ckv.astype(BF16)
    kz = _dot(u, wkr_ref[...])
    csk = csk_ref[...]
    kr = kz[:, :QK_ROPE] * csk[:, :QK_ROPE] + kz[:, QK_ROPE:] * csk[:, QK_ROPE:]
    kr_ref[...] = kr
    krb_ref[...] = kr.astype(BF16)
    cqn = _rms(_dot(u, wcq_ref[...]), qg_ref[...]).astype(BF16)
    qn = _dot(cqn, wqn_ref[...]).astype(BF16)
    for h in range(MLA_HEADS):
        qlat_ref[:, h * KV_LORA:(h + 1) * KV_LORA] = _dot(
            qn[:, h * QK_NOPE:(h + 1) * QK_NOPE], wuk_ref[h]).astype(BF16)
    qr = _dot(cqn, wqr_ref[...]) * cosq_ref[...] + _dot(cqn, wqrr_ref[...]) * sinq_ref[...]
    qrope_ref[...] = qr.astype(BF16)


def _in_proj(x, g, csk, cosq, sinq, w):
    m = x.shape[0]
    tm = min(m, 512)
    nrep = csk.shape[0] // tm
    full = lambda a: pl.BlockSpec(a.shape, lambda i: (0,) * a.ndim, pipeline_mode=pl.Buffered(1))
    row = lambda width: pl.BlockSpec((tm, width), lambda i: (i, 0))
    tab = lambda width: pl.BlockSpec((tm, width), lambda i: (i % nrep, 0))
    rope_w = MLA_HEADS * QK_ROPE
    return pl.pallas_call(
        _inproj_kernel,
        grid=(m // tm,),
        in_specs=[row(D_MODEL), full(g), tab(2 * QK_ROPE), tab(rope_w), tab(rope_w),
                  full(w["w_h"]), full(w["w_cq"]), full(w["w_ckv"]), full(w["w_kr"]),
                  full(w["q_norm_g"]), full(w["kv_norm_g"]), full(w["w_qn"]), full(w["w_qr"]),
                  full(w["w_qrr"]), full(w["w_uk"])],
        out_specs=[row(4 * HG_W), row(KV_LORA), row(QK_ROPE), row(KV_LORA), row(QK_ROPE),
                   row(MLA_HEADS * KV_LORA), row(rope_w)],
        out_shape=[
            jax.ShapeDtypeStruct((m, 4 * HG_W), F32),
            jax.ShapeDtypeStruct((m, KV_LORA), F32),
            jax.ShapeDtypeStruct((m, QK_ROPE), F32),
            jax.ShapeDtypeStruct((m, KV_LORA), BF16),
            jax.ShapeDtypeStruct((m, QK_ROPE), BF16),
            jax.ShapeDtypeStruct((m, MLA_HEADS * KV_LORA), BF16),
            jax.ShapeDtypeStruct((m, rope_w), BF16),
        ],
        compiler_params=_params("parallel"),
        name="in_proj",
    )(x, g, csk, cosq, sinq, w["w_h"], w["w_cq"], w["w_ckv"], w["w_kr"], w["q_norm_g"],
      w["kv_norm_g"], w["w_qn"], w["w_qr"], w["w_qrr"], w["w_uk"])


def _split3(g):
    g1 = g.astype(BF16).astype(F32)
    r1 = g - g1
    g2 = r1.astype(BF16).astype(F32)
    g3 = (r1 - g2).astype(BF16).astype(F32)
    return g1, g2, g3


HG_SUB = 16


def _hgrn_kernel(*refs, chunk, t_valid, has_s0):
    if has_s0:
        hq_ref, hf_ref, hi_ref, hg_ref, lbl_ref, ng_ref, s0_ref = refs[:7]
        refs = refs[7:]
    else:
        hq_ref, hf_ref, hi_ref, hg_ref, lbl_ref, ng_ref = refs[:6]
        s0_ref = None
        refs = refs[6:]
    o_ref, sfin_ref, st_ref, c_ref, v_ref = refs
    ti = pl.program_id(1)
    t_tile = hq_ref.shape[0]
    n_chunks = t_tile // chunk
    n_tiles = chunk // SUBLANES

    lbl = lbl_ref[...]
    e = jnp.exp(lbl - jnp.max(lbl, axis=0, keepdims=True))
    lb_all = e[0:1, :] / jnp.sum(e, axis=0, keepdims=True)

    @pl.when(ti == 0)
    def _():
        for h in range(HG_HEADS):
            st_ref[h] = s0_ref[h].T if has_s0 else jnp.zeros((HG_DV, HG_DK), F32)

    r_i = lax.broadcasted_iota(jnp.int32, (chunk, chunk), 0)
    c_i = lax.broadcasted_iota(jnp.int32, (chunk, chunk), 1)
    tril = (c_i <= r_i).astype(F32)
    row8 = lax.broadcasted_iota(jnp.int32, (SUBLANES, HG_DK), 0)
    ng = ng_ref[...]
    ones = jnp.ones((HG_DK, HG_DV), BF16)
    sub = min(chunk, HG_SUB)
    n_sub = chunk // sub
    pairs = [(s, i) for s in range(chunk) for i in range(s // SUBLANES, (s // sub + 1) * sub // SUBLANES)]
    if n_sub > 1:
        assert sub & (sub - 1) == 0 and chunk & (chunk - 1) == 0
        rows_sub = lax.broadcasted_iota(jnp.int32, (chunk, (n_sub - 1) * chunk), 0) >> (sub.bit_length() - 1)
        cols_blk = lax.broadcasted_iota(jnp.int32, (chunk, (n_sub - 1) * chunk), 1) >> (chunk.bit_length() - 1)
        sub_mask = cols_blk == rows_sub - 1

    def gates(h, start):
        sl = pl.ds(start, chunk)
        hl = slice(h * HG_DK, (h + 1) * HG_DK)
        lb = lb_all[:, hl]
        q = _silu(hq_ref[sl, hl])
        f = lb + (1.0 - lb) * _sigmoid(hf_ref[sl, hl])
        g = jnp.log(f)
        k = 1.0 - f
        v = hi_ref[sl, hl]
        if t_valid is not None:
            pos = lax.broadcasted_iota(jnp.int32, (chunk, HG_DK), 0) + (ti * t_tile + start)
            valid = pos < t_valid
            g = jnp.where(valid, g, 0.0)
            k = jnp.where(valid, k, 0.0)
            v = jnp.where(valid, v, 0.0)
        g1, g2, g3 = _split3(g)
        b = _dot(tril, g1) + _dot(tril, g2) + _dot(tril, g3)
        return q, k, v, b * LOG2E

    def products(h, q, k, v, b2):
        c_ref[h] = b2 - jnp.log(k) * LOG2E
        v_ref[h] = v
        q_t = [q[i * SUBLANES:(i + 1) * SUBLANES, :] for i in range(n_tiles)]
        b_t = [b2[i * SUBLANES:(i + 1) * SUBLANES, :] for i in range(n_tiles)]
        p_tiles = []
        for s, i in pairs:
            d = b_t[i] - c_ref[h, s:s + 1, :]
            if i == s // SUBLANES and s % SUBLANES:
                d = jnp.where(row8 >= s % SUBLANES, d, NEG_BIG)
            p_tiles.append(q_t[i] * jnp.exp2(d))
        a_rep = _dot(jnp.concatenate(p_tiles, axis=0).astype(BF16), ones)

        a_sub = None
        if n_sub > 1:
            zeros_sub = jnp.zeros((sub, HG_DK), F32)
            q_parts, k_parts = [zeros_sub], []
            for j in range(1, n_sub):
                lo = j * sub
                r_j = b2[lo - 1:lo, :]
                q_parts.append(q[lo:lo + sub, :] * jnp.exp2(b2[lo:lo + sub, :] - r_j))
                k_parts += [k[:lo, :] * jnp.exp2(r_j - b2[:lo, :])] + [zeros_sub] * (n_sub - j)
            a_sub = _dot_nt(jnp.concatenate(q_parts, axis=0).astype(BF16),
                            jnp.concatenate(k_parts, axis=0).astype(BF16))

        st = st_ref[h]
        o_state = _dot_nt((q * jnp.exp2(b2)).astype(BF16), st.astype(BF16))
        bl = b2[chunk - 1:chunk, :]
        v_b = v.astype(BF16)
        st_ref[h] = st * jnp.exp2(bl) + _dot_tn(v_b, (k * jnp.exp2(bl - b2)).astype(BF16))
        return a_rep, a_sub, o_state, v_b

    def outputs(h, start, a_rep, a_sub, o_state, v_b):
        sl = pl.ds(start, chunk)
        hl = slice(h * HG_DK, (h + 1) * HG_DK)
        o_t = [o_state[i * SUBLANES:(i + 1) * SUBLANES, :] for i in range(n_tiles)]
        for j, (s, i) in enumerate(pairs):
            o_t[i] = o_t[i] + a_rep[j * SUBLANES:(j + 1) * SUBLANES, :] * v_ref[h, s:s + 1, :]
        o = jnp.concatenate(o_t, axis=0) if n_tiles > 1 else o_t[0]
        if a_sub is not None:
            o = o + _dot(jnp.where(sub_mask, a_sub, 0.0).astype(BF16),
                         jnp.concatenate([v_b] * (n_sub - 1), axis=0))
        o_ref[sl, hl] = (_rms(o, ng) * _silu(hg_ref[sl, hl])).astype(o_ref.dtype)

    def chunk_body(c, carry):
        start = pl.multiple_of(c * chunk, chunk)
        heads = range(HG_HEADS)
        gated = [gates(h, start) for h in heads]
        prods = [products(h, *gated[h]) for h in heads]
        for h in heads:
            outputs(h, start, *prods[h])
        return carry

    lax.fori_loop(0, n_chunks, chunk_body, 0)

    @pl.when(ti == pl.num_programs(1) - 1)
    def _():
        for h in range(HG_HEADS):
            sfin_ref[h] = st_ref[h].T


def _hgrn(zh, lb_logits, ng, s0, chunk, t_valid):
    bsz, t, _ = zh.shape
    has_s0 = s0 is not None
    t_tile = min(t, 512)
    assert t % t_tile == 0 and t_tile % chunk == 0
    col = lambda j: pl.BlockSpec((None, t_tile, HG_W), lambda b, i: (b, i, j))
    in_specs = [col(0), col(1), col(2), col(3),
                pl.BlockSpec(lb_logits.shape, lambda b, i: (0, 0)),
                pl.BlockSpec((1, HG_DV), lambda b, i: (0, 0))]
    args = [zh, zh, zh, zh, lb_logits, ng]
    st_spec = pl.BlockSpec((None, HG_HEADS, HG_DK, HG_DV), lambda b, i: (b, 0, 0, 0))
    if has_s0:
        in_specs.append(st_spec)
        args.append(s0)
    return pl.pallas_call(
        functools.partial(_hgrn_kernel, chunk=chunk, t_valid=t_valid, has_s0=has_s0),
        grid=(bsz, t // t_tile),
        in_specs=in_specs,
        out_specs=[pl.BlockSpec((None, t_tile, HG_W), lambda b, i: (b, i, 0)), st_spec],
        out_shape=[jax.ShapeDtypeStruct((bsz, t, HG_W), BF16),
                   jax.ShapeDtypeStruct((bsz, HG_HEADS, HG_DK, HG_DV), F32)],
        scratch_shapes=[pltpu.VMEM((HG_HEADS, HG_DV, HG_DK), F32),
                        pltpu.VMEM((HG_HEADS, chunk, HG_DK), F32),
                        pltpu.VMEM((HG_HEADS, chunk, HG_DV), F32)],
        compiler_params=_params("parallel", "arbitrary"),
        name="hgrn",
    )(*args)


ATTN_SCALE = 1.0 / math.sqrt(QK_NOPE + QK_ROPE)
ATTN_EXP2_SCALE = ATTN_SCALE * LOG2E

LANES = 128


def _attn_prompt_kernel(ql_ref, qr_ref, kc_ref, kr_ref, wuv_ref, o_ref,
                        m_ref, l_ref, a_ref, acc_ref, s_ref, p_ref, *, tq):
    qi = pl.program_id(1)
    ql = jnp.concatenate([ql_ref[:, h * KV_LORA:(h + 1) * KV_LORA] for h in range(MLA_HEADS)], axis=0)
    qr = jnp.concatenate([qr_ref[:, h * QK_ROPE:(h + 1) * QK_ROPE] for h in range(MLA_HEADS)], axis=0)
    m_ref[...] = jnp.full_like(m_ref, NEG_BIG)
    l_ref[...] = jnp.zeros_like(l_ref)
    a_ref[...] = jnp.zeros_like(a_ref)
    acc_ref[...] = jnp.zeros_like(acc_ref)
    p_ref[1] = jnp.zeros(p_ref.shape[1:], BF16)

    def keys(kb):
        return pl.ds(pl.multiple_of(kb * tq, tq), tq)

    def scores(kb):
        return (_dot_nt(ql, kc_ref[keys(kb), :]) + _dot_nt(qr, kr_ref[keys(kb), :])) * ATTN_EXP2_SCALE

    def add_values(kb, slot):
        acc_ref[...] = (jnp.tile(a_ref[...], (1, KV_LORA // LANES)) * acc_ref[...]
                        + _dot(p_ref[slot], kc_ref[keys(kb), :]))

    def softmax(slot, masked):
        s = s_ref[slot]
        if masked:
            rows = lax.broadcasted_iota(jnp.int32, s.shape, 0) & (tq - 1)
            cols = lax.broadcasted_iota(jnp.int32, s.shape, 1)
            s = jnp.where(cols <= rows, s, NEG_BIG)
        m_prev = m_ref[...]
        m_new = jnp.maximum(m_prev, jnp.max(s, axis=-1, keepdims=True))
        alpha = jnp.exp2(m_prev - m_new)
        p = jnp.exp2(s - jnp.tile(m_new, (1, tq // LANES)))
        l_ref[...] = alpha * l_ref[...] + jnp.sum(p, axis=-1, keepdims=True)
        m_ref[...] = m_new
        a_ref[...] = alpha
        p_ref[slot] = p.astype(BF16)

    s_ref[0] = scores(0)

    def iteration(i, cur):
        s_ref[1 - cur] = scores(i + 1)
        add_values(jnp.maximum(i - 1, 0), 1 - cur)
        softmax(cur, False)

    def last(cur):
        add_values(jnp.maximum(qi - 1, 0), 1 - cur)
        softmax(cur, True)
        add_values(qi, cur)

    def pair(j, carry):
        iteration(2 * j, 0)
        iteration(2 * j + 1, 1)
        return carry

    lax.fori_loop(0, qi >> 1, pair, 0)

    @pl.when((qi & 1) == 1)
    def _():
        iteration(qi - 1, 0)
        last(1)

    @pl.when((qi & 1) == 0)
    def _():
        last(0)

    o_lat =(acc_ref[...] / jnp.tile(l_ref[...], (1, KV_LORA // LANES))).astype(BF16)
    for h in range(MLA_HEADS):
        o_ref[:, h * V_DIM:(h + 1) * V_DIM] = _dot(o_lat[h * tq:(h + 1) * tq, :], wuv_ref[h]).astype(o_ref.dtype)


def _attn_prompt(qlat, qrope, ckv_b, kr_b, w_uv, bsz, t):
    tq = 256
    assert tq & (tq - 1) == 0 and t % tq == 0
    nq = t // tq
    rows = MLA_HEADS * tq
    return pl.pallas_call(
        functools.partial(_attn_prompt_kernel, tq=tq),
        grid=(bsz, nq),
        in_specs=[
            pl.BlockSpec((tq, MLA_HEADS * KV_LORA), lambda b, i: (b * nq + i, 0)),
            pl.BlockSpec((tq, MLA_HEADS * QK_ROPE), lambda b, i: (b * nq + i, 0)),
            pl.BlockSpec((t, KV_LORA), lambda b, i: (b, 0)),
            pl.BlockSpec((t, QK_ROPE), lambda b, i: (b, 0)),
            pl.BlockSpec(w_uv.shape, lambda b, i: (0, 0, 0)),
        ],
        out_specs=pl.BlockSpec((tq, MLA_HEADS * V_DIM), lambda b, i: (b * nq + i, 0)),
        out_shape=jax.ShapeDtypeStruct((bsz * t, MLA_HEADS * V_DIM), BF16),
        scratch_shapes=[pltpu.VMEM((rows, LANES), F32), pltpu.VMEM((rows, LANES), F32),
                        pltpu.VMEM((rows, LANES), F32), pltpu.VMEM((rows, KV_LORA), F32),
                        pltpu.VMEM((2, rows, tq), F32), pltpu.VMEM((2, rows, tq), BF16)],
        compiler_params=_params("parallel", "parallel"),
        name="attn_prompt",
    )(qlat, qrope, ckv_b, kr_b, w_uv)


PAGES_PER_STEP = 16
NEW_KEY_ROWS = 128


def _attn_sample_kernel(pt_ref, wq_ref, wr_ref, nc_ref, nr_ref, wuv_ref, *rest, t_new, n_q):
    del pt_ref
    pages_c = rest[:PAGES_PER_STEP]
    pages_r = rest[PAGES_PER_STEP:2 * PAGES_PER_STEP]
    o_ref, m_ref, l_ref, acc_ref, kv_ref, kq_ref, kr_ref = rest[2 * PAGES_PER_STEP:]
    j = pl.program_id(1)
    groups = LANES // n_q
    page = pages_c[0].shape[0]
    ppg = PAGES_PER_STEP // groups
    lat_rep = KV_LORA // LANES

    @pl.when(j == 0)
    def _():
        m_ref[...] = jnp.full_like(m_ref, NEG_BIG)
        l_ref[...] = jnp.zeros_like(l_ref)
        acc_ref[...] = jnp.zeros_like(acc_ref)

    for i in range(PAGES_PER_STEP):
        g, r = divmod(i, ppg)
        x = pages_c[i][...].astype(BF16)
        kv_ref[i * page:(i + 1) * page, :] = x
        kq_ref[r * page:(r + 1) * page, g * KV_LORA:(g + 1) * KV_LORA] = x
        kr_ref[g * QK_ROPE:(g + 1) * QK_ROPE, r * page:(r + 1) * page] = pages_r[i][...].astype(BF16)

    def lanes_to_rows(row):
        return jnp.broadcast_to(row, (LANES, LANES)).T

    def update(s, place, values):
        m_prev = m_ref[...]
        m_new = jnp.maximum(m_prev, jnp.max(s, axis=0, keepdims=True))
        alpha = jnp.exp2(m_prev - m_new)
        p = jnp.exp2(s - m_new)
        l_ref[...] = alpha * l_ref[...] + jnp.sum(p, axis=0, keepdims=True)
        m_ref[...] = m_new
        pv = _dot(place(p.T), values)
        acc_ref[...] = jnp.tile(lanes_to_rows(alpha), (1, lat_rep)) * acc_ref[...] + pv

    q_shift = n_q.bit_length() - 1

    def block_diag(p_t):
        row_g = lax.broadcasted_iota(jnp.int32, p_t.shape, 0) >> q_shift
        return jnp.concatenate([jnp.where(row_g == g, p_t, 0.0).astype(BF16) for g in range(groups)], axis=1)

    s = (_dot(kq_ref[...], wq_ref[...]) + _dot_tn(kr_ref[...], wr_ref[...])) * ATTN_EXP2_SCALE
    update(s, block_diag, kv_ref[...])

    @pl.when(j == pl.num_programs(1) - 1)
    def _():
        nc = nc_ref[...]
        s2 = (_dot(nc, wq_ref[:KV_LORA, :]) + _dot(nr_ref[...], wr_ref[:QK_ROPE, :])) * ATTN_EXP2_SCALE
        t_k = lax.broadcasted_iota(jnp.int32, s2.shape, 0)
        lane = lax.broadcasted_iota(jnp.int32, s2.shape, 1)
        ok = (lane < n_q) & (t_k <= (lane & (t_new - 1)))
        update(jnp.where(ok, s2, NEG_BIG), lambda p_t: p_t.astype(BF16), nc)

        grp = lambda a, g: a[g * n_q:(g + 1) * n_q]
        total = lambda a: functools.reduce(lambda x, y: x + y, [grp(a, g) for g in range(groups)])
        m_t = lanes_to_rows(m_ref[...])
        l_t = lanes_to_rows(l_ref[...])
        m_q = functools.reduce(jnp.maximum, [grp(m_t, g) for g in range(groups)])
        w = jnp.exp2(m_t - jnp.tile(m_q, (groups, 1)))
        l_q = total(w * l_t)
        o_lat = total(jnp.tile(w, (1, lat_rep)) * acc_ref[...]) / jnp.tile(l_q, (1, lat_rep))
        o_lat = o_lat.astype(BF16)
        head = lax.broadcasted_iota(jnp.int32, (n_q, V_DIM), 0) >> (t_new.bit_length() - 1)
        out = jnp.zeros((n_q, V_DIM), F32)
        for h in range(MLA_HEADS):
            out = out + jnp.where(head == h, _dot(o_lat, wuv_ref[h]), 0.0)
        o_ref[...] = out.astype(o_ref.dtype)


def _attn_sample(ql_t, qr_t, new_c, new_r, w_uv, cache_c, cache_r_t, page_table, t_new):
    bsz, _, n_q = ql_t.shape
    n_pages = page_table.shape[1]
    groups = LANES // n_q
    assert LANES % n_q == 0 and PAGES_PER_STEP % groups == 0 and n_pages % PAGES_PER_STEP == 0
    assert t_new & (t_new - 1) == 0 and n_q & (n_q - 1) == 0 and t_new <= NEW_KEY_ROWS and n_q >= SUBLANES
    page = cache_c.shape[1]
    keys = PAGES_PER_STEP * page

    eye = jnp.eye(groups, dtype=ql_t.dtype)
    place = lambda a: (eye[None, :, None, :, None] * a[:, None, :, None, :]).reshape(
        bsz, groups * a.shape[1], groups * n_q)
    wq, wr = place(ql_t), place(qr_t)

    per_b = lambda a: pl.BlockSpec((None,) + a.shape[1:], lambda b, j, pt: (b, 0, 0))

    def page_spec(i, a):
        return pl.BlockSpec((None,) + a.shape[1:], lambda b, j, pt: (pt[b, j * PAGES_PER_STEP + i], 0, 0))

    grid_spec = pltpu.PrefetchScalarGridSpec(
        num_scalar_prefetch=1,
        grid=(bsz, n_pages // PAGES_PER_STEP),
        in_specs=[per_b(wq), per_b(wr), per_b(new_c), per_b(new_r),
                  pl.BlockSpec(w_uv.shape, lambda b, j, pt: (0, 0, 0))]
        + [page_spec(i, cache_c) for i in range(PAGES_PER_STEP)]
        + [page_spec(i, cache_r_t) for i in range(PAGES_PER_STEP)],
        out_specs=pl.BlockSpec((None, n_q, V_DIM), lambda b, j, pt: (b, 0, 0)),
        scratch_shapes=[pltpu.VMEM((1, LANES), F32), pltpu.VMEM((1, LANES), F32),
                        pltpu.VMEM((LANES, KV_LORA), F32),
                        pltpu.VMEM((keys, KV_LORA), BF16),
                        pltpu.VMEM((keys // groups, groups * KV_LORA), BF16),
                        pltpu.VMEM((groups * QK_ROPE, keys // groups), BF16)],
    )
    return pl.pallas_call(
        functools.partial(_attn_sample_kernel, t_new=t_new, n_q=n_q),
        grid_spec=grid_spec,
        out_shape=jax.ShapeDtypeStruct((bsz, n_q, V_DIM), BF16),
        compiler_params=_params("parallel", "arbitrary"),
        name="attn_sample",
    )(page_table, wq, wr, new_c, new_r, w_uv, *([cache_c] * PAGES_PER_STEP),
      *([cache_r_t] * PAGES_PER_STEP))


def _rot_cols(w):
    half = w.shape[-1] // 2
    return jnp.concatenate([-w[..., half:], w[..., :half]], axis=-1)


def _prep_weights(norm_g, w_ffn_gate, w_ffn_up, w_ffn_down, w_in, q_norm_g, w_q_up, kv_norm_g,
                  w_kv_up, hg_norm_g, w_out):
    bf = lambda a: a.astype(BF16)
    w_h, w_cq, w_ckv, w_kr = jnp.split(w_in, [4 * HG_W, 4 * HG_W + Q_LORA, 4 * HG_W + Q_LORA + KV_LORA], axis=-1)
    w_q_rope = w_q_up[..., QK_NOPE:]
    return {
        "norm_g": norm_g.reshape(-1, 1, D_MODEL),
        "wg": bf(w_ffn_gate), "wu": bf(w_ffn_up), "wd": bf(w_ffn_down),
        "w_h": bf(w_h), "w_cq": bf(w_cq), "w_ckv": bf(w_ckv),
        "w_kr": bf(jnp.concatenate([w_kr, _rot_cols(w_kr)], axis=-1)),
        "q_norm_g": q_norm_g.reshape(1, Q_LORA), "kv_norm_g": kv_norm_g.reshape(1, KV_LORA),
        "w_qn": bf(w_q_up[..., :QK_NOPE].reshape(Q_LORA, MLA_HEADS * QK_NOPE)),
        "w_qr": bf(w_q_rope.reshape(Q_LORA, MLA_HEADS * QK_ROPE)),
        "w_qrr": bf(_rot_cols(w_q_rope).reshape(Q_LORA, MLA_HEADS * QK_ROPE)),
        "w_uk": bf(jnp.transpose(w_kv_up[..., :QK_NOPE], (1, 2, 0))),
        "w_uv": bf(jnp.transpose(w_kv_up[..., QK_NOPE:], (1, 0, 2))),
        "hg_norm_g": hg_norm_g.reshape(1, HG_DV),
        "w_out1": bf(w_out[:HG_W]), "w_out2": bf(w_out[HG_W:]),
    }


def _rope_tables(pos):
    half = QK_ROPE // 2
    inv = ROPE_THETA ** (-jnp.arange(half, dtype=F32) / half)
    ang = pos.astype(F32)[:, None] * inv[None, :]
    cos = jnp.tile(jnp.cos(ang), (1, 2))
    sin = jnp.tile(jnp.sin(ang), (1, 2))
    return (jnp.concatenate([cos, sin], axis=-1), jnp.tile(cos, (1, MLA_HEADS)), jnp.tile(sin, (1, MLA_HEADS)))


def kernel(x_prompt, x_sample, cache_kv_latent, cache_k_rope, state_hgrn, page_table, hgrn_lb_logits,
           norm_g, w_ffn_gate, w_ffn_up, w_ffn_down, w_in, q_norm_g, w_q_up, kv_norm_g, w_kv_up,
           hg_norm_g, w_out):
    bp, tp, _ = x_prompt.shape
    bs, ts, _ = x_sample.shape
    depth = norm_g.shape[0]
    assert depth == 1
    past_len = page_table.shape[1] * cache_kv_latent.shape[2]
    w = _prep_weights(norm_g[0], w_ffn_gate[0], w_ffn_up[0], w_ffn_down[0], w_in[0], q_norm_g[0],
                      w_q_up[0], kv_norm_g[0], w_kv_up[0], hg_norm_g[0], w_out[0])
    ng = w["norm_g"]
    lb_logits = hgrn_lb_logits.astype(F32)

    def pre(x, tables):
        h1 = _ffn_block(x, ng[0], ng[1], w["wg"][0], w["wu"][0], w["wd"][0])
        return (h1,) + tuple(_in_proj(h1, ng[2], *tables, w))

    def post(h1, o_h, o_m):
        return _ffn_block(h1, ng[4], ng[5], w["wg"][1], w["wu"][1], w["wd"][1],
                          mix=(o_h, o_m, w["w_out1"], w["w_out2"], ng[3]))

    h1, zh, ckv, kr, ckv_b, kr_b, qlat, qrope = pre(
        x_prompt.reshape(bp * tp, D_MODEL), _rope_tables(jnp.arange(tp, dtype=jnp.int32)))
    o_h, st_p = _hgrn(zh.reshape(bp, tp, 4 * HG_W), lb_logits, w["hg_norm_g"], None, chunk=64, t_valid=None)
    o_m = _attn_prompt(qlat, qrope, ckv_b, kr_b, w["w_uv"], bp, tp)
    y_p = post(h1, o_h.reshape(bp * tp, HG_W), o_m).reshape(bp, tp, D_MODEL)
    ckv_p = ckv.reshape(1, bp, tp, KV_LORA)
    kr_p = kr.reshape(1, bp, tp, QK_ROPE)

    pos_s = past_len + jnp.arange(ts, dtype=jnp.int32)
    tabs = tuple(jnp.tile(a, (bs, 1)) for a in _rope_tables(pos_s))
    h1s, zhs, ckvs, krs, ckvs_b, krs_b, qlats, qropes = pre(x_sample.reshape(bs * ts, D_MODEL), tabs)
    t_pad = -(-ts // SUBLANES) * SUBLANES
    zhs_pad = jnp.pad(zhs.reshape(bs, ts, 4 * HG_W), ((0, 0), (0, t_pad - ts), (0, 0)))
    o_hs, st_s = _hgrn(zhs_pad, lb_logits, w["hg_norm_g"], state_hgrn[0], chunk=SUBLANES,
                       t_valid=ts if t_pad != ts else None)
    q_cols = lambda a, wd: a.reshape(bs, ts, MLA_HEADS, wd).transpose(0, 3, 2, 1).reshape(bs, wd, MLA_HEADS * ts)
    pad_new = lambda a, wd: jnp.pad(a.reshape(bs, ts, wd), ((0, 0), (0, NEW_KEY_ROWS - ts), (0, 0)))
    o_ms = _attn_sample(q_cols(qlats, KV_LORA), q_cols(qropes, QK_ROPE),
                        pad_new(ckvs_b, KV_LORA), pad_new(krs_b, QK_ROPE), w["w_uv"],
                        cache_kv_latent[0], jnp.swapaxes(cache_k_rope[0], 1, 2), page_table, ts)
    o_ms = o_ms.reshape(bs, MLA_HEADS, ts, V_DIM).transpose(0, 2, 1, 3).reshape(bs * ts, MLA_HEADS * V_DIM)
    y_s = post(h1s, o_hs[:, :ts].reshape(bs * ts, HG_W), o_ms).reshape(bs, ts, D_MODEL)

    return (y_p, y_s, ckv_p, kr_p, st_p[None], ckvs.reshape(1, bs, ts, KV_LORA),
            krs.reshape(1, bs, ts, QK_ROPE), st_s[None])
```

```python
import functools
import math

import jax
import jax.numpy as jnp
from jax import lax
from jax.experimental import pallas as pl
from jax.experimental.pallas import tpu as pltpu

D_MODEL = 1024
D_FF = 2816
HG_HEADS = 4
HG_DK = 128
HG_DV = 128
MLA_HEADS = 4
QK_NOPE = 128
QK_ROPE = 64
V_DIM = 128
Q_LORA = 768
KV_LORA = 256
ROPE_THETA = 10000.0
EPS = 1e-6
HG_W = HG_HEADS * HG_DK

VMEM_LIMIT_BYTES = 56 * 1024 * 1024
SUBLANES = 8
NEG_BIG = -1e30
LOG2E = math.log2(math.e)

F32 = jnp.float32
BF16 = jnp.bfloat16


def _dot(a, b):
    return jnp.dot(a, b, preferred_element_type=F32)


def _dot_nt(a, b):
    return lax.dot_general(a, b, (((1,), (1,)), ((), ())), preferred_element_type=F32)


def _dot_tn(a, b):
    return lax.dot_general(a, b, (((0,), (0,)), ((), ())), preferred_element_type=F32)


def _rms(x, g):
    return x * lax.rsqrt(jnp.mean(x * x, axis=-1, keepdims=True) + EPS) * g


def _sigmoid(x):
    return 1.0 / (1.0 + jnp.exp(-x))


def _silu(x):
    return x * _sigmoid(x)


def _params(*sem):
    return pltpu.CompilerParams(dimension_semantics=sem, vmem_limit_bytes=VMEM_LIMIT_BYTES)


FFN_CHUNK = 256


def _ffn_kernel(*refs, mix_in):
    if mix_in:
        h_ref, oh_ref, om_ref, w1_ref, w2_ref, gmix_ref = refs[:6]
        refs = refs[6:]
        mixed = _dot(oh_ref[...], w1_ref[...]) + _dot(om_ref[...], w2_ref[...])
        x = h_ref[...] + _rms(mixed, gmix_ref[...])
    else:
        x = refs[0][...]
        refs = refs[1:]
    gpre_ref, gpost_ref, wg_ref, wu_ref, wd_ref, o_ref = refs
    xn = _rms(x, gpre_ref[...]).astype(BF16)
    acc = None
    for f in range(D_FF // FFN_CHUNK):
        cols = slice(f * FFN_CHUNK, (f + 1) * FFN_CHUNK)
        a = _dot(xn, wg_ref[:, cols])
        b = _dot(xn, wu_ref[:, cols])
        part = _dot((_silu(a) * b).astype(BF16), wd_ref[cols, :])
        acc = part if acc is None else acc + part
    o_ref[...] = x + 0.5 * _rms(acc, gpost_ref[...])


def _ffn_block(x, g_pre, g_post, wg, wu, wd, mix=None):
    m = x.shape[0]
    tm = min(m, 512)
    const = lambda a: pl.BlockSpec(a.shape, lambda i: (0, 0), pipeline_mode=pl.Buffered(1))
    row = lambda a: pl.BlockSpec((tm, a.shape[1]), lambda i: (i, 0))
    args, in_specs = [x], [row(x)]
    if mix is not None:
        o_h, o_m, w1, w2, g_mix = mix
        args += [o_h, o_m, w1, w2, g_mix]
        in_specs += [row(o_h), row(o_m), const(w1), const(w2), const(g_mix)]
    args += [g_pre, g_post, wg, wu, wd]
    in_specs += [const(g_pre), const(g_post), const(wg), const(wu), const(wd)]
    return pl.pallas_call(
        functools.partial(_ffn_kernel, mix_in=mix is not None),
        grid=(m // tm,),
        in_specs=in_specs,
        out_specs=row(x),
        out_shape=jax.ShapeDtypeStruct((m, D_MODEL), F32),
        compiler_params=_params("parallel"),
        name="ffn_mix_block" if mix is not None else "ffn_block",
    )(*args)


def _inproj_kernel(x_ref, g_ref, csk_ref, cosq_ref, sinq_ref, wh_ref, wcq_ref, wckv_ref, wkr_ref,
                   qg_ref, kvg_ref, wqn_ref, wqr_ref, wqrr_ref, wuk_ref,
                   zh_ref, ckv_ref, kr_ref, ckvb_ref, krb_ref, qlat_ref, qrope_ref):
    u = _rms(x_ref[...], g_ref[...]).astype(BF16)
    zh_ref[...] = _dot(u, wh_ref[...])
    ckv = _rms(_dot(u, wckv_ref[...]), kvg_ref[...])
    ckv_ref[...] = ckv
    ckvb_ref[...] = ckv.astype(BF16)
    kz = _dot(u, wkr_ref[...])
    csk = csk_ref[...]
    kr = kz[:, :QK_ROPE] * csk[:, :QK_ROPE] + kz[:, QK_ROPE:] * csk[:, QK_ROPE:]
    kr_ref[...] = kr
    krb_ref[...] = kr.astype(BF16)
    cqn = _rms(_dot(u, wcq_ref[...]), qg_ref[...]).astype(BF16)
    qn = _dot(cqn, wqn_ref[...]).astype(BF16)
    for h in range(MLA_HEADS):
        qlat_ref[:, h * KV_LORA:(h + 1) * KV_LORA] = _dot(
            qn[:, h * QK_NOPE:(h + 1) * QK_NOPE], wuk_ref[h]).astype(BF16)
    qr = _dot(cqn, wqr_ref[...]) * cosq_ref[...] + _dot(cqn, wqrr_ref[...]) * sinq_ref[...]
    qrope_ref[...] = qr.astype(BF16)


def _in_proj(x, g, csk, cosq, sinq, w):
    m = x.shape[0]
    tm = min(m, 512)
    nrep = csk.shape[0] // tm
    full = lambda a: pl.BlockSpec(a.shape, lambda i: (0,) * a.ndim, pipeline_mode=pl.Buffered(1))
    row = lambda width: pl.BlockSpec((tm, width), lambda i: (i, 0))
    tab = lambda width: pl.BlockSpec((tm, width), lambda i: (i % nrep, 0))
    rope_w = MLA_HEADS * QK_ROPE
    return pl.pallas_call(
        _inproj_kernel,
        grid=(m // tm,),
        in_specs=[row(D_MODEL), full(g), tab(2 * QK_ROPE), tab(rope_w), tab(rope_w),
                  full(w["w_h"]), full(w["w_cq"]), full(w["w_ckv"]), full(w["w_kr"]),
                  full(w["q_norm_g"]), full(w["kv_norm_g"]), full(w["w_qn"]), full(w["w_qr"]),
                  full(w["w_qrr"]), full(w["w_uk"])],
        out_specs=[row(4 * HG_W), row(KV_LORA), row(QK_ROPE), row(KV_LORA), row(QK_ROPE),
                   row(MLA_HEADS * KV_LORA), row(rope_w)],
        out_shape=[
            jax.ShapeDtypeStruct((m, 4 * HG_W), F32),
            jax.ShapeDtypeStruct((m, KV_LORA), F32),
            jax.ShapeDtypeStruct((m, QK_ROPE), F32),
            jax.ShapeDtypeStruct((m, KV_LORA), BF16),
            jax.ShapeDtypeStruct((m, QK_ROPE), BF16),
            jax.ShapeDtypeStruct((m, MLA_HEADS * KV_LORA), BF16),
            jax.ShapeDtypeStruct((m, rope_w), BF16),
        ],
        compiler_params=_params("parallel"),
        name="in_proj",
    )(x, g, csk, cosq, sinq, w["w_h"], w["w_cq"], w["w_ckv"], w["w_kr"], w["q_norm_g"],
      w["kv_norm_g"], w["w_qn"], w["w_qr"], w["w_qrr"], w["w_uk"])


def _split3(g):
    g1 = g.astype(BF16).astype(F32)
    r1 = g - g1
    g2 = r1.astype(BF16).astype(F32)
    g3 = (r1 - g2).astype(BF16).astype(F32)
    return g1, g2, g3


HG_SUB = 16


def _hgrn_kernel(*refs, chunk, t_valid, has_s0):
    if has_s0:
        hq_ref, hf_ref, hi_ref, hg_ref, lbl_ref, ng_ref, s0_ref = refs[:7]
        refs = refs[7:]
    else:
        hq_ref, hf_ref, hi_ref, hg_ref, lbl_ref, ng_ref = refs[:6]
        s0_ref = None
        refs = refs[6:]
    o_ref, sfin_ref, st_ref, c_ref, v_ref = refs
    ti = pl.program_id(1)
    t_tile = hq_ref.shape[0]
    n_chunks = t_tile // chunk
    n_tiles = chunk // SUBLANES

    lbl = lbl_ref[...]
    e = jnp.exp(lbl - jnp.max(lbl, axis=0, keepdims=True))
    lb_all = e[0:1, :] / jnp.sum(e, axis=0, keepdims=True)

    @pl.when(ti == 0)
    def _():
        for h in range(HG_HEADS):
            st_ref[h] = s0_ref[h].T if has_s0 else jnp.zeros((HG_DV, HG_DK), F32)

    r_i = lax.broadcasted_iota(jnp.int32, (chunk, chunk), 0)
    c_i = lax.broadcasted_iota(jnp.int32, (chunk, chunk), 1)
    tril = (c_i <= r_i).astype(F32)
    row8 = lax.broadcasted_iota(jnp.int32, (SUBLANES, HG_DK), 0)
    ng = ng_ref[...]
    ones = jnp.ones((HG_DK, HG_DV), BF16)
    sub = min(chunk, HG_SUB)
    n_sub = chunk // sub
    pairs = [(s, i) for s in range(chunk) for i in range(s // SUBLANES, (s // sub + 1) * sub // SUBLANES)]
    if n_sub > 1:
        assert sub & (sub - 1) == 0 and chunk & (chunk - 1) == 0
        rows_sub = lax.broadcasted_iota(jnp.int32, (chunk, (n_sub - 1) * chunk), 0) >> (sub.bit_length() - 1)
        cols_blk = lax.broadcasted_iota(jnp.int32, (chunk, (n_sub - 1) * chunk), 1) >> (chunk.bit_length() - 1)
        sub_mask = cols_blk == rows_sub - 1

    def gates(h, start):
        sl = pl.ds(start, chunk)
        hl = slice(h * HG_DK, (h + 1) * HG_DK)
        lb = lb_all[:, hl]
        q = _silu(hq_ref[sl, hl])
        f = lb + (1.0 - lb) * _sigmoid(hf_ref[sl, hl])
        g = jnp.log(f)
        k = 1.0 - f
        v = hi_ref[sl, hl]
        if t_valid is not None:
            pos = lax.broadcasted_iota(jnp.int32, (chunk, HG_DK), 0) + (ti * t_tile + start)
            valid = pos < t_valid
            g = jnp.where(valid, g, 0.0)
            k = jnp.where(valid, k, 0.0)
            v = jnp.where(valid, v, 0.0)
        g1, g2, g3 = _split3(g)
        b = _dot(tril, g1) + _dot(tril, g2) + _dot(tril, g3)
        return q, k, v, b * LOG2E

    def products(h, q, k, v, b2):
        c_ref[h] = b2 - jnp.log(k) * LOG2E
        v_ref[h] = v
        q_t = [q[i * SUBLANES:(i + 1) * SUBLANES, :] for i in range(n_tiles)]
        b_t = [b2[i * SUBLANES:(i + 1) * SUBLANES, :] for i in range(n_tiles)]
        p_tiles = []
        for s, i in pairs:
            d = b_t[i] - c_ref[h, s:s + 1, :]
            if i == s // SUBLANES and s % SUBLANES:
                d = jnp.where(row8 >= s % SUBLANES, d, NEG_BIG)
            p_tiles.append(q_t[i] * jnp.exp2(d))
        a_rep = _dot(jnp.concatenate(p_tiles, axis=0).astype(BF16), ones)

        a_sub = None
        if n_sub > 1:
            zeros_sub = jnp.zeros((sub, HG_DK), F32)
            q_parts, k_parts = [zeros_sub], []
            for j in range(1, n_sub):
                lo = j * sub
                r_j = b2[lo - 1:lo, :]
                q_parts.append(q[lo:lo + sub, :] * jnp.exp2(b2[lo:lo + sub, :] - r_j))
                k_parts += [k[:lo, :] * jnp.exp2(r_j - b2[:lo, :])] + [zeros_sub] * (n_sub - j)
            a_sub = _dot_nt(jnp.concatenate(q_parts, axis=0).astype(BF16),
                            jnp.concatenate(k_parts, axis=0).astype(BF16))

        st = st_ref[h]
        o_state = _dot_nt((q * jnp.exp2(b2)).astype(BF16), st.astype(BF16))
        bl = b2[chunk - 1:chunk, :]
        v_b = v.astype(BF16)
        st_ref[h] = st * jnp.exp2(bl) + _dot_tn(v_b, (k * jnp.exp2(bl - b2)).astype(BF16))
        return a_rep, a_sub, o_state, v_b

    def outputs(h, start, a_rep, a_sub, o_state, v_b):
        sl = pl.ds(start, chunk)
        hl = slice(h * HG_DK, (h + 1) * HG_DK)
        o_t = [o_state[i * SUBLANES:(i + 1) * SUBLANES, :] for i in range(n_tiles)]
        for j, (s, i) in enumerate(pairs):
            o_t[i] = o_t[i] + a_rep[j * SUBLANES:(j + 1) * SUBLANES, :] * v_ref[h, s:s + 1, :]
        o = jnp.concatenate(o_t, axis=0) if n_tiles > 1 else o_t[0]
        if a_sub is not None:
            o = o + _dot(jnp.where(sub_mask, a_sub, 0.0).astype(BF16),
                         jnp.concatenate([v_b] * (n_sub - 1), axis=0))
        o_ref[sl, hl] = (_rms(o, ng) * _silu(hg_ref[sl, hl])).astype(o_ref.dtype)

    def chunk_body(c, carry):
        start = pl.multiple_of(c * chunk, chunk)
        heads = range(HG_HEADS)
        gated = [gates(h, start) for h in heads]
        prods = [products(h, *gated[h]) for h in heads]
        for h in heads:
            outputs(h, start, *prods[h])
        return carry

    lax.fori_loop(0, n_chunks, chunk_body, 0)

    @pl.when(ti == pl.num_programs(1) - 1)
    def _():
        for h in range(HG_HEADS):
            sfin_ref[h] = st_ref[h].T


def _hgrn(zh, lb_logits, ng, s0, chunk, t_valid):
    bsz, t, _ = zh.shape
    has_s0 = s0 is not None
    t_tile = min(t, 512)
    assert t % t_tile == 0 and t_tile % chunk == 0
    col = lambda j: pl.BlockSpec((None, t_tile, HG_W), lambda b, i: (b, i, j))
    in_specs = [col(0), col(1), col(2), col(3),
                pl.BlockSpec(lb_logits.shape, lambda b, i: (0, 0)),
                pl.BlockSpec((1, HG_DV), lambda b, i: (0, 0))]
    args = [zh, zh, zh, zh, lb_logits, ng]
    st_spec = pl.BlockSpec((None, HG_HEADS, HG_DK, HG_DV), lambda b, i: (b, 0, 0, 0))
    if has_s0:
        in_specs.append(st_spec)
        args.append(s0)
    return pl.pallas_call(
        functools.partial(_hgrn_kernel, chunk=chunk, t_valid=t_valid, has_s0=has_s0),
        grid=(bsz, t // t_tile),
        in_specs=in_specs,
        out_specs=[pl.BlockSpec((None, t_tile, HG_W), lambda b, i: (b, i, 0)), st_spec],
        out_shape=[jax.ShapeDtypeStruct((bsz, t, HG_W), BF16),
                   jax.ShapeDtypeStruct((bsz, HG_HEADS, HG_DK, HG_DV), F32)],
        scratch_shapes=[pltpu.VMEM((HG_HEADS, HG_DV, HG_DK), F32),
                        pltpu.VMEM((HG_HEADS, chunk, HG_DK), F32),
                        pltpu.VMEM((HG_HEADS, chunk, HG_DV), F32)],
        compiler_params=_params("parallel", "arbitrary"),
        name="hgrn",
    )(*args)


ATTN_SCALE = 1.0 / math.sqrt(QK_NOPE + QK_ROPE)
ATTN_EXP2_SCALE = ATTN_SCALE * LOG2E

LANES = 128


def _attn_prompt_kernel(ql_ref, qr_ref, kc_ref, kr_ref, wuv_ref, o_ref,
                        m_ref, l_ref, a_ref, acc_ref, s_ref, p_ref, *, tq):
    qi = pl.program_id(1)
    ql = jnp.concatenate([ql_ref[:, h * KV_LORA:(h + 1) * KV_LORA] for h in range(MLA_HEADS)], axis=0)
    qr = jnp.concatenate([qr_ref[:, h * QK_ROPE:(h + 1) * QK_ROPE] for h in range(MLA_HEADS)], axis=0)
    m_ref[...] = jnp.full_like(m_ref, NEG_BIG)
    l_ref[...] = jnp.zeros_like(l_ref)
    a_ref[...] = jnp.zeros_like(a_ref)
    acc_ref[...] = jnp.zeros_like(acc_ref)
    p_ref[1] = jnp.zeros(p_ref.shape[1:], BF16)

    def keys(kb):
        return pl.ds(pl.multiple_of(kb * tq, tq), tq)

    def scores(kb):
        return (_dot_nt(ql, kc_ref[keys(kb), :]) + _dot_nt(qr, kr_ref[keys(kb), :])) * ATTN_EXP2_SCALE

    def add_values(kb, slot):
        acc_ref[...] = (jnp.tile(a_ref[...], (1, KV_LORA // LANES)) * acc_ref[...]
                        + _dot(p_ref[slot], kc_ref[keys(kb), :]))

    def softmax(slot, masked):
        s = s_ref[slot]
        if masked:
            rows = lax.broadcasted_iota(jnp.int32, s.shape, 0) & (tq - 1)
            cols = lax.broadcasted_iota(jnp.int32, s.shape, 1)
            s = jnp.where(cols <= rows, s, NEG_BIG)
        m_prev = m_ref[...]
        m_new = jnp.maximum(m_prev, jnp.max(s, axis=-1, keepdims=True))
        alpha = jnp.exp2(m_prev - m_new)
        p = jnp.exp2(s - jnp.tile(m_new, (1, tq // LANES)))
        l_ref[...] = alpha * l_ref[...] + jnp.sum(p, axis=-1, keepdims=True)
        m_ref[...] = m_new
        a_ref[...] = alpha
        p_ref[slot] = p.astype(BF16)

    s_ref[0] = scores(0)

    def iteration(i, cur):
        s_ref[1 - cur] = scores(i + 1)
        add_values(jnp.maximum(i - 1, 0), 1 - cur)
        softmax(cur, False)

    def last(cur):
        add_values(jnp.maximum(qi - 1, 0), 1 - cur)
        softmax(cur, True)
        add_values(qi, cur)

    def pair(j, carry):
        iteration(2 * j, 0)
        iteration(2 * j + 1, 1)
        return carry

    lax.fori_loop(0, qi >> 1, pair, 0)

    @pl.when((qi & 1) == 1)
    def _():
        iteration(qi - 1, 0)
        last(1)

    @pl.when((qi & 1) == 0)
    def _():
        last(0)

    o_lat =(acc_ref[...] / jnp.tile(l_ref[...], (1, KV_LORA // LANES))).astype(BF16)
    for h in range(MLA_HEADS):
        o_ref[:, h * V_DIM:(h + 1) * V_DIM] = _dot(o_lat[h * tq:(h + 1) * tq, :], wuv_ref[h]).astype(o_ref.dtype)


def _attn_prompt(qlat, qrope, ckv_b, kr_b, w_uv, bsz, t):
    tq = 256
    assert tq & (tq - 1) == 0 and t % tq == 0
    nq = t // tq
    rows = MLA_HEADS * tq
    return pl.pallas_call(
        functools.partial(_attn_prompt_kernel, tq=tq),
        grid=(bsz, nq),
        in_specs=[
            pl.BlockSpec((tq, MLA_HEADS * KV_LORA), lambda b, i: (b * nq + i, 0)),
            pl.BlockSpec((tq, MLA_HEADS * QK_ROPE), lambda b, i: (b * nq + i, 0)),
            pl.BlockSpec((t, KV_LORA), lambda b, i: (b, 0)),
            pl.BlockSpec((t, QK_ROPE), lambda b, i: (b, 0)),
            pl.BlockSpec(w_uv.shape, lambda b, i: (0, 0, 0)),
        ],
        out_specs=pl.BlockSpec((tq, MLA_HEADS * V_DIM), lambda b, i: (b * nq + i, 0)),
        out_shape=jax.ShapeDtypeStruct((bsz * t, MLA_HEADS * V_DIM), BF16),
        scratch_shapes=[pltpu.VMEM((rows, LANES), F32), pltpu.VMEM((rows, LANES), F32),
                        pltpu.VMEM((rows, LANES), F32), pltpu.VMEM((rows, KV_LORA), F32),
                        pltpu.VMEM((2, rows, tq), F32), pltpu.VMEM((2, rows, tq), BF16)],
        compiler_params=_params("parallel", "parallel"),
        name="attn_prompt",
    )(qlat, qrope, ckv_b, kr_b, w_uv)


PAGES_PER_STEP = 16
NEW_KEY_ROWS = 128


def _attn_sample_kernel(pt_ref, ql_ref, qr_ref, nc_ref, nr_ref, wuv_ref, cc_hbm, cr_hbm, o_ref,
                        cbuf, rbuf, sem, wq_ref, wr_ref, m_ref, l_ref, acc_ref, kv_ref, kq_ref, kr_ref,
                        *, t_new, n_q, n_steps):
    b = pl.program_id(0)
    nb = pl.num_programs(0)
    groups = LANES // n_q
    page = cbuf.shape[1] // PAGES_PER_STEP
    ppg = PAGES_PER_STEP // groups
    lat_rep = KV_LORA // LANES

    def page_copies(bb, step, slot):
        copies = []
        for i in range(PAGES_PER_STEP):
            pid = pt_ref[bb, step * PAGES_PER_STEP + i]
            copies.append(pltpu.make_async_copy(cc_hbm.at[pid], cbuf.at[slot, pl.ds(i * page, page), :],
                                                sem.at[slot]))
            copies.append(pltpu.make_async_copy(cr_hbm.at[pid], rbuf.at[slot, i], sem.at[slot]))
        return copies

    @pl.when(b == 0)
    def _():
        for c in page_copies(0, 0, 0):
            c.start()

    q_shift = n_q.bit_length() - 1
    spread = (lax.broadcasted_iota(jnp.int32, (n_q, LANES), 1) & (n_q - 1)) == lax.broadcasted_iota(
        jnp.int32, (n_q, LANES), 0)
    spread = jnp.where(spread, 1.0, 0.0).astype(BF16)
    for src, dst in ((ql_ref, wq_ref), (qr_ref, wr_ref)):
        feat = src.shape[0]
        tiled = _dot(src[...], spread)
        lane_g = lax.broadcasted_iota(jnp.int32, tiled.shape, 1) >> q_shift
        for g in range(groups):
            dst[g * feat:(g + 1) * feat, :] = jnp.where(lane_g == g, tiled, 0.0).astype(BF16)

    m_ref[...] = jnp.full_like(m_ref, NEG_BIG)
    l_ref[...] = jnp.zeros_like(l_ref)
    acc_ref[...] = jnp.zeros_like(acc_ref)

    def lanes_to_rows(row):
        return jnp.broadcast_to(row, (LANES, LANES)).T

    def update(s, place, values):
        m_prev = m_ref[...]
        m_new = jnp.maximum(m_prev, jnp.max(s, axis=0, keepdims=True))
        alpha = jnp.exp2(m_prev - m_new)
        p = jnp.exp2(s - m_new)
        l_ref[...] = alpha * l_ref[...] + jnp.sum(p, axis=0, keepdims=True)
        m_ref[...] = m_new
        pv = _dot(place(p.T), values)
        acc_ref[...] = jnp.tile(lanes_to_rows(alpha), (1, lat_rep)) * acc_ref[...] + pv

    def block_diag(p_t):
        row_g = lax.broadcasted_iota(jnp.int32, p_t.shape, 0) >> q_shift
        return jnp.concatenate([jnp.where(row_g == g, p_t, 0.0).astype(BF16) for g in range(groups)], axis=1)

    for j in range(n_steps):
        slot = j & 1
        if j + 1 < n_steps:
            nxt = page_copies(b, j + 1, 1 - slot)
        else:
            nxt = page_copies(jnp.minimum(b + 1, nb - 1), 0, 1 - slot)
        for c in nxt:
            c.start()
        for c in page_copies(b, j, slot):
            c.wait()
        for i in range(PAGES_PER_STEP):
            g, r = divmod(i, ppg)
            x = cbuf[slot, i * page:(i + 1) * page, :].astype(BF16)
            kv_ref[i * page:(i + 1) * page, :] = x
            kq_ref[r * page:(r + 1) * page, g * KV_LORA:(g + 1) * KV_LORA] = x
            kr_ref[g * QK_ROPE:(g + 1) * QK_ROPE, r * page:(r + 1) * page] = rbuf[slot, i].astype(BF16)
        s = (_dot(kq_ref[...], wq_ref[...]) + _dot_tn(kr_ref[...], wr_ref[...])) * ATTN_EXP2_SCALE
        update(s, block_diag, kv_ref[...])

    @pl.when(b == nb - 1)
    def _():
        for c in page_copies(b, 0, n_steps & 1):
            c.wait()

    nc = nc_ref[...]
    s2 = (_dot(nc, wq_ref[:KV_LORA, :]) + _dot(nr_ref[...], wr_ref[:QK_ROPE, :])) * ATTN_EXP2_SCALE
    t_k = lax.broadcasted_iota(jnp.int32, s2.shape, 0)
    lane = lax.broadcasted_iota(jnp.int32, s2.shape, 1)
    ok = (lane < n_q) & (t_k <= (lane & (t_new - 1)))
    update(jnp.where(ok, s2, NEG_BIG), lambda p_t: p_t.astype(BF16), nc)

    grp = lambda a, g: a[g * n_q:(g + 1) * n_q]
    total = lambda a: functools.reduce(lambda x, y: x + y, [grp(a, g) for g in range(groups)])
    m_t = lanes_to_rows(m_ref[...])
    l_t = lanes_to_rows(l_ref[...])
    m_q = functools.reduce(jnp.maximum, [grp(m_t, g) for g in range(groups)])
    w = jnp.exp2(m_t - jnp.tile(m_q, (groups, 1)))
    l_q = total(w * l_t)
    o_lat = total(jnp.tile(w, (1, lat_rep)) * acc_ref[...]) / jnp.tile(l_q, (1, lat_rep))
    o_lat = o_lat.astype(BF16)
    head = lax.broadcasted_iota(jnp.int32, (n_q, V_DIM), 0) >> (t_new.bit_length() - 1)
    out = jnp.zeros((n_q, V_DIM), F32)
    for h in range(MLA_HEADS):
        out = out + jnp.where(head == h, _dot(o_lat, wuv_ref[h]), 0.0)
    o_ref[...] = out.astype(o_ref.dtype)


def _attn_sample(ql_t, qr_t, new_c, new_r, w_uv, cache_c, cache_r_t, page_table, t_new):
    bsz, _, n_q = ql_t.shape
    n_pages = page_table.shape[1]
    groups = LANES // n_q
    assert LANES % n_q == 0 and PAGES_PER_STEP % groups == 0 and n_pages % PAGES_PER_STEP == 0
    assert t_new & (t_new - 1) == 0 and n_q & (n_q - 1) == 0 and t_new <= NEW_KEY_ROWS and n_q >= SUBLANES
    page = cache_c.shape[1]
    keys = PAGES_PER_STEP * page
    n_steps = n_pages // PAGES_PER_STEP
    assert n_steps % 2 == 0

    per_b = lambda a: pl.BlockSpec((None,) + a.shape[1:], lambda b, pt: (b, 0, 0))
    grid_spec = pltpu.PrefetchScalarGridSpec(
        num_scalar_prefetch=1,
        grid=(bsz,),
        in_specs=[per_b(ql_t), per_b(qr_t), per_b(new_c), per_b(new_r),
                  pl.BlockSpec(w_uv.shape, lambda b, pt: (0, 0, 0)),
                  pl.BlockSpec(memory_space=pl.ANY), pl.BlockSpec(memory_space=pl.ANY)],
        out_specs=pl.BlockSpec((None, n_q, V_DIM), lambda b, pt: (b, 0, 0)),
        scratch_shapes=[pltpu.VMEM((2, keys, KV_LORA), cache_c.dtype),
                        pltpu.VMEM((2, PAGES_PER_STEP, QK_ROPE, page), cache_r_t.dtype),
                        pltpu.SemaphoreType.DMA((2,)),
                        pltpu.VMEM((groups * KV_LORA, LANES), BF16),
                        pltpu.VMEM((groups * QK_ROPE, LANES), BF16),
                        pltpu.VMEM((1, LANES), F32), pltpu.VMEM((1, LANES), F32),
                        pltpu.VMEM((LANES, KV_LORA), F32),
                        pltpu.VMEM((keys, KV_LORA), BF16),
                        pltpu.VMEM((keys // groups, groups * KV_LORA), BF16),
                        pltpu.VMEM((groups * QK_ROPE, keys // groups), BF16)],
    )
    return pl.pallas_call(
        functools.partial(_attn_sample_kernel, t_new=t_new, n_q=n_q, n_steps=n_steps),
        grid_spec=grid_spec,
        out_shape=jax.ShapeDtypeStruct((bsz, n_q, V_DIM), BF16),
        compiler_params=_params("arbitrary"),
        name="attn_sample",
    )(page_table, ql_t, qr_t, new_c, new_r, w_uv, cache_c, cache_r_t)


def _rot_cols(w):
    half = w.shape[-1] // 2
    return jnp.concatenate([-w[..., half:], w[..., :half]], axis=-1)


def _prep_weights(norm_g, w_ffn_gate, w_ffn_up, w_ffn_down, w_in, q_norm_g, w_q_up, kv_norm_g,
                  w_kv_up, hg_norm_g, w_out):
    bf = lambda a: a.astype(BF16)
    w_h, w_cq, w_ckv, w_kr = jnp.split(w_in, [4 * HG_W, 4 * HG_W + Q_LORA, 4 * HG_W + Q_LORA + KV_LORA], axis=-1)
    w_q_rope = w_q_up[..., QK_NOPE:]
    return {
        "norm_g": norm_g.reshape(-1, 1, D_MODEL),
        "wg": bf(w_ffn_gate), "wu": bf(w_ffn_up), "wd": bf(w_ffn_down),
        "w_h": bf(w_h), "w_cq": bf(w_cq), "w_ckv": bf(w_ckv),
        "w_kr": bf(jnp.concatenate([w_kr, _rot_cols(w_kr)], axis=-1)),
        "q_norm_g": q_norm_g.reshape(1, Q_LORA), "kv_norm_g": kv_norm_g.reshape(1, KV_LORA),
        "w_qn": bf(w_q_up[..., :QK_NOPE].reshape(Q_LORA, MLA_HEADS * QK_NOPE)),
        "w_qr": bf(w_q_rope.reshape(Q_LORA, MLA_HEADS * QK_ROPE)),
        "w_qrr": bf(_rot_cols(w_q_rope).reshape(Q_LORA, MLA_HEADS * QK_ROPE)),
        "w_uk": bf(jnp.transpose(w_kv_up[..., :QK_NOPE], (1, 2, 0))),
        "w_uv": bf(jnp.transpose(w_kv_up[..., QK_NOPE:], (1, 0, 2))),
        "hg_norm_g": hg_norm_g.reshape(1, HG_DV),
        "w_out1": bf(w_out[:HG_W]), "w_out2": bf(w_out[HG_W:]),
    }


def _rope_tables(pos):
    half = QK_ROPE // 2
    inv = ROPE_THETA ** (-jnp.arange(half, dtype=F32) / half)
    ang = pos.astype(F32)[:, None] * inv[None, :]
    cos = jnp.tile(jnp.cos(ang), (1, 2))
    sin = jnp.tile(jnp.sin(ang), (1, 2))
    return (jnp.concatenate([cos, sin], axis=-1), jnp.tile(cos, (1, MLA_HEADS)), jnp.tile(sin, (1, MLA_HEADS)))


def kernel(x_prompt, x_sample, cache_kv_latent, cache_k_rope, state_hgrn, page_table, hgrn_lb_logits,
           norm_g, w_ffn_gate, w_ffn_up, w_ffn_down, w_in, q_norm_g, w_q_up, kv_norm_g, w_kv_up,
           hg_norm_g, w_out):
    bp, tp, _ = x_prompt.shape
    bs, ts, _ = x_sample.shape
    depth = norm_g.shape[0]
    assert depth == 1
    past_len = page_table.shape[1] * cache_kv_latent.shape[2]
    w = _prep_weights(norm_g[0], w_ffn_gate[0], w_ffn_up[0], w_ffn_down[0], w_in[0], q_norm_g[0],
                      w_q_up[0], kv_norm_g[0], w_kv_up[0], hg_norm_g[0], w_out[0])
    ng = w["norm_g"]
    lb_logits = hgrn_lb_logits.astype(F32)

    def pre(x, tables):
        h1 = _ffn_block(x, ng[0], ng[1], w["wg"][0], w["wu"][0], w["wd"][0])
        return (h1,) + tuple(_in_proj(h1, ng[2], *tables, w))

    def post(h1, o_h, o_m):
        return _ffn_block(h1, ng[4], ng[5], w["wg"][1], w["wu"][1], w["wd"][1],
                          mix=(o_h, o_m, w["w_out1"], w["w_out2"], ng[3]))

    h1, zh, ckv, kr, ckv_b, kr_b, qlat, qrope = pre(
        x_prompt.reshape(bp * tp, D_MODEL), _rope_tables(jnp.arange(tp, dtype=jnp.int32)))
    o_h, st_p = _hgrn(zh.reshape(bp, tp, 4 * HG_W), lb_logits, w["hg_norm_g"], None, chunk=64, t_valid=None)
    o_m = _attn_prompt(qlat, qrope, ckv_b, kr_b, w["w_uv"], bp, tp)
    y_p = post(h1, o_h.reshape(bp * tp, HG_W), o_m).reshape(bp, tp, D_MODEL)
    ckv_p = ckv.reshape(1, bp, tp, KV_LORA)
    kr_p = kr.reshape(1, bp, tp, QK_ROPE)

    pos_s = past_len + jnp.arange(ts, dtype=jnp.int32)
    tabs = tuple(jnp.tile(a, (bs, 1)) for a in _rope_tables(pos_s))
    h1s, zhs, ckvs, krs, ckvs_b, krs_b, qlats, qropes = pre(x_sample.reshape(bs * ts, D_MODEL), tabs)
    t_pad = -(-ts // SUBLANES) * SUBLANES
    zhs_pad = jnp.pad(zhs.reshape(bs, ts, 4 * HG_W), ((0, 0), (0, t_pad - ts), (0, 0)))
    o_hs, st_s = _hgrn(zhs_pad, lb_logits, w["hg_norm_g"], state_hgrn[0], chunk=SUBLANES,
                       t_valid=ts if t_pad != ts else None)
    q_cols = lambda a, wd: a.reshape(bs, ts, MLA_HEADS, wd).transpose(0, 3, 2, 1).reshape(bs, wd, MLA_HEADS * ts)
    pad_new = lambda a, wd: jnp.pad(a.reshape(bs, ts, wd), ((0, 0), (0, NEW_KEY_ROWS - ts), (0, 0)))
    o_ms = _attn_sample(q_cols(qlats, KV_LORA), q_cols(qropes, QK_ROPE),
                        pad_new(ckvs_b, KV_LORA), pad_new(krs_b, QK_ROPE), w["w_uv"],
                        cache_kv_latent[0], jnp.swapaxes(cache_k_rope[0], 1, 2), page_table, ts)
    o_ms = o_ms.reshape(bs, MLA_HEADS, ts, V_DIM).transpose(0, 2, 1, 3).reshape(bs * ts, MLA_HEADS * V_DIM)
    y_s = post(h1s, o_hs[:, :ts].reshape(bs * ts, HG_W), o_ms).reshape(bs, ts, D_MODEL)

    return (y_p, y_s, ckv_p, kr_p, st_p[None], ckvs.reshape(1, bs, ts, KV_LORA),
            krs.reshape(1, bs, ts, QK_ROPE), st_s[None])
```

```python
import functools
import math

import jax
import jax.numpy as jnp
from jax import lax
from jax.experimental import pallas as pl
from jax.experimental.pallas import tpu as pltpu

D_MODEL = 1024
D_FF = 2816
HG_HEADS = 4
HG_DK = 128
HG_DV = 128
MLA_HEADS = 4
QK_NOPE = 128
QK_ROPE = 64
V_DIM = 128
Q_LORA = 768
KV_LORA = 256
ROPE_THETA = 10000.0
EPS = 1e-6
HG_W = HG_HEADS * HG_DK

VMEM_LIMIT_BYTES = 56 * 1024 * 1024
SUBLANES = 8
NEG_BIG = -1e30
LOG2E = math.log2(math.e)

F32 = jnp.float32
BF16 = jnp.bfloat16


def _dot(a, b):
    return jnp.dot(a, b, preferred_element_type=F32)


def _dot_nt(a, b):
    return lax.dot_general(a, b, (((1,), (1,)), ((), ())), preferred_element_type=F32)


def _dot_tn(a, b):
    return lax.dot_general(a, b, (((0,), (0,)), ((), ())), preferred_element_type=F32)


def _rms(x, g):
    return x * lax.rsqrt(jnp.mean(x * x, axis=-1, keepdims=True) + EPS) * g


def _sigmoid(x):
    return 1.0 / (1.0 + jnp.exp(-x))


def _silu(x):
    return x * _sigmoid(x)


def _params(*sem):
    return pltpu.CompilerParams(dimension_semantics=sem, vmem_limit_bytes=VMEM_LIMIT_BYTES)


FFN_CHUNK = 256


def _ffn_kernel(*refs, mix_in):
    if mix_in:
        h_ref, oh_ref, om_ref, w1_ref, w2_ref, gmix_ref = refs[:6]
        refs = refs[6:]
        mixed = _dot(oh_ref[...], w1_ref[...]) + _dot(om_ref[...], w2_ref[...])
        x = h_ref[...] + _rms(mixed, gmix_ref[...])
    else:
        x = refs[0][...]
        refs = refs[1:]
    gpre_ref, gpost_ref, wg_ref, wu_ref, wd_ref, o_ref = refs
    xn = _rms(x, gpre_ref[...]).astype(BF16)
    acc = None
    for f in range(D_FF // FFN_CHUNK):
        cols = slice(f * FFN_CHUNK, (f + 1) * FFN_CHUNK)
        a = _dot(xn, wg_ref[:, cols])
        b = _dot(xn, wu_ref[:, cols])
        part = _dot((_silu(a) * b).astype(BF16), wd_ref[cols, :])
        acc = part if acc is None else acc + part
    o_ref[...] = x + 0.5 * _rms(acc, gpost_ref[...])


def _ffn_block(x, g_pre, g_post, wg, wu, wd, mix=None):
    m = x.shape[0]
    tm = min(m, 512)
    const = lambda a: pl.BlockSpec(a.shape, lambda i: (0, 0), pipeline_mode=pl.Buffered(1))
    row = lambda a: pl.BlockSpec((tm, a.shape[1]), lambda i: (i, 0))
    args, in_specs = [x], [row(x)]
    if mix is not None:
        o_h, o_m, w1, w2, g_mix = mix
        args += [o_h, o_m, w1, w2, g_mix]
        in_specs += [row(o_h), row(o_m), const(w1), const(w2), const(g_mix)]
    args += [g_pre, g_post, wg, wu, wd]
    in_specs += [const(g_pre), const(g_post), const(wg), const(wu), const(wd)]
    return pl.pallas_call(
        functools.partial(_ffn_kernel, mix_in=mix is not None),
        grid=(m // tm,),
        in_specs=in_specs,
        out_specs=row(x),
        out_shape=jax.ShapeDtypeStruct((m, D_MODEL), F32),
        compiler_params=_params("parallel"),
        name="ffn_mix_block" if mix is not None else "ffn_block",
    )(*args)


def _inproj_kernel(x_ref, g_ref, csk_ref, cosq_ref, sinq_ref, wh_ref, wcq_ref, wckv_ref, wkr_ref,
                   qg_ref, kvg_ref, wqn_ref, wqr_ref, wqrr_ref, wuk_ref,
                   zh_ref, ckv_ref, kr_ref, ckvb_ref, krb_ref, qlat_ref, qrope_ref):
    u = _rms(x_ref[...], g_ref[...]).astype(BF16)
    zh_ref[...] = _dot(u, wh_ref[...])
    ckv = _rms(_dot(u, wckv_ref[...]), kvg_ref[...])
    ckv_ref[...] = ckv
    ckvb_ref[...] = ckv.astype(BF16)
    kz = _dot(u, wkr_ref[...])
    csk = csk_ref[...]
    kr = kz[:, :QK_ROPE] * csk[:, :QK_ROPE] + kz[:, QK_ROPE:] * csk[:, QK_ROPE:]
    kr_ref[...] = kr
    krb_ref[...] = kr.astype(BF16)
    cqn = _rms(_dot(u, wcq_ref[...]), qg_ref[...]).astype(BF16)
    qn = _dot(cqn, wqn_ref[...]).astype(BF16)
    for h in range(MLA_HEADS):
        qlat_ref[:, h * KV_LORA:(h + 1) * KV_LORA] = _dot(
            qn[:, h * QK_NOPE:(h + 1) * QK_NOPE], wuk_ref[h]).astype(BF16)
    qr = _dot(cqn, wqr_ref[...]) * cosq_ref[...] + _dot(cqn, wqrr_ref[...]) * sinq_ref[...]
    qrope_ref[...] = qr.astype(BF16)


def _in_proj(x, g, csk, cosq, sinq, w):
    m = x.shape[0]
    tm = min(m, 512)
    nrep = csk.shape[0] // tm
    full = lambda a: pl.BlockSpec(a.shape, lambda i: (0,) * a.ndim, pipeline_mode=pl.Buffered(1))
    row = lambda width: pl.BlockSpec((tm, width), lambda i: (i, 0))
    tab = lambda width: pl.BlockSpec((tm, width), lambda i: (i % nrep, 0))
    rope_w = MLA_HEADS * QK_ROPE
    return pl.pallas_call(
        _inproj_kernel,
        grid=(m // tm,),
        in_specs=[row(D_MODEL), full(g), tab(2 * QK_ROPE), tab(rope_w), tab(rope_w),
                  full(w["w_h"]), full(w["w_cq"]), full(w["w_ckv"]), full(w["w_kr"]),
                  full(w["q_norm_g"]), full(w["kv_norm_g"]), full(w["w_qn"]), full(w["w_qr"]),
                  full(w["w_qrr"]), full(w["w_uk"])],
        out_specs=[row(4 * HG_W), row(KV_LORA), row(QK_ROPE), row(KV_LORA), row(QK_ROPE),
                   row(MLA_HEADS * KV_LORA), row(rope_w)],
        out_shape=[
            jax.ShapeDtypeStruct((m, 4 * HG_W), F32),
            jax.ShapeDtypeStruct((m, KV_LORA), F32),
            jax.ShapeDtypeStruct((m, QK_ROPE), F32),
            jax.ShapeDtypeStruct((m, KV_LORA), BF16),
            jax.ShapeDtypeStruct((m, QK_ROPE), BF16),
            jax.ShapeDtypeStruct((m, MLA_HEADS * KV_LORA), BF16),
            jax.ShapeDtypeStruct((m, rope_w), BF16),
        ],
        compiler_params=_params("parallel"),
        name="in_proj",
    )(x, g, csk, cosq, sinq, w["w_h"], w["w_cq"], w["w_ckv"], w["w_kr"], w["q_norm_g"],
      w["kv_norm_g"], w["w_qn"], w["w_qr"], w["w_qrr"], w["w_uk"])


def _split3(g):
    g1 = g.astype(BF16).astype(F32)
    r1 = g - g1
    g2 = r1.astype(BF16).astype(F32)
    g3 = (r1 - g2).astype(BF16).astype(F32)
    return g1, g2, g3


HG_SUB = 16


def _hgrn_kernel(*refs, chunk, t_valid, has_s0):
    if has_s0:
        hq_ref, hf_ref, hi_ref, hg_ref, lbl_ref, ng_ref, s0_ref = refs[:7]
        refs = refs[7:]
    else:
        hq_ref, hf_ref, hi_ref, hg_ref, lbl_ref, ng_ref = refs[:6]
        s0_ref = None
        refs = refs[6:]
    o_ref, sfin_ref, st_ref, c_ref, v_ref = refs
    ti = pl.program_id(1)
    t_tile = hq_ref.shape[0]
    n_chunks = t_tile // chunk
    n_tiles = chunk // SUBLANES

    lbl = lbl_ref[...]
    e = jnp.exp(lbl - jnp.max(lbl, axis=0, keepdims=True))
    lb_all = e[0:1, :] / jnp.sum(e, axis=0, keepdims=True)

    @pl.when(ti == 0)
    def _():
        for h in range(HG_HEADS):
            st_ref[h] = s0_ref[h].T if has_s0 else jnp.zeros((HG_DV, HG_DK), F32)

    r_i = lax.broadcasted_iota(jnp.int32, (chunk, chunk), 0)
    c_i = lax.broadcasted_iota(jnp.int32, (chunk, chunk), 1)
    tril = (c_i <= r_i).astype(F32)
    row8 = lax.broadcasted_iota(jnp.int32, (SUBLANES, HG_DK), 0)
    ng = ng_ref[...]
    ones = jnp.ones((HG_DK, HG_DV), BF16)
    sub = min(chunk, HG_SUB)
    n_sub = chunk // sub
    pairs = [(s, i) for s in range(chunk) for i in range(s // SUBLANES, (s // sub + 1) * sub // SUBLANES)]
    if n_sub > 1:
        assert sub & (sub - 1) == 0 and chunk & (chunk - 1) == 0
        rows_sub = lax.broadcasted_iota(jnp.int32, (chunk, (n_sub - 1) * chunk), 0) >> (sub.bit_length() - 1)
        cols_blk = lax.broadcasted_iota(jnp.int32, (chunk, (n_sub - 1) * chunk), 1) >> (chunk.bit_length() - 1)
        sub_mask = cols_blk == rows_sub - 1

    def gates(h, start):
        sl = pl.ds(start, chunk)
        hl = slice(h * HG_DK, (h + 1) * HG_DK)
        lb = lb_all[:, hl]
        q = _silu(hq_ref[sl, hl])
        f = lb + (1.0 - lb) * _sigmoid(hf_ref[sl, hl])
        g = jnp.log(f)
        k = 1.0 - f
        v = hi_ref[sl, hl]
        if t_valid is not None:
            pos = lax.broadcasted_iota(jnp.int32, (chunk, HG_DK), 0) + (ti * t_tile + start)
            valid = pos < t_valid
            g = jnp.where(valid, g, 0.0)
            k = jnp.where(valid, k, 0.0)
            v = jnp.where(valid, v, 0.0)
        g1, g2, g3 = _split3(g)
        b = _dot(tril, g1) + _dot(tril, g2) + _dot(tril, g3)
        return q, k, v, b * LOG2E

    def products(h, q, k, v, b2):
        c_ref[h] = b2 - jnp.log(k) * LOG2E
        v_ref[h] = v
        q_t = [q[i * SUBLANES:(i + 1) * SUBLANES, :] for i in range(n_tiles)]
        b_t = [b2[i * SUBLANES:(i + 1) * SUBLANES, :] for i in range(n_tiles)]
        p_tiles = []
        for s, i in pairs:
            d = b_t[i] - c_ref[h, s:s + 1, :]
            if i == s // SUBLANES and s % SUBLANES:
                d = jnp.where(row8 >= s % SUBLANES, d, NEG_BIG)
            p_tiles.append(q_t[i] * jnp.exp2(d))
        a_rep = _dot(jnp.concatenate(p_tiles, axis=0).astype(BF16), ones)

        a_sub = None
        if n_sub > 1:
            zeros_sub = jnp.zeros((sub, HG_DK), F32)
            q_parts, k_parts = [zeros_sub], []
            for j in range(1, n_sub):
                lo = j * sub
                r_j = b2[lo - 1:lo, :]
                q_parts.append(q[lo:lo + sub, :] * jnp.exp2(b2[lo:lo + sub, :] - r_j))
                k_parts += [k[:lo, :] * jnp.exp2(r_j - b2[:lo, :])] + [zeros_sub] * (n_sub - j)
            a_sub = _dot_nt(jnp.concatenate(q_parts, axis=0).astype(BF16),
                            jnp.concatenate(k_parts, axis=0).astype(BF16))

        st = st_ref[h]
        o_state = _dot_nt((q * jnp.exp2(b2)).astype(BF16), st.astype(BF16))
        bl = b2[chunk - 1:chunk, :]
        v_b = v.astype(BF16)
        st_ref[h] = st * jnp.exp2(bl) + _dot_tn(v_b, (k * jnp.exp2(bl - b2)).astype(BF16))
        return a_rep, a_sub, o_state, v_b

    def outputs(h, start, a_rep, a_sub, o_state, v_b):
        sl = pl.ds(start, chunk)
        hl = slice(h * HG_DK, (h + 1) * HG_DK)
        o_t = [o_state[i * SUBLANES:(i + 1) * SUBLANES, :] for i in range(n_tiles)]
        for j, (s, i) in enumerate(pairs):
            o_t[i] = o_t[i] + a_rep[j * SUBLANES:(j + 1) * SUBLANES, :] * v_ref[h, s:s + 1, :]
        o = jnp.concatenate(o_t, axis=0) if n_tiles > 1 else o_t[0]
        if a_sub is not None:
            o = o + _dot(jnp.where(sub_mask, a_sub, 0.0).astype(BF16),
                         jnp.concatenate([v_b] * (n_sub - 1), axis=0))
        o_ref[sl, hl] = (_rms(o, ng) * _silu(hg_ref[sl, hl])).astype(o_ref.dtype)

    def chunk_body(c, carry):
        start = pl.multiple_of(c * chunk, chunk)
        heads = range(HG_HEADS)
        gated = [gates(h, start) for h in heads]
        prods = [products(h, *gated[h]) for h in heads]
        for h in heads:
            outputs(h, start, *prods[h])
        return carry

    lax.fori_loop(0, n_chunks, chunk_body, 0)

    @pl.when(ti == pl.num_programs(1) - 1)
    def _():
        for h in range(HG_HEADS):
            sfin_ref[h] = st_ref[h].T


def _hgrn(zh, lb_logits, ng, s0, chunk, t_valid):
    bsz, t, _ = zh.shape
    has_s0 = s0 is not None
    t_tile = min(t, 512)
    assert t % t_tile == 0 and t_tile % chunk == 0
    col = lambda j: pl.BlockSpec((None, t_tile, HG_W), lambda b, i: (b, i, j))
    in_specs = [col(0), col(1), col(2), col(3),
                pl.BlockSpec(lb_logits.shape, lambda b, i: (0, 0)),
                pl.BlockSpec((1, HG_DV), lambda b, i: (0, 0))]
    args = [zh, zh, zh, zh, lb_logits, ng]
    st_spec = pl.BlockSpec((None, HG_HEADS, HG_DK, HG_DV), lambda b, i: (b, 0, 0, 0))
    if has_s0:
        in_specs.append(st_spec)
        args.append(s0)
    return pl.pallas_call(
        functools.partial(_hgrn_kernel, chunk=chunk, t_valid=t_valid, has_s0=has_s0),
        grid=(bsz, t // t_tile),
        in_specs=in_specs,
        out_specs=[pl.BlockSpec((None, t_tile, HG_W), lambda b, i: (b, i, 0)), st_spec],
        out_shape=[jax.ShapeDtypeStruct((bsz, t, HG_W), BF16),
                   jax.ShapeDtypeStruct((bsz, HG_HEADS, HG_DK, HG_DV), F32)],
        scratch_shapes=[pltpu.VMEM((HG_HEADS, HG_DV, HG_DK), F32),
                        pltpu.VMEM((HG_HEADS, chunk, HG_DK), F32),
                        pltpu.VMEM((HG_HEADS, chunk, HG_DV), F32)],
        compiler_params=_params("parallel", "arbitrary"),
        name="hgrn",
    )(*args)


ATTN_SCALE = 1.0 / math.sqrt(QK_NOPE + QK_ROPE)
ATTN_EXP2_SCALE = ATTN_SCALE * LOG2E

LANES = 128


def _attn_prompt_kernel(ql_ref, qr_ref, kc_ref, kr_ref, wuv_ref, o_ref,
                        m_ref, l_ref, a_ref, acc_ref, s_ref, p_ref, *, tq):
    qi = pl.program_id(1)
    ql = jnp.concatenate([ql_ref[:, h * KV_LORA:(h + 1) * KV_LORA] for h in range(MLA_HEADS)], axis=0)
    qr = jnp.concatenate([qr_ref[:, h * QK_ROPE:(h + 1) * QK_ROPE] for h in range(MLA_HEADS)], axis=0)
    m_ref[...] = jnp.full_like(m_ref, NEG_BIG)
    l_ref[...] = jnp.zeros_like(l_ref)
    a_ref[...] = jnp.zeros_like(a_ref)
    acc_ref[...] = jnp.zeros_like(acc_ref)
    p_ref[1] = jnp.zeros(p_ref.shape[1:], BF16)

    def keys(kb):
        return pl.ds(pl.multiple_of(kb * tq, tq), tq)

    def scores(kb):
        return (_dot_nt(ql, kc_ref[keys(kb), :]) + _dot_nt(qr, kr_ref[keys(kb), :])) * ATTN_EXP2_SCALE

    def add_values(kb, slot):
        acc_ref[...] = (jnp.tile(a_ref[...], (1, KV_LORA // LANES)) * acc_ref[...]
                        + _dot(p_ref[slot], kc_ref[keys(kb), :]))

    def softmax(slot, masked):
        s = s_ref[slot]
        if masked:
            rows = lax.broadcasted_iota(jnp.int32, s.shape, 0) & (tq - 1)
            cols = lax.broadcasted_iota(jnp.int32, s.shape, 1)
            s = jnp.where(cols <= rows, s, NEG_BIG)
        m_prev = m_ref[...]
        m_new = jnp.maximum(m_prev, jnp.max(s, axis=-1, keepdims=True))
        alpha = jnp.exp2(m_prev - m_new)
        p = jnp.exp2(s - jnp.tile(m_new, (1, tq // LANES)))
        l_ref[...] = alpha * l_ref[...] + jnp.sum(p, axis=-1, keepdims=True)
        m_ref[...] = m_new
        a_ref[...] = alpha
        p_ref[slot] = p.astype(BF16)

    s_ref[0] = scores(0)

    def iteration(i, cur):
        s_ref[1 - cur] = scores(i + 1)
        add_values(jnp.maximum(i - 1, 0), 1 - cur)
        softmax(cur, False)

    def last(cur):
        add_values(jnp.maximum(qi - 1, 0), 1 - cur)
        softmax(cur, True)
        add_values(qi, cur)

    def pair(j, carry):
        iteration(2 * j, 0)
        iteration(2 * j + 1, 1)
        return carry

    lax.fori_loop(0, qi >> 1, pair, 0)

    @pl.when((qi & 1) == 1)
    def _():
        iteration(qi - 1, 0)
        last(1)

    @pl.when((qi & 1) == 0)
    def _():
        last(0)

    o_lat =(acc_ref[...] / jnp.tile(l_ref[...], (1, KV_LORA // LANES))).astype(BF16)
    for h in range(MLA_HEADS):
        o_ref[:, h * V_DIM:(h + 1) * V_DIM] = _dot(o_lat[h * tq:(h + 1) * tq, :], wuv_ref[h]).astype(o_ref.dtype)


def _attn_prompt(qlat, qrope, ckv_b, kr_b, w_uv, bsz, t):
    tq = 256
    assert tq & (tq - 1) == 0 and t % tq == 0
    nq = t // tq
    rows = MLA_HEADS * tq
    return pl.pallas_call(
        functools.partial(_attn_prompt_kernel, tq=tq),
        grid=(bsz, nq),
        in_specs=[
            pl.BlockSpec((tq, MLA_HEADS * KV_LORA), lambda b, i: (b * nq + i, 0)),
            pl.BlockSpec((tq, MLA_HEADS * QK_ROPE), lambda b, i: (b * nq + i, 0)),
            pl.BlockSpec((t, KV_LORA), lambda b, i: (b, 0)),
            pl.BlockSpec((t, QK_ROPE), lambda b, i: (b, 0)),
            pl.BlockSpec(w_uv.shape, lambda b, i: (0, 0, 0)),
        ],
        out_specs=pl.BlockSpec((tq, MLA_HEADS * V_DIM), lambda b, i: (b * nq + i, 0)),
        out_shape=jax.ShapeDtypeStruct((bsz * t, MLA_HEADS * V_DIM), BF16),
        scratch_shapes=[pltpu.VMEM((rows, LANES), F32), pltpu.VMEM((rows, LANES), F32),
                        pltpu.VMEM((rows, LANES), F32), pltpu.VMEM((rows, KV_LORA), F32),
                        pltpu.VMEM((2, rows, tq), F32), pltpu.VMEM((2, rows, tq), BF16)],
        compiler_params=_params("parallel", "parallel"),
        name="attn_prompt",
    )(qlat, qrope, ckv_b, kr_b, w_uv)


PAGES_PER_STEP = 16
NEW_KEY_ROWS = 128


def _attn_sample_kernel(pt_ref, ql_ref, qr_ref, nc_ref, nr_ref, wuv_ref, cc_hbm, cr_hbm, o_ref,
                        cbuf, rbuf, sem, wq_ref, wr_ref, m_ref, l_ref, acc_ref, kv_ref, kq_ref, kr_ref,
                        *, t_new, n_q, n_steps):
    b = pl.program_id(0)
    nb = pl.num_programs(0)
    groups = LANES // n_q
    page = cbuf.shape[1] // PAGES_PER_STEP
    ppg = PAGES_PER_STEP // groups
    lat_rep = KV_LORA // LANES

    def page_copies(bb, step, slot):
        copies = []
        for i in range(PAGES_PER_STEP):
            pid = pt_ref[bb, step * PAGES_PER_STEP + i]
            copies.append(pltpu.make_async_copy(cc_hbm.at[pid], cbuf.at[slot, pl.ds(i * page, page), :],
                                                sem.at[slot]))
            copies.append(pltpu.make_async_copy(cr_hbm.at[pid], rbuf.at[slot, i], sem.at[slot]))
        return copies

    @pl.when(b == 0)
    def _():
        for step in range(2):
            for c in page_copies(0, step, step):
                c.start()

    q_shift = n_q.bit_length() - 1
    spread = (lax.broadcasted_iota(jnp.int32, (n_q, LANES), 1) & (n_q - 1)) == lax.broadcasted_iota(
        jnp.int32, (n_q, LANES), 0)
    spread = jnp.where(spread, 1.0, 0.0).astype(BF16)
    for src, dst in ((ql_ref, wq_ref), (qr_ref, wr_ref)):
        feat = src.shape[0]
        tiled = _dot(src[...], spread)
        lane_g = lax.broadcasted_iota(jnp.int32, tiled.shape, 1) >> q_shift
        for g in range(groups):
            dst[g * feat:(g + 1) * feat, :] = jnp.where(lane_g == g, tiled, 0.0).astype(BF16)

    m_ref[...] = jnp.full_like(m_ref, NEG_BIG)
    l_ref[...] = jnp.zeros_like(l_ref)
    acc_ref[...] = jnp.zeros_like(acc_ref)

    def lanes_to_rows(row):
        return jnp.broadcast_to(row, (LANES, LANES)).T

    def update(s, place, values):
        m_prev = m_ref[...]
        m_new = jnp.maximum(m_prev, jnp.max(s, axis=0, keepdims=True))
        alpha = jnp.exp2(m_prev - m_new)
        p = jnp.exp2(s - m_new)
        l_ref[...] = alpha * l_ref[...] + jnp.sum(p, axis=0, keepdims=True)
        m_ref[...] = m_new
        pv = _dot(place(p.T), values)
        acc_ref[...] = jnp.tile(lanes_to_rows(alpha), (1, lat_rep)) * acc_ref[...] + pv

    def block_diag(p_t):
        row_g = lax.broadcasted_iota(jnp.int32, p_t.shape, 0) >> q_shift
        return jnp.concatenate([jnp.where(row_g == g, p_t, 0.0).astype(BF16) for g in range(groups)], axis=1)

    def stage_and_score(j):
        slot = j & 1
        for c in page_copies(b, j, slot):
            c.wait()
        for i in range(PAGES_PER_STEP):
            g, r = divmod(i, ppg)
            x = cbuf[slot, i * page:(i + 1) * page, :].astype(BF16)
            kv_ref[slot, i * page:(i + 1) * page, :] = x
            kq_ref[slot, r * page:(r + 1) * page, g * KV_LORA:(g + 1) * KV_LORA] = x
            kr_ref[slot, g * QK_ROPE:(g + 1) * QK_ROPE, r * page:(r + 1) * page] = rbuf[slot, i].astype(BF16)
        if j + 2 < n_steps:
            nxt = page_copies(b, j + 2, slot)
        else:
            nxt = page_copies(jnp.minimum(b + 1, nb - 1), j + 2 - n_steps, slot)
        for c in nxt:
            c.start()
        return (_dot(kq_ref[slot], wq_ref[...]) + _dot_tn(kr_ref[slot], wr_ref[...])) * ATTN_EXP2_SCALE

    s_cur = stage_and_score(0)
    for j in range(n_steps):
        s_next = stage_and_score(j + 1) if j + 1 < n_steps else None
        update(s_cur, block_diag, kv_ref[j & 1])
        s_cur = s_next

    @pl.when(b == nb - 1)
    def _():
        for step in range(2):
            for c in page_copies(b, step, step):
                c.wait()

    nc = nc_ref[...]
    s2 = (_dot(nc, wq_ref[:KV_LORA, :]) + _dot(nr_ref[...], wr_ref[:QK_ROPE, :])) * ATTN_EXP2_SCALE
    t_k = lax.broadcasted_iota(jnp.int32, s2.shape, 0)
    lane = lax.broadcasted_iota(jnp.int32, s2.shape, 1)
    ok = (lane < n_q) & (t_k <= (lane & (t_new - 1)))
    update(jnp.where(ok, s2, NEG_BIG), lambda p_t: p_t.astype(BF16), nc)

    grp = lambda a, g: a[g * n_q:(g + 1) * n_q]
    total = lambda a: functools.reduce(lambda x, y: x + y, [grp(a, g) for g in range(groups)])
    m_t = lanes_to_rows(m_ref[...])
    l_t = lanes_to_rows(l_ref[...])
    m_q = functools.reduce(jnp.maximum, [grp(m_t, g) for g in range(groups)])
    w = jnp.exp2(m_t - jnp.tile(m_q, (groups, 1)))
    l_q = total(w * l_t)
    o_lat = total(jnp.tile(w, (1, lat_rep)) * acc_ref[...]) / jnp.tile(l_q, (1, lat_rep))
    o_lat = o_lat.astype(BF16)
    head = lax.broadcasted_iota(jnp.int32, (n_q, V_DIM), 0) >> (t_new.bit_length() - 1)
    out = jnp.zeros((n_q, V_DIM), F32)
    for h in range(MLA_HEADS):
        out = out + jnp.where(head == h, _dot(o_lat, wuv_ref[h]), 0.0)
    o_ref[...] = out.astype(o_ref.dtype)


def _attn_sample(ql_t, qr_t, new_c, new_r, w_uv, cache_c, cache_r_t, page_table, t_new):
    bsz, _, n_q = ql_t.shape
    n_pages = page_table.shape[1]
    groups = LANES // n_q
    assert LANES % n_q == 0 and PAGES_PER_STEP % groups == 0 and n_pages % PAGES_PER_STEP == 0
    assert t_new & (t_new - 1) == 0 and n_q & (n_q - 1) == 0 and t_new <= NEW_KEY_ROWS and n_q >= SUBLANES
    page = cache_c.shape[1]
    keys = PAGES_PER_STEP * page
    n_steps = n_pages // PAGES_PER_STEP
    assert n_steps % 2 == 0

    per_b = lambda a: pl.BlockSpec((None,) + a.shape[1:], lambda b, pt: (b, 0, 0))
    grid_spec = pltpu.PrefetchScalarGridSpec(
        num_scalar_prefetch=1,
        grid=(bsz,),
        in_specs=[per_b(ql_t), per_b(qr_t), per_b(new_c), per_b(new_r),
                  pl.BlockSpec(w_uv.shape, lambda b, pt: (0, 0, 0)),
                  pl.BlockSpec(memory_space=pl.ANY), pl.BlockSpec(memory_space=pl.ANY)],
        out_specs=pl.BlockSpec((None, n_q, V_DIM), lambda b, pt: (b, 0, 0)),
        scratch_shapes=[pltpu.VMEM((2, keys, KV_LORA), cache_c.dtype),
                        pltpu.VMEM((2, PAGES_PER_STEP, QK_ROPE, page), cache_r_t.dtype),
                        pltpu.SemaphoreType.DMA((2,)),
                        pltpu.VMEM((groups * KV_LORA, LANES), BF16),
                        pltpu.VMEM((groups * QK_ROPE, LANES), BF16),
                        pltpu.VMEM((1, LANES), F32), pltpu.VMEM((1, LANES), F32),
                        pltpu.VMEM((LANES, KV_LORA), F32),
                        pltpu.VMEM((2, keys, KV_LORA), BF16),
                        pltpu.VMEM((2, keys // groups, groups * KV_LORA), BF16),
                        pltpu.VMEM((2, groups * QK_ROPE, keys // groups), BF16)],
    )
    return pl.pallas_call(
        functools.partial(_attn_sample_kernel, t_new=t_new, n_q=n_q, n_steps=n_steps),
        grid_spec=grid_spec,
        out_shape=jax.ShapeDtypeStruct((bsz, n_q, V_DIM), BF16),
        compiler_params=_params("arbitrary"),
        name="attn_sample",
    )(page_table, ql_t, qr_t, new_c, new_r, w_uv, cache_c, cache_r_t)


def _rot_cols(w):
    half = w.shape[-1] // 2
    return jnp.concatenate([-w[..., half:], w[..., :half]], axis=-1)


def _prep_weights(norm_g, w_ffn_gate, w_ffn_up, w_ffn_down, w_in, q_norm_g, w_q_up, kv_norm_g,
                  w_kv_up, hg_norm_g, w_out):
    bf = lambda a: a.astype(BF16)
    w_h, w_cq, w_ckv, w_kr = jnp.split(w_in, [4 * HG_W, 4 * HG_W + Q_LORA, 4 * HG_W + Q_LORA + KV_LORA], axis=-1)
    w_q_rope = w_q_up[..., QK_NOPE:]
    return {
        "norm_g": norm_g.reshape(-1, 1, D_MODEL),
        "wg": bf(w_ffn_gate), "wu": bf(w_ffn_up), "wd": bf(w_ffn_down),
        "w_h": bf(w_h), "w_cq": bf(w_cq), "w_ckv": bf(w_ckv),
        "w_kr": bf(jnp.concatenate([w_kr, _rot_cols(w_kr)], axis=-1)),
        "q_norm_g": q_norm_g.reshape(1, Q_LORA), "kv_norm_g": kv_norm_g.reshape(1, KV_LORA),
        "w_qn": bf(w_q_up[..., :QK_NOPE].reshape(Q_LORA, MLA_HEADS * QK_NOPE)),
        "w_qr": bf(w_q_rope.reshape(Q_LORA, MLA_HEADS * QK_ROPE)),
        "w_qrr": bf(_rot_cols(w_q_rope).reshape(Q_LORA, MLA_HEADS * QK_ROPE)),
        "w_uk": bf(jnp.transpose(w_kv_up[..., :QK_NOPE], (1, 2, 0))),
        "w_uv": bf(jnp.transpose(w_kv_up[..., QK_NOPE:], (1, 0, 2))),
        "hg_norm_g": hg_norm_g.reshape(1, HG_DV),
        "w_out1": bf(w_out[:HG_W]), "w_out2": bf(w_out[HG_W:]),
    }


def _rope_tables(pos):
    half = QK_ROPE // 2
    inv = ROPE_THETA ** (-jnp.arange(half, dtype=F32) / half)
    ang = pos.astype(F32)[:, None] * inv[None, :]
    cos = jnp.tile(jnp.cos(ang), (1, 2))
    sin = jnp.tile(jnp.sin(ang), (1, 2))
    return (jnp.concatenate([cos, sin], axis=-1), jnp.tile(cos, (1, MLA_HEADS)), jnp.tile(sin, (1, MLA_HEADS)))


def kernel(x_prompt, x_sample, cache_kv_latent, cache_k_rope, state_hgrn, page_table, hgrn_lb_logits,
           norm_g, w_ffn_gate, w_ffn_up, w_ffn_down, w_in, q_norm_g, w_q_up, kv_norm_g, w_kv_up,
           hg_norm_g, w_out):
    bp, tp, _ = x_prompt.shape
    bs, ts, _ = x_sample.shape
    depth = norm_g.shape[0]
    assert depth == 1
    past_len = page_table.shape[1] * cache_kv_latent.shape[2]
    w = _prep_weights(norm_g[0], w_ffn_gate[0], w_ffn_up[0], w_ffn_down[0], w_in[0], q_norm_g[0],
                      w_q_up[0], kv_norm_g[0], w_kv_up[0], hg_norm_g[0], w_out[0])
    ng = w["norm_g"]
    lb_logits = hgrn_lb_logits.astype(F32)

    def pre(x, tables):
        h1 = _ffn_block(x, ng[0], ng[1], w["wg"][0], w["wu"][0], w["wd"][0])
        return (h1,) + tuple(_in_proj(h1, ng[2], *tables, w))

    def post(h1, o_h, o_m):
        return _ffn_block(h1, ng[4], ng[5], w["wg"][1], w["wu"][1], w["wd"][1],
                          mix=(o_h, o_m, w["w_out1"], w["w_out2"], ng[3]))

    h1, zh, ckv, kr, ckv_b, kr_b, qlat, qrope = pre(
        x_prompt.reshape(bp * tp, D_MODEL), _rope_tables(jnp.arange(tp, dtype=jnp.int32)))
    o_h, st_p = _hgrn(zh.reshape(bp, tp, 4 * HG_W), lb_logits, w["hg_norm_g"], None, chunk=64, t_valid=None)
    o_m = _attn_prompt(qlat, qrope, ckv_b, kr_b, w["w_uv"], bp, tp)
    y_p = post(h1, o_h.reshape(bp * tp, HG_W), o_m).reshape(bp, tp, D_MODEL)
    ckv_p = ckv.reshape(1, bp, tp, KV_LORA)
    kr_p = kr.reshape(1, bp, tp, QK_ROPE)

    pos_s = past_len + jnp.arange(ts, dtype=jnp.int32)
    tabs = tuple(jnp.tile(a, (bs, 1)) for a in _rope_tables(pos_s))
    h1s, zhs, ckvs, krs, ckvs_b, krs_b, qlats, qropes = pre(x_sample.reshape(bs * ts, D_MODEL), tabs)
    t_pad = -(-ts // SUBLANES) * SUBLANES
    zhs_pad = jnp.pad(zhs.reshape(bs, ts, 4 * HG_W), ((0, 0), (0, t_pad - ts), (0, 0)))
    o_hs, st_s = _hgrn(zhs_pad, lb_logits, w["hg_norm_g"], state_hgrn[0], chunk=SUBLANES,
                       t_valid=ts if t_pad != ts else None)
    q_cols = lambda a, wd: a.reshape(bs, ts, MLA_HEADS, wd).transpose(0, 3, 2, 1).reshape(bs, wd, MLA_HEADS * ts)
    pad_new = lambda a, wd: jnp.pad(a.reshape(bs, ts, wd), ((0, 0), (0, NEW_KEY_ROWS - ts), (0, 0)))
    o_ms = _attn_sample(q_cols(qlats, KV_LORA), q_cols(qropes, QK_ROPE),
                        pad_new(ckvs_b, KV_LORA), pad_new(krs_b, QK_ROPE), w["w_uv"],
                        cache_kv_latent[0], jnp.swapaxes(cache_k_rope[0], 1, 2), page_table, ts)
    o_ms = o_ms.reshape(bs, MLA_HEADS, ts, V_DIM).transpose(0, 2, 1, 3).reshape(bs * ts, MLA_HEADS * V_DIM)
    y_s = post(h1s, o_hs[:, :ts].reshape(bs * ts, HG_W), o_ms).reshape(bs, ts, D_MODEL)

    return (y_p, y_s, ckv_p, kr_p, st_p[None], ckvs.reshape(1, bs, ts, KV_LORA),
            krs.reshape(1, bs, ts, QK_ROPE), st_s[None])
```

```python
import functools
import math

import jax
import jax.numpy as jnp
from jax import lax
from jax.experimental import pallas as pl
from jax.experimental.pallas import tpu as pltpu

D_MODEL = 1024
D_FF = 2816
HG_HEADS = 4
HG_DK = 128
HG_DV = 128
MLA_HEADS = 4
QK_NOPE = 128
QK_ROPE = 64
V_DIM = 128
Q_LORA = 768
KV_LORA = 256
ROPE_THETA = 10000.0
EPS = 1e-6
HG_W = HG_HEADS * HG_DK

VMEM_LIMIT_BYTES = 56 * 1024 * 1024
SUBLANES = 8
NEG_BIG = -1e30
LOG2E = math.log2(math.e)

F32 = jnp.float32
BF16 = jnp.bfloat16


def _dot(a, b):
    return jnp.dot(a, b, preferred_element_type=F32)


def _dot_nt(a, b):
    return lax.dot_general(a, b, (((1,), (1,)), ((), ())), preferred_element_type=F32)


def _dot_tn(a, b):
    return lax.dot_general(a, b, (((0,), (0,)), ((), ())), preferred_element_type=F32)


def _rms(x, g):
    return x * lax.rsqrt(jnp.mean(x * x, axis=-1, keepdims=True) + EPS) * g


def _sigmoid(x):
    return 1.0 / (1.0 + jnp.exp(-x))


def _silu(x):
    return x * _sigmoid(x)


def _params(*sem):
    return pltpu.CompilerParams(dimension_semantics=sem, vmem_limit_bytes=VMEM_LIMIT_BYTES)


FFN_CHUNK = 256


def _ffn_kernel(*refs, mix_in):
    if mix_in:
        h_ref, oh_ref, om_ref, w1_ref, w2_ref, gmix_ref = refs[:6]
        refs = refs[6:]
        mixed = _dot(oh_ref[...], w1_ref[...]) + _dot(om_ref[...], w2_ref[...])
        x = h_ref[...] + _rms(mixed, gmix_ref[...])
    else:
        x = refs[0][...]
        refs = refs[1:]
    gpre_ref, gpost_ref, wg_ref, wu_ref, wd_ref, o_ref = refs
    xn = _rms(x, gpre_ref[...]).astype(BF16)
    acc = None
    for f in range(D_FF // FFN_CHUNK):
        cols = slice(f * FFN_CHUNK, (f + 1) * FFN_CHUNK)
        a = _dot(xn, wg_ref[:, cols])
        b = _dot(xn, wu_ref[:, cols])
        part = _dot((_silu(a) * b).astype(BF16), wd_ref[cols, :])
        acc = part if acc is None else acc + part
    o_ref[...] = x + 0.5 * _rms(acc, gpost_ref[...])


def _ffn_block(x, g_pre, g_post, wg, wu, wd, which, mix=None):
    m = x.shape[0]
    tm = min(m, 512)

    def const(a):
        if a.ndim == 2:
            return pl.BlockSpec(a.shape, lambda i: (0, 0), pipeline_mode=pl.Buffered(1))
        return pl.BlockSpec((None,) + a.shape[1:], lambda i: (which, 0, 0), pipeline_mode=pl.Buffered(1))

    row = lambda a: pl.BlockSpec((tm, a.shape[1]), lambda i: (i, 0))
    args, in_specs = [x], [row(x)]
    if mix is not None:
        o_h, o_m, w1, w2, g_mix = mix
        args += [o_h, o_m, w1, w2, g_mix]
        in_specs += [row(o_h), row(o_m), const(w1), const(w2), const(g_mix)]
    args += [g_pre, g_post, wg, wu, wd]
    in_specs += [const(g_pre), const(g_post), const(wg), const(wu), const(wd)]
    return pl.pallas_call(
        functools.partial(_ffn_kernel, mix_in=mix is not None),
        grid=(m // tm,),
        in_specs=in_specs,
        out_specs=row(x),
        out_shape=jax.ShapeDtypeStruct((m, D_MODEL), F32),
        compiler_params=_params("parallel"),
        name="ffn_mix_block" if mix is not None else "ffn_block",
    )(*args)


def _inproj_kernel(x_ref, g_ref, csk_ref, cosq_ref, sinq_ref, wh_ref, wcq_ref, wckv_ref, wkr_ref,
                   qg_ref, kvg_ref, wqn_ref, wqr_ref, wqrr_ref, wuk_ref,
                   zh_ref, ckv_ref, kr_ref, ckvb_ref, krb_ref, qlat_ref, qrope_ref):
    u = _rms(x_ref[...], g_ref[...]).astype(BF16)
    zh_ref[...] = _dot(u, wh_ref[...])
    ckv = _rms(_dot(u, wckv_ref[...]), kvg_ref[...])
    ckv_ref[...] = ckv
    ckvb_ref[...] = ckv.astype(BF16)
    kz = _dot(u, wkr_ref[...])
    csk = csk_ref[...]
    kr = kz[:, :QK_ROPE] * csk[:, :QK_ROPE] + kz[:, QK_ROPE:] * csk[:, QK_ROPE:]
    kr_ref[...] = kr
    krb_ref[...] = kr.astype(BF16)
    cqn = _rms(_dot(u, wcq_ref[...]), qg_ref[...]).astype(BF16)
    qn = _dot(cqn, wqn_ref[...]).astype(BF16)
    for h in range(MLA_HEADS):
        qlat_ref[:, h * KV_LORA:(h + 1) * KV_LORA] = _dot(
            qn[:, h * QK_NOPE:(h + 1) * QK_NOPE], wuk_ref[h]).astype(BF16)
    qr = _dot(cqn, wqr_ref[...]) * cosq_ref[...] + _dot(cqn, wqrr_ref[...]) * sinq_ref[...]
    qrope_ref[...] = qr.astype(BF16)


def _in_proj(x, g, csk, cosq, sinq, w):
    m = x.shape[0]
    tm = min(m, 512)
    nrep = csk.shape[0] // tm
    full = lambda a: pl.BlockSpec(a.shape, lambda i: (0,) * a.ndim, pipeline_mode=pl.Buffered(1))
    row = lambda width: pl.BlockSpec((tm, width), lambda i: (i, 0))
    tab = lambda width: pl.BlockSpec((tm, width), lambda i: (i % nrep, 0))
    rope_w = MLA_HEADS * QK_ROPE
    return pl.pallas_call(
        _inproj_kernel,
        grid=(m // tm,),
        in_specs=[row(D_MODEL), full(g), tab(2 * QK_ROPE), tab(rope_w), tab(rope_w),
                  full(w["w_h"]), full(w["w_cq"]), full(w["w_ckv"]), full(w["w_kr"]),
                  full(w["q_norm_g"]), full(w["kv_norm_g"]), full(w["w_qn"]), full(w["w_qr"]),
                  full(w["w_qrr"]), full(w["w_uk"])],
        out_specs=[row(4 * HG_W), row(KV_LORA), row(QK_ROPE), row(KV_LORA), row(QK_ROPE),
                   row(MLA_HEADS * KV_LORA), row(rope_w)],
        out_shape=[
            jax.ShapeDtypeStruct((m, 4 * HG_W), F32),
            jax.ShapeDtypeStruct((m, KV_LORA), F32),
            jax.ShapeDtypeStruct((m, QK_ROPE), F32),
            jax.ShapeDtypeStruct((m, KV_LORA), BF16),
            jax.ShapeDtypeStruct((m, QK_ROPE), BF16),
            jax.ShapeDtypeStruct((m, MLA_HEADS * KV_LORA), BF16),
            jax.ShapeDtypeStruct((m, rope_w), BF16),
        ],
        compiler_params=_params("parallel"),
        name="in_proj",
    )(x, g, csk, cosq, sinq, w["w_h"], w["w_cq"], w["w_ckv"], w["w_kr"], w["q_norm_g"],
      w["kv_norm_g"], w["w_qn"], w["w_qr"], w["w_qrr"], w["w_uk"])


def _split3(g):
    g1 = g.astype(BF16).astype(F32)
    r1 = g - g1
    g2 = r1.astype(BF16).astype(F32)
    g3 = (r1 - g2).astype(BF16).astype(F32)
    return g1, g2, g3


HG_SUB = 16


def _hgrn_kernel(*refs, chunk, t_valid, has_s0):
    if has_s0:
        hq_ref, hf_ref, hi_ref, hg_ref, lbl_ref, ng_ref, s0_ref = refs[:7]
        refs = refs[7:]
    else:
        hq_ref, hf_ref, hi_ref, hg_ref, lbl_ref, ng_ref = refs[:6]
        s0_ref = None
        refs = refs[6:]
    o_ref, sfin_ref, st_ref, c_ref, v_ref = refs
    ti = pl.program_id(1)
    t_tile = hq_ref.shape[0]
    n_chunks = t_tile // chunk
    n_tiles = chunk // SUBLANES

    lbl = lbl_ref[...]
    e = jnp.exp(lbl - jnp.max(lbl, axis=0, keepdims=True))
    lb_all = e[0:1, :] / jnp.sum(e, axis=0, keepdims=True)

    @pl.when(ti == 0)
    def _():
        for h in range(HG_HEADS):
            st_ref[h] = s0_ref[h].T if has_s0 else jnp.zeros((HG_DV, HG_DK), F32)

    r_i = lax.broadcasted_iota(jnp.int32, (chunk, chunk), 0)
    c_i = lax.broadcasted_iota(jnp.int32, (chunk, chunk), 1)
    tril = (c_i <= r_i).astype(F32)
    row8 = lax.broadcasted_iota(jnp.int32, (SUBLANES, HG_DK), 0)
    ng = ng_ref[...]
    ones = jnp.ones((HG_DK, HG_DV), BF16)
    sub = min(chunk, HG_SUB)
    n_sub = chunk // sub
    pairs = [(s, i) for s in range(chunk) for i in range(s // SUBLANES, (s // sub + 1) * sub // SUBLANES)]
    if n_sub > 1:
        assert sub & (sub - 1) == 0 and chunk & (chunk - 1) == 0
        rows_sub = lax.broadcasted_iota(jnp.int32, (chunk, (n_sub - 1) * chunk), 0) >> (sub.bit_length() - 1)
        cols_blk = lax.broadcasted_iota(jnp.int32, (chunk, (n_sub - 1) * chunk), 1) >> (chunk.bit_length() - 1)
        sub_mask = cols_blk == rows_sub - 1

    def gates(h, start):
        sl = pl.ds(start, chunk)
        hl = slice(h * HG_DK, (h + 1) * HG_DK)
        lb = lb_all[:, hl]
        q = _silu(hq_ref[sl, hl])
        f = lb + (1.0 - lb) * _sigmoid(hf_ref[sl, hl])
        g = jnp.log(f)
        k = 1.0 - f
        v = hi_ref[sl, hl]
        if t_valid is not None:
            pos = lax.broadcasted_iota(jnp.int32, (chunk, HG_DK), 0) + (ti * t_tile + start)
            valid = pos < t_valid
            g = jnp.where(valid, g, 0.0)
            k = jnp.where(valid, k, 0.0)
            v = jnp.where(valid, v, 0.0)
        g1, g2, g3 = _split3(g)
        b = _dot(tril, g1) + _dot(tril, g2) + _dot(tril, g3)
        return q, k, v, b * LOG2E

    def products(h, q, k, v, b2):
        c_ref[h] = b2 - jnp.log(k) * LOG2E
        v_ref[h] = v
        q_t = [q[i * SUBLANES:(i + 1) * SUBLANES, :] for i in range(n_tiles)]
        b_t = [b2[i * SUBLANES:(i + 1) * SUBLANES, :] for i in range(n_tiles)]
        p_tiles = []
        for s, i in pairs:
            d = b_t[i] - c_ref[h, s:s + 1, :]
            if i == s // SUBLANES and s % SUBLANES:
                d = jnp.where(row8 >= s % SUBLANES, d, NEG_BIG)
            p_tiles.append(q_t[i] * jnp.exp2(d))
        a_rep = _dot(jnp.concatenate(p_tiles, axis=0).astype(BF16), ones)

        a_sub = None
        if n_sub > 1:
            zeros_sub = jnp.zeros((sub, HG_DK), F32)
            q_parts, k_parts = [zeros_sub], []
            for j in range(1, n_sub):
                lo = j * sub
                r_j = b2[lo - 1:lo, :]
                q_parts.append(q[lo:lo + sub, :] * jnp.exp2(b2[lo:lo + sub, :] - r_j))
                k_parts += [k[:lo, :] * jnp.exp2(r_j - b2[:lo, :])] + [zeros_sub] * (n_sub - j)
            a_sub = _dot_nt(jnp.concatenate(q_parts, axis=0).astype(BF16),
                            jnp.concatenate(k_parts, axis=0).astype(BF16))

        st = st_ref[h]
        o_state = _dot_nt((q * jnp.exp2(b2)).astype(BF16), st.astype(BF16))
        bl = b2[chunk - 1:chunk, :]
        v_b = v.astype(BF16)
        st_ref[h] = st * jnp.exp2(bl) + _dot_tn(v_b, (k * jnp.exp2(bl - b2)).astype(BF16))
        return a_rep, a_sub, o_state, v_b

    def outputs(h, start, a_rep, a_sub, o_state, v_b):
        sl = pl.ds(start, chunk)
        hl = slice(h * HG_DK, (h + 1) * HG_DK)
        o_t = [o_state[i * SUBLANES:(i + 1) * SUBLANES, :] for i in range(n_tiles)]
        for j, (s, i) in enumerate(pairs):
            o_t[i] = o_t[i] + a_rep[j * SUBLANES:(j + 1) * SUBLANES, :] * v_ref[h, s:s + 1, :]
        o = jnp.concatenate(o_t, axis=0) if n_tiles > 1 else o_t[0]
        if a_sub is not None:
            o = o + _dot(jnp.where(sub_mask, a_sub, 0.0).astype(BF16),
                         jnp.concatenate([v_b] * (n_sub - 1), axis=0))
        o_ref[sl, hl] = (_rms(o, ng) * _silu(hg_ref[sl, hl])).astype(o_ref.dtype)

    def chunk_body(c, carry):
        start = pl.multiple_of(c * chunk, chunk)
        heads = range(HG_HEADS)
        gated = [gates(h, start) for h in heads]
        prods = [products(h, *gated[h]) for h in heads]
        for h in heads:
            outputs(h, start, *prods[h])
        return carry

    lax.fori_loop(0, n_chunks, chunk_body, 0)

    @pl.when(ti == pl.num_programs(1) - 1)
    def _():
        for h in range(HG_HEADS):
            sfin_ref[h] = st_ref[h].T


def _hgrn(zh, lb_logits, ng, s0, chunk, t_valid):
    bsz, t, _ = zh.shape
    has_s0 = s0 is not None
    t_tile = min(t, 512)
    assert t % t_tile == 0 and t_tile % chunk == 0
    col = lambda j: pl.BlockSpec((None, t_tile, HG_W), lambda b, i: (b, i, j))
    in_specs = [col(0), col(1), col(2), col(3),
                pl.BlockSpec(lb_logits.shape, lambda b, i: (0, 0)),
                pl.BlockSpec((1, HG_DV), lambda b, i: (0, 0))]
    args = [zh, zh, zh, zh, lb_logits, ng]
    st_spec = pl.BlockSpec((None, HG_HEADS, HG_DK, HG_DV), lambda b, i: (b, 0, 0, 0))
    if has_s0:
        in_specs.append(st_spec)
        args.append(s0)
    return pl.pallas_call(
        functools.partial(_hgrn_kernel, chunk=chunk, t_valid=t_valid, has_s0=has_s0),
        grid=(bsz, t // t_tile),
        in_specs=in_specs,
        out_specs=[pl.BlockSpec((None, t_tile, HG_W), lambda b, i: (b, i, 0)), st_spec],
        out_shape=[jax.ShapeDtypeStruct((bsz, t, HG_W), BF16),
                   jax.ShapeDtypeStruct((bsz, HG_HEADS, HG_DK, HG_DV), F32)],
        scratch_shapes=[pltpu.VMEM((HG_HEADS, HG_DV, HG_DK), F32),
                        pltpu.VMEM((HG_HEADS, chunk, HG_DK), F32),
                        pltpu.VMEM((HG_HEADS, chunk, HG_DV), F32)],
        compiler_params=_params("parallel", "arbitrary"),
        name="hgrn",
    )(*args)


ATTN_SCALE = 1.0 / math.sqrt(QK_NOPE + QK_ROPE)
ATTN_EXP2_SCALE = ATTN_SCALE * LOG2E

LANES = 128


def _attn_prompt_kernel(ql_ref, qr_ref, kc_ref, kr_ref, wuv_ref, o_ref,
                        m_ref, l_ref, a_ref, acc_ref, s_ref, p_ref, *, tq):
    qi = pl.program_id(1)
    ql = jnp.concatenate([ql_ref[:, h * KV_LORA:(h + 1) * KV_LORA] for h in range(MLA_HEADS)], axis=0)
    qr = jnp.concatenate([qr_ref[:, h * QK_ROPE:(h + 1) * QK_ROPE] for h in range(MLA_HEADS)], axis=0)
    m_ref[...] = jnp.full_like(m_ref, NEG_BIG)
    l_ref[...] = jnp.zeros_like(l_ref)
    a_ref[...] = jnp.zeros_like(a_ref)
    acc_ref[...] = jnp.zeros_like(acc_ref)
    p_ref[1] = jnp.zeros(p_ref.shape[1:], BF16)

    def keys(kb):
        return pl.ds(pl.multiple_of(kb * tq, tq), tq)

    def scores(kb):
        return (_dot_nt(ql, kc_ref[keys(kb), :]) + _dot_nt(qr, kr_ref[keys(kb), :])) * ATTN_EXP2_SCALE

    def add_values(kb, slot):
        acc_ref[...] = (jnp.tile(a_ref[...], (1, KV_LORA // LANES)) * acc_ref[...]
                        + _dot(p_ref[slot], kc_ref[keys(kb), :]))

    def softmax(slot, masked):
        s = s_ref[slot]
        if masked:
            rows = lax.broadcasted_iota(jnp.int32, s.shape, 0) & (tq - 1)
            cols = lax.broadcasted_iota(jnp.int32, s.shape, 1)
            s = jnp.where(cols <= rows, s, NEG_BIG)
        m_prev = m_ref[...]
        m_new = jnp.maximum(m_prev, jnp.max(s, axis=-1, keepdims=True))
        alpha = jnp.exp2(m_prev - m_new)
        p = jnp.exp2(s - jnp.tile(m_new, (1, tq // LANES)))
        l_ref[...] = alpha * l_ref[...] + jnp.sum(p, axis=-1, keepdims=True)
        m_ref[...] = m_new
        a_ref[...] = alpha
        p_ref[slot] = p.astype(BF16)

    s_ref[0] = scores(0)

    def iteration(i, cur):
        s_ref[1 - cur] = scores(i + 1)
        add_values(jnp.maximum(i - 1, 0), 1 - cur)
        softmax(cur, False)

    def last(cur):
        add_values(jnp.maximum(qi - 1, 0), 1 - cur)
        softmax(cur, True)
        add_values(qi, cur)

    def pair(j, carry):
        iteration(2 * j, 0)
        iteration(2 * j + 1, 1)
        return carry

    lax.fori_loop(0, qi >> 1, pair, 0)

    @pl.when((qi & 1) == 1)
    def _():
        iteration(qi - 1, 0)
        last(1)

    @pl.when((qi & 1) == 0)
    def _():
        last(0)

    o_lat =(acc_ref[...] / jnp.tile(l_ref[...], (1, KV_LORA // LANES))).astype(BF16)
    for h in range(MLA_HEADS):
        o_ref[:, h * V_DIM:(h + 1) * V_DIM] = _dot(o_lat[h * tq:(h + 1) * tq, :], wuv_ref[h]).astype(o_ref.dtype)


def _attn_prompt(qlat, qrope, ckv_b, kr_b, w_uv, bsz, t):
    tq = 256
    assert tq & (tq - 1) == 0 and t % tq == 0
    nq = t // tq
    rows = MLA_HEADS * tq
    return pl.pallas_call(
        functools.partial(_attn_prompt_kernel, tq=tq),
        grid=(bsz, nq),
        in_specs=[
            pl.BlockSpec((tq, MLA_HEADS * KV_LORA), lambda b, i: (b * nq + i, 0)),
            pl.BlockSpec((tq, MLA_HEADS * QK_ROPE), lambda b, i: (b * nq + i, 0)),
            pl.BlockSpec((t, KV_LORA), lambda b, i: (b, 0)),
            pl.BlockSpec((t, QK_ROPE), lambda b, i: (b, 0)),
            pl.BlockSpec(w_uv.shape, lambda b, i: (0, 0, 0)),
        ],
        out_specs=pl.BlockSpec((tq, MLA_HEADS * V_DIM), lambda b, i: (b * nq + i, 0)),
        out_shape=jax.ShapeDtypeStruct((bsz * t, MLA_HEADS * V_DIM), BF16),
        scratch_shapes=[pltpu.VMEM((rows, LANES), F32), pltpu.VMEM((rows, LANES), F32),
                        pltpu.VMEM((rows, LANES), F32), pltpu.VMEM((rows, KV_LORA), F32),
                        pltpu.VMEM((2, rows, tq), F32), pltpu.VMEM((2, rows, tq), BF16)],
        compiler_params=_params("parallel", "parallel"),
        name="attn_prompt",
    )(qlat, qrope, ckv_b, kr_b, w_uv)


PAGES_PER_STEP = 16
PAGE_RING = 4
NEW_KEY_ROWS = 128


def _attn_sample_kernel(pt_ref, ql_ref, qr_ref, nc_ref, nr_ref, wuv_ref, cc_hbm, cr_hbm, o_ref,
                        cbuf, rbuf, sem, wq_ref, wr_ref, m_ref, l_ref, acc_ref, kv_ref, kq_ref, kr_ref,
                        *, t_new, n_q, n_steps):
    b = pl.program_id(0)
    nb = pl.num_programs(0)
    groups = LANES // n_q
    page = cbuf.shape[1] // PAGES_PER_STEP
    ppg = PAGES_PER_STEP // groups
    lat_rep = KV_LORA // LANES

    def page_copies(bb, step, slot):
        copies = []
        for i in range(PAGES_PER_STEP):
            pid = pt_ref[bb, step * PAGES_PER_STEP + i]
            copies.append(pltpu.make_async_copy(cc_hbm.at[pid], cbuf.at[slot, pl.ds(i * page, page), :],
                                                sem.at[slot]))
            copies.append(pltpu.make_async_copy(cr_hbm.at[pid], rbuf.at[slot, i], sem.at[slot]))
        return copies

    @pl.when(b == 0)
    def _():
        for step in range(PAGE_RING):
            for c in page_copies(0, step, step):
                c.start()

    q_shift = n_q.bit_length() - 1
    spread = (lax.broadcasted_iota(jnp.int32, (n_q, LANES), 1) & (n_q - 1)) == lax.broadcasted_iota(
        jnp.int32, (n_q, LANES), 0)
    spread = jnp.where(spread, 1.0, 0.0).astype(BF16)
    for src, dst in ((ql_ref, wq_ref), (qr_ref, wr_ref)):
        feat = src.shape[0]
        tiled = _dot(src[...], spread)
        lane_g = lax.broadcasted_iota(jnp.int32, tiled.shape, 1) >> q_shift
        for g in range(groups):
            dst[g * feat:(g + 1) * feat, :] = jnp.where(lane_g == g, tiled, 0.0).astype(BF16)

    m_ref[...] = jnp.full_like(m_ref, NEG_BIG)
    l_ref[...] = jnp.zeros_like(l_ref)
    acc_ref[...] = jnp.zeros_like(acc_ref)

    def lanes_to_rows(row):
        return jnp.broadcast_to(row, (LANES, LANES)).T

    def update(s, place, values):
        m_prev = m_ref[...]
        m_new = jnp.maximum(m_prev, jnp.max(s, axis=0, keepdims=True))
        alpha = jnp.exp2(m_prev - m_new)
        p = jnp.exp2(s - m_new)
        l_ref[...] = alpha * l_ref[...] + jnp.sum(p, axis=0, keepdims=True)
        m_ref[...] = m_new
        pv = _dot(place(p.T), values)
        acc_ref[...] = jnp.tile(lanes_to_rows(alpha), (1, lat_rep)) * acc_ref[...] + pv

    def block_diag(p_t):
        row_g = lax.broadcasted_iota(jnp.int32, p_t.shape, 0) >> q_shift
        return jnp.concatenate([jnp.where(row_g == g, p_t, 0.0).astype(BF16) for g in range(groups)], axis=1)

    def stage_and_score(j):
        ring = j % PAGE_RING
        slot = j & 1
        for c in page_copies(b, j, ring):
            c.wait()
        for i in range(PAGES_PER_STEP):
            g, r = divmod(i, ppg)
            x = cbuf[ring, i * page:(i + 1) * page, :].astype(BF16)
            kv_ref[slot, i * page:(i + 1) * page, :] = x
            kq_ref[slot, r * page:(r + 1) * page, g * KV_LORA:(g + 1) * KV_LORA] = x
            kr_ref[slot, g * QK_ROPE:(g + 1) * QK_ROPE, r * page:(r + 1) * page] = rbuf[ring, i].astype(BF16)
        if j + PAGE_RING < n_steps:
            nxt = page_copies(b, j + PAGE_RING, ring)
        else:
            nxt = page_copies(jnp.minimum(b + 1, nb - 1), j + PAGE_RING - n_steps, ring)
        for c in nxt:
            c.start()
        return (_dot(kq_ref[slot], wq_ref[...]) + _dot_tn(kr_ref[slot], wr_ref[...])) * ATTN_EXP2_SCALE

    s_cur = stage_and_score(0)
    for j in range(n_steps):
        s_next = stage_and_score(j + 1) if j + 1 < n_steps else None
        update(s_cur, block_diag, kv_ref[j & 1])
        s_cur = s_next

    @pl.when(b == nb - 1)
    def _():
        for step in range(PAGE_RING):
            for c in page_copies(b, step, step):
                c.wait()

    nc = nc_ref[...]
    s2 = (_dot(nc, wq_ref[:KV_LORA, :]) + _dot(nr_ref[...], wr_ref[:QK_ROPE, :])) * ATTN_EXP2_SCALE
    t_k = lax.broadcasted_iota(jnp.int32, s2.shape, 0)
    lane = lax.broadcasted_iota(jnp.int32, s2.shape, 1)
    ok = (lane < n_q) & (t_k <= (lane & (t_new - 1)))
    update(jnp.where(ok, s2, NEG_BIG), lambda p_t: p_t.astype(BF16), nc)

    grp = lambda a, g: a[g * n_q:(g + 1) * n_q]
    total = lambda a: functools.reduce(lambda x, y: x + y, [grp(a, g) for g in range(groups)])
    m_t = lanes_to_rows(m_ref[...])
    l_t = lanes_to_rows(l_ref[...])
    m_q = functools.reduce(jnp.maximum, [grp(m_t, g) for g in range(groups)])
    w = jnp.exp2(m_t - jnp.tile(m_q, (groups, 1)))
    l_q = total(w * l_t)
    o_lat = total(jnp.tile(w, (1, lat_rep)) * acc_ref[...]) / jnp.tile(l_q, (1, lat_rep))
    o_lat = o_lat.astype(BF16)
    head = lax.broadcasted_iota(jnp.int32, (n_q, V_DIM), 0) >> (t_new.bit_length() - 1)
    out = jnp.zeros((n_q, V_DIM), F32)
    for h in range(MLA_HEADS):
        out = out + jnp.where(head == h, _dot(o_lat, wuv_ref[h]), 0.0)
    o_ref[...] = out.astype(o_ref.dtype)


def _attn_sample(ql_t, qr_t, new_c, new_r, w_uv, cache_c, cache_r_t, page_table, t_new):
    bsz, _, n_q = ql_t.shape
    n_pages = page_table.shape[1]
    groups = LANES // n_q
    assert LANES % n_q == 0 and PAGES_PER_STEP % groups == 0 and n_pages % PAGES_PER_STEP == 0
    assert t_new & (t_new - 1) == 0 and n_q & (n_q - 1) == 0 and t_new <= NEW_KEY_ROWS and n_q >= SUBLANES
    page = cache_c.shape[1]
    keys = PAGES_PER_STEP * page
    n_steps = n_pages // PAGES_PER_STEP
    assert n_steps % PAGE_RING == 0

    per_b = lambda a: pl.BlockSpec((None,) + a.shape[1:], lambda b, pt: (b, 0, 0))
    grid_spec = pltpu.PrefetchScalarGridSpec(
        num_scalar_prefetch=1,
        grid=(bsz,),
        in_specs=[per_b(ql_t), per_b(qr_t), per_b(new_c), per_b(new_r),
                  pl.BlockSpec(w_uv.shape, lambda b, pt: (0, 0, 0)),
                  pl.BlockSpec(memory_space=pl.ANY), pl.BlockSpec(memory_space=pl.ANY)],
        out_specs=pl.BlockSpec((None, n_q, V_DIM), lambda b, pt: (b, 0, 0)),
        scratch_shapes=[pltpu.VMEM((PAGE_RING, keys, KV_LORA), cache_c.dtype),
                        pltpu.VMEM((PAGE_RING, PAGES_PER_STEP, QK_ROPE, page), cache_r_t.dtype),
                        pltpu.SemaphoreType.DMA((PAGE_RING,)),
                        pltpu.VMEM((groups * KV_LORA, LANES), BF16),
                        pltpu.VMEM((groups * QK_ROPE, LANES), BF16),
                        pltpu.VMEM((1, LANES), F32), pltpu.VMEM((1, LANES), F32),
                        pltpu.VMEM((LANES, KV_LORA), F32),
                        pltpu.VMEM((2, keys, KV_LORA), BF16),
                        pltpu.VMEM((2, keys // groups, groups * KV_LORA), BF16),
                        pltpu.VMEM((2, groups * QK_ROPE, keys // groups), BF16)],
    )
    return pl.pallas_call(
        functools.partial(_attn_sample_kernel, t_new=t_new, n_q=n_q, n_steps=n_steps),
        grid_spec=grid_spec,
        out_shape=jax.ShapeDtypeStruct((bsz, n_q, V_DIM), BF16),
        compiler_params=_params("arbitrary"),
        name="attn_sample",
    )(page_table, ql_t, qr_t, new_c, new_r, w_uv, cache_c, cache_r_t)


def _rot_cols(w):
    half = w.shape[-1] // 2
    return jnp.concatenate([-w[..., half:], w[..., :half]], axis=-1)


def _prep_weights(norm_g, w_ffn_gate, w_ffn_up, w_ffn_down, w_in, q_norm_g, w_q_up, kv_norm_g,
                  w_kv_up, hg_norm_g, w_out):
    bf = lambda a: a.astype(BF16)
    w_h, w_cq, w_ckv, w_kr = jnp.split(w_in, [4 * HG_W, 4 * HG_W + Q_LORA, 4 * HG_W + Q_LORA + KV_LORA], axis=-1)
    w_q_rope = w_q_up[..., QK_NOPE:]
    return {
        "norm_g": norm_g.reshape(-1, 1, D_MODEL),
        "wg": bf(w_ffn_gate), "wu": bf(w_ffn_up), "wd": bf(w_ffn_down),
        "w_h": bf(w_h), "w_cq": bf(w_cq), "w_ckv": bf(w_ckv),
        "w_kr": bf(jnp.concatenate([w_kr, _rot_cols(w_kr)], axis=-1)),
        "q_norm_g": q_norm_g.reshape(1, Q_LORA), "kv_norm_g": kv_norm_g.reshape(1, KV_LORA),
        "w_qn": bf(w_q_up[..., :QK_NOPE].reshape(Q_LORA, MLA_HEADS * QK_NOPE)),
        "w_qr": bf(w_q_rope.reshape(Q_LORA, MLA_HEADS * QK_ROPE)),
        "w_qrr": bf(_rot_cols(w_q_rope).reshape(Q_LORA, MLA_HEADS * QK_ROPE)),
        "w_uk": bf(jnp.transpose(w_kv_up[..., :QK_NOPE], (1, 2, 0))),
        "w_uv": bf(jnp.transpose(w_kv_up[..., QK_NOPE:], (1, 0, 2))),
        "hg_norm_g": hg_norm_g.reshape(1, HG_DV),
        "w_out1": bf(w_out[:HG_W]), "w_out2": bf(w_out[HG_W:]),
    }


def _rope_tables(pos):
    half = QK_ROPE // 2
    inv = ROPE_THETA ** (-jnp.arange(half, dtype=F32) / half)
    ang = pos.astype(F32)[:, None] * inv[None, :]
    cos = jnp.tile(jnp.cos(ang), (1, 2))
    sin = jnp.tile(jnp.sin(ang), (1, 2))
    return (jnp.concatenate([cos, sin], axis=-1), jnp.tile(cos, (1, MLA_HEADS)), jnp.tile(sin, (1, MLA_HEADS)))


def kernel(x_prompt, x_sample, cache_kv_latent, cache_k_rope, state_hgrn, page_table, hgrn_lb_logits,
           norm_g, w_ffn_gate, w_ffn_up, w_ffn_down, w_in, q_norm_g, w_q_up, kv_norm_g, w_kv_up,
           hg_norm_g, w_out):
    bp, tp, _ = x_prompt.shape
    bs, ts, _ = x_sample.shape
    depth = norm_g.shape[0]
    assert depth == 1
    past_len = page_table.shape[1] * cache_kv_latent.shape[2]
    w = _prep_weights(norm_g[0], w_ffn_gate[0], w_ffn_up[0], w_ffn_down[0], w_in[0], q_norm_g[0],
                      w_q_up[0], kv_norm_g[0], w_kv_up[0], hg_norm_g[0], w_out[0])
    ng = w["norm_g"]
    lb_logits = hgrn_lb_logits.astype(F32)

    def pre(x, tables):
        h1 = _ffn_block(x, ng[0], ng[1], w["wg"], w["wu"], w["wd"], 0)
        return (h1,) + tuple(_in_proj(h1, ng[2], *tables, w))

    def post(h1, o_h, o_m):
        return _ffn_block(h1, ng[4], ng[5], w["wg"], w["wu"], w["wd"], 1,
                          mix=(o_h, o_m, w["w_out1"], w["w_out2"], ng[3]))

    h1, zh, ckv, kr, ckv_b, kr_b, qlat, qrope = pre(
        x_prompt.reshape(bp * tp, D_MODEL), _rope_tables(jnp.arange(tp, dtype=jnp.int32)))
    o_h, st_p = _hgrn(zh.reshape(bp, tp, 4 * HG_W), lb_logits, w["hg_norm_g"], None, chunk=64, t_valid=None)
    o_m = _attn_prompt(qlat, qrope, ckv_b, kr_b, w["w_uv"], bp, tp)
    y_p = post(h1, o_h.reshape(bp * tp, HG_W), o_m).reshape(bp, tp, D_MODEL)
    ckv_p = ckv.reshape(1, bp, tp, KV_LORA)
    kr_p = kr.reshape(1, bp, tp, QK_ROPE)

    pos_s = past_len + jnp.arange(ts, dtype=jnp.int32)
    tabs = tuple(jnp.tile(a, (bs, 1)) for a in _rope_tables(pos_s))
    h1s, zhs, ckvs, krs, ckvs_b, krs_b, qlats, qropes = pre(x_sample.reshape(bs * ts, D_MODEL), tabs)
    t_pad = -(-ts // SUBLANES) * SUBLANES
    zhs_pad = jnp.pad(zhs.reshape(bs, ts, 4 * HG_W), ((0, 0), (0, t_pad - ts), (0, 0)))
    o_hs, st_s = _hgrn(zhs_pad, lb_logits, w["hg_norm_g"], state_hgrn[0], chunk=SUBLANES,
                       t_valid=ts if t_pad != ts else None)
    q_cols = lambda a, wd: a.reshape(bs, ts, MLA_HEADS, wd).transpose(0, 3, 2, 1).reshape(bs, wd, MLA_HEADS * ts)
    pad_new = lambda a, wd: jnp.pad(a.reshape(bs, ts, wd), ((0, 0), (0, NEW_KEY_ROWS - ts), (0, 0)))
    o_ms = _attn_sample(q_cols(qlats, KV_LORA), q_cols(qropes, QK_ROPE),
                        pad_new(ckvs_b, KV_LORA), pad_new(krs_b, QK_ROPE), w["w_uv"],
                        cache_kv_latent[0], jnp.swapaxes(cache_k_rope[0], 1, 2), page_table, ts)
    o_ms = o_ms.reshape(bs, MLA_HEADS, ts, V_DIM).transpose(0, 2, 1, 3).reshape(bs * ts, MLA_HEADS * V_DIM)
    y_s = post(h1s, o_hs[:, :ts].reshape(bs * ts, HG_W), o_ms).reshape(bs, ts, D_MODEL)

    return (y_p, y_s, ckv_p, kr_p, st_p[None], ckvs.reshape(1, bs, ts, KV_LORA),
            krs.reshape(1, bs, ts, QK_ROPE), st_s[None])
```

```python
import functools
import math

import jax
import jax.numpy as jnp
from jax import lax
from jax.experimental import pallas as pl
from jax.experimental.pallas import tpu as pltpu

D_MODEL = 1024
D_FF = 2816
HG_HEADS = 4
HG_DK = 128
HG_DV = 128
MLA_HEADS = 4
QK_NOPE = 128
QK_ROPE = 64
V_DIM = 128
Q_LORA = 768
KV_LORA = 256
ROPE_THETA = 10000.0
EPS = 1e-6
HG_W = HG_HEADS * HG_DK

VMEM_LIMIT_BYTES = 56 * 1024 * 1024
SUBLANES = 8
NEG_BIG = -1e30
LOG2E = math.log2(math.e)

F32 = jnp.float32
BF16 = jnp.bfloat16


def _dot(a, b):
    return jnp.dot(a, b, preferred_element_type=F32)


def _dot_nt(a, b):
    return lax.dot_general(a, b, (((1,), (1,)), ((), ())), preferred_element_type=F32)


def _dot_tn(a, b):
    return lax.dot_general(a, b, (((0,), (0,)), ((), ())), preferred_element_type=F32)


def _rms(x, g):
    return x * lax.rsqrt(jnp.mean(x * x, axis=-1, keepdims=True) + EPS) * g


def _sigmoid(x):
    return 1.0 / (1.0 + jnp.exp(-x))


def _silu(x):
    return x * _sigmoid(x)


def _params(*sem):
    return pltpu.CompilerParams(dimension_semantics=sem, vmem_limit_bytes=VMEM_LIMIT_BYTES)


FFN_CHUNK = 256


def _ffn_kernel(*refs, mix_in):
    if mix_in:
        h_ref, oh_ref, om_ref, w1_ref, w2_ref, gmix_ref = refs[:6]
        refs = refs[6:]
        mixed = _dot(oh_ref[...], w1_ref[...]) + _dot(om_ref[...], w2_ref[...])
        x = h_ref[...] + _rms(mixed, gmix_ref[...])
    else:
        x = refs[0][...]
        refs = refs[1:]
    gpre_ref, gpost_ref, wg_ref, wu_ref, wd_ref, o_ref = refs
    xn = _rms(x, gpre_ref[...]).astype(BF16)
    acc = None
    for f in range(D_FF // FFN_CHUNK):
        cols = slice(f * FFN_CHUNK, (f + 1) * FFN_CHUNK)
        a = _dot(xn, wg_ref[:, cols])
        b = _dot(xn, wu_ref[:, cols])
        part = _dot((_silu(a) * b).astype(BF16), wd_ref[cols, :])
        acc = part if acc is None else acc + part
    o_ref[...] = x + 0.5 * _rms(acc, gpost_ref[...])


def _ffn_block(x, g_pre, g_post, wg, wu, wd, which, mix=None):
    m = x.shape[0]
    tm = min(m, 512)

    def const(a):
        if a.ndim == 2:
            return pl.BlockSpec(a.shape, lambda i: (0, 0), pipeline_mode=pl.Buffered(1))
        return pl.BlockSpec((None,) + a.shape[1:], lambda i: (which, 0, 0), pipeline_mode=pl.Buffered(1))

    row = lambda a: pl.BlockSpec((tm, a.shape[1]), lambda i: (i, 0))
    args, in_specs = [x], [row(x)]
    if mix is not None:
        o_h, o_m, w1, w2, g_mix = mix
        args += [o_h, o_m, w1, w2, g_mix]
        in_specs += [row(o_h), row(o_m), const(w1), const(w2), const(g_mix)]
    args += [g_pre, g_post, wg, wu, wd]
    in_specs += [const(g_pre), const(g_post), const(wg), const(wu), const(wd)]
    return pl.pallas_call(
        functools.partial(_ffn_kernel, mix_in=mix is not None),
        grid=(m // tm,),
        in_specs=in_specs,
        out_specs=row(x),
        out_shape=jax.ShapeDtypeStruct((m, D_MODEL), F32),
        compiler_params=_params("parallel"),
        name="ffn_mix_block" if mix is not None else "ffn_block",
    )(*args)


def _inproj_kernel(x_ref, g_ref, csk_ref, cosq_ref, sinq_ref, wh_ref, wcq_ref, wckv_ref, wkr_ref,
                   qg_ref, kvg_ref, wqn_ref, wqr_ref, wqrr_ref, wuk_ref,
                   zh_ref, ckv_ref, kr_ref, ckvb_ref, krb_ref, qlat_ref, qrope_ref):
    u = _rms(x_ref[...], g_ref[...]).astype(BF16)
    cq = _dot(u, wcq_ref[...])
    ckv = _dot(u, wckv_ref[...])
    kz = _dot(u, wkr_ref[...])
    half = wh_ref.shape[1] // 2
    zh_ref[:, :half] = _dot(u, wh_ref[:, :half])
    cqn = _rms(cq, qg_ref[...]).astype(BF16)
    qn = _dot(cqn, wqn_ref[...])
    qr_a = _dot(cqn, wqr_ref[...])
    qr_b = _dot(cqn, wqrr_ref[...])
    zh_ref[:, half:] = _dot(u, wh_ref[:, half:])
    qn = qn.astype(BF16)
    for h in range(MLA_HEADS):
        qlat_ref[:, h * KV_LORA:(h + 1) * KV_LORA] = _dot(
            qn[:, h * QK_NOPE:(h + 1) * QK_NOPE], wuk_ref[h]).astype(BF16)
    ckv = _rms(ckv, kvg_ref[...])
    ckv_ref[...] = ckv
    ckvb_ref[...] = ckv.astype(BF16)
    csk = csk_ref[...]
    kr = kz[:, :QK_ROPE] * csk[:, :QK_ROPE] + kz[:, QK_ROPE:] * csk[:, QK_ROPE:]
    kr_ref[...] = kr
    krb_ref[...] = kr.astype(BF16)
    qrope_ref[...] = (qr_a * cosq_ref[...] + qr_b * sinq_ref[...]).astype(BF16)


def _in_proj(x, g, csk, cosq, sinq, w):
    m = x.shape[0]
    tm = min(m, 512)
    nrep = csk.shape[0] // tm
    full = lambda a: pl.BlockSpec(a.shape, lambda i: (0,) * a.ndim, pipeline_mode=pl.Buffered(1))
    row = lambda width: pl.BlockSpec((tm, width), lambda i: (i, 0))
    tab = lambda width: pl.BlockSpec((tm, width), lambda i: (i % nrep, 0))
    rope_w = MLA_HEADS * QK_ROPE
    return pl.pallas_call(
        _inproj_kernel,
        grid=(m // tm,),
        in_specs=[row(D_MODEL), full(g), tab(2 * QK_ROPE), tab(rope_w), tab(rope_w),
                  full(w["w_h"]), full(w["w_cq"]), full(w["w_ckv"]), full(w["w_kr"]),
                  full(w["q_norm_g"]), full(w["kv_norm_g"]), full(w["w_qn"]), full(w["w_qr"]),
                  full(w["w_qrr"]), full(w["w_uk"])],
        out_specs=[row(4 * HG_W), row(KV_LORA), row(QK_ROPE), row(KV_LORA), row(QK_ROPE),
                   row(MLA_HEADS * KV_LORA), row(rope_w)],
        out_shape=[
            jax.ShapeDtypeStruct((m, 4 * HG_W), F32),
            jax.ShapeDtypeStruct((m, KV_LORA), F32),
            jax.ShapeDtypeStruct((m, QK_ROPE), F32),
            jax.ShapeDtypeStruct((m, KV_LORA), BF16),
            jax.ShapeDtypeStruct((m, QK_ROPE), BF16),
            jax.ShapeDtypeStruct((m, MLA_HEADS * KV_LORA), BF16),
            jax.ShapeDtypeStruct((m, rope_w), BF16),
        ],
        compiler_params=_params("parallel"),
        name="in_proj",
    )(x, g, csk, cosq, sinq, w["w_h"], w["w_cq"], w["w_ckv"], w["w_kr"], w["q_norm_g"],
      w["kv_norm_g"], w["w_qn"], w["w_qr"], w["w_qrr"], w["w_uk"])


def _split3(g):
    g1 = g.astype(BF16).astype(F32)
    r1 = g - g1
    g2 = r1.astype(BF16).astype(F32)
    g3 = (r1 - g2).astype(BF16).astype(F32)
    return g1, g2, g3


HG_SUB = 16


def _hgrn_kernel(*refs, chunk, t_valid, has_s0):
    if has_s0:
        hq_ref, hf_ref, hi_ref, hg_ref, lbl_ref, ng_ref, s0_ref = refs[:7]
        refs = refs[7:]
    else:
        hq_ref, hf_ref, hi_ref, hg_ref, lbl_ref, ng_ref = refs[:6]
        s0_ref = None
        refs = refs[6:]
    o_ref, sfin_ref, st_ref, c_ref, v_ref = refs
    ti = pl.program_id(1)
    t_tile = hq_ref.shape[0]
    n_chunks = t_tile // chunk
    n_tiles = chunk // SUBLANES

    lbl = lbl_ref[...]
    e = jnp.exp(lbl - jnp.max(lbl, axis=0, keepdims=True))
    lb_all = e[0:1, :] / jnp.sum(e, axis=0, keepdims=True)

    @pl.when(ti == 0)
    def _():
        for h in range(HG_HEADS):
            st_ref[h] = s0_ref[h].T if has_s0 else jnp.zeros((HG_DV, HG_DK), F32)

    r_i = lax.broadcasted_iota(jnp.int32, (chunk, chunk), 0)
    c_i = lax.broadcasted_iota(jnp.int32, (chunk, chunk), 1)
    tril = (c_i <= r_i).astype(F32)
    row8 = lax.broadcasted_iota(jnp.int32, (SUBLANES, HG_DK), 0)
    ng = ng_ref[...]
    ones = jnp.ones((HG_DK, HG_DV), BF16)
    sub = min(chunk, HG_SUB)
    n_sub = chunk // sub
    pairs = [(s, i) for s in range(chunk) for i in range(s // SUBLANES, (s // sub + 1) * sub // SUBLANES)]
    if n_sub > 1:
        assert sub & (sub - 1) == 0 and chunk & (chunk - 1) == 0
        rows_sub = lax.broadcasted_iota(jnp.int32, (chunk, (n_sub - 1) * chunk), 0) >> (sub.bit_length() - 1)
        cols_blk = lax.broadcasted_iota(jnp.int32, (chunk, (n_sub - 1) * chunk), 1) >> (chunk.bit_length() - 1)
        sub_mask = cols_blk == rows_sub - 1

    def gates(h, start):
        sl = pl.ds(start, chunk)
        hl = slice(h * HG_DK, (h + 1) * HG_DK)
        lb = lb_all[:, hl]
        q = _silu(hq_ref[sl, hl])
        f = lb + (1.0 - lb) * _sigmoid(hf_ref[sl, hl])
        g = jnp.log(f)
        k = 1.0 - f
        v = hi_ref[sl, hl]
        if t_valid is not None:
            pos = lax.broadcasted_iota(jnp.int32, (chunk, HG_DK), 0) + (ti * t_tile + start)
            valid = pos < t_valid
            g = jnp.where(valid, g, 0.0)
            k = jnp.where(valid, k, 0.0)
            v = jnp.where(valid, v, 0.0)
        g1, g2, g3 = _split3(g)
        b = _dot(tril, g1) + _dot(tril, g2) + _dot(tril, g3)
        return q, k, v, b * LOG2E

    def products(h, q, k, v, b2):
        c_ref[h] = b2 - jnp.log(k) * LOG2E
        v_ref[h] = v
        q_t = [q[i * SUBLANES:(i + 1) * SUBLANES, :] for i in range(n_tiles)]
        b_t = [b2[i * SUBLANES:(i + 1) * SUBLANES, :] for i in range(n_tiles)]
        p_tiles = []
        for s, i in pairs:
            d = b_t[i] - c_ref[h, s:s + 1, :]
            if i == s // SUBLANES and s % SUBLANES:
                d = jnp.where(row8 >= s % SUBLANES, d, NEG_BIG)
            p_tiles.append(q_t[i] * jnp.exp2(d))
        a_rep = _dot(jnp.concatenate(p_tiles, axis=0).astype(BF16), ones)

        a_sub = None
        if n_sub > 1:
            zeros_sub = jnp.zeros((sub, HG_DK), F32)
            q_parts, k_parts = [zeros_sub], []
            for j in range(1, n_sub):
                lo = j * sub
                r_j = b2[lo - 1:lo, :]
                q_parts.append(q[lo:lo + sub, :] * jnp.exp2(b2[lo:lo + sub, :] - r_j))
                k_parts += [k[:lo, :] * jnp.exp2(r_j - b2[:lo, :])] + [zeros_sub] * (n_sub - j)
            a_sub = _dot_nt(jnp.concatenate(q_parts, axis=0).astype(BF16),
                            jnp.concatenate(k_parts, axis=0).astype(BF16))

        st = st_ref[h]
        o_state = _dot_nt((q * jnp.exp2(b2)).astype(BF16), st.astype(BF16))
        bl = b2[chunk - 1:chunk, :]
        v_b = v.astype(BF16)
        st_ref[h] = st * jnp.exp2(bl) + _dot_tn(v_b, (k * jnp.exp2(bl - b2)).astype(BF16))
        return a_rep, a_sub, o_state, v_b

    def outputs(h, start, a_rep, a_sub, o_state, v_b):
        sl = pl.ds(start, chunk)
        hl = slice(h * HG_DK, (h + 1) * HG_DK)
        o_t = [o_state[i * SUBLANES:(i + 1) * SUBLANES, :] for i in range(n_tiles)]
        for j, (s, i) in enumerate(pairs):
            o_t[i] = o_t[i] + a_rep[j * SUBLANES:(j + 1) * SUBLANES, :] * v_ref[h, s:s + 1, :]
        o = jnp.concatenate(o_t, axis=0) if n_tiles > 1 else o_t[0]
        if a_sub is not None:
            o = o + _dot(jnp.where(sub_mask, a_sub, 0.0).astype(BF16),
                         jnp.concatenate([v_b] * (n_sub - 1), axis=0))
        o_ref[sl, hl] = (_rms(o, ng) * _silu(hg_ref[sl, hl])).astype(o_ref.dtype)

    def chunk_body(c, carry):
        start = pl.multiple_of(c * chunk, chunk)
        heads = range(HG_HEADS)
        gated = [gates(h, start) for h in heads]
        prods = [products(h, *gated[h]) for h in heads]
        for h in heads:
            outputs(h, start, *prods[h])
        return carry

    lax.fori_loop(0, n_chunks, chunk_body, 0, unroll=4 if n_chunks % 4 == 0 else 1)

    @pl.when(ti == pl.num_programs(1) - 1)
    def _():
        for h in range(HG_HEADS):
            sfin_ref[h] = st_ref[h].T


def _hgrn(zh, lb_logits, ng, s0, chunk, t_valid):
    bsz, t, _ = zh.shape
    has_s0 = s0 is not None
    t_tile = min(t, 512)
    assert t % t_tile == 0 and t_tile % chunk == 0
    col = lambda j: pl.BlockSpec((None, t_tile, HG_W), lambda b, i: (b, i, j))
    in_specs = [col(0), col(1), col(2), col(3),
                pl.BlockSpec(lb_logits.shape, lambda b, i: (0, 0)),
                pl.BlockSpec((1, HG_DV), lambda b, i: (0, 0))]
    args = [zh, zh, zh, zh, lb_logits, ng]
    st_spec = pl.BlockSpec((None, HG_HEADS, HG_DK, HG_DV), lambda b, i: (b, 0, 0, 0))
    if has_s0:
        in_specs.append(st_spec)
        args.append(s0)
    return pl.pallas_call(
        functools.partial(_hgrn_kernel, chunk=chunk, t_valid=t_valid, has_s0=has_s0),
        grid=(bsz, t // t_tile),
        in_specs=in_specs,
        out_specs=[pl.BlockSpec((None, t_tile, HG_W), lambda b, i: (b, i, 0)), st_spec],
        out_shape=[jax.ShapeDtypeStruct((bsz, t, HG_W), BF16),
                   jax.ShapeDtypeStruct((bsz, HG_HEADS, HG_DK, HG_DV), F32)],
        scratch_shapes=[pltpu.VMEM((HG_HEADS, HG_DV, HG_DK), F32),
                        pltpu.VMEM((HG_HEADS, chunk, HG_DK), F32),
                        pltpu.VMEM((HG_HEADS, chunk, HG_DV), F32)],
        compiler_params=_params("parallel", "arbitrary"),
        name="hgrn",
    )(*args)


ATTN_SCALE = 1.0 / math.sqrt(QK_NOPE + QK_ROPE)
ATTN_EXP2_SCALE = ATTN_SCALE * LOG2E

LANES = 128


def _attn_prompt_kernel(ql_ref, qr_ref, kc_ref, kr_ref, wuv_ref, o_ref,
                        m_ref, l_ref, a_ref, acc_ref, s_ref, p_ref, *, tq):
    qi = pl.program_id(1)
    ql = jnp.concatenate([ql_ref[:, h * KV_LORA:(h + 1) * KV_LORA] for h in range(MLA_HEADS)], axis=0)
    qr = jnp.concatenate([qr_ref[:, h * QK_ROPE:(h + 1) * QK_ROPE] for h in range(MLA_HEADS)], axis=0)
    m_ref[...] = jnp.full_like(m_ref, NEG_BIG)
    l_ref[...] = jnp.zeros_like(l_ref)
    a_ref[...] = jnp.zeros_like(a_ref)
    acc_ref[...] = jnp.zeros_like(acc_ref)
    p_ref[1] = jnp.zeros(p_ref.shape[1:], BF16)

    def keys(kb):
        return pl.ds(pl.multiple_of(kb * tq, tq), tq)

    def scores(kb):
        return (_dot_nt(ql, kc_ref[keys(kb), :]) + _dot_nt(qr, kr_ref[keys(kb), :])) * ATTN_EXP2_SCALE

    def add_values(kb, slot):
        acc_ref[...] = (jnp.tile(a_ref[...], (1, KV_LORA // LANES)) * acc_ref[...]
                        + _dot(p_ref[slot], kc_ref[keys(kb), :]))

    def softmax(slot, masked):
        s = s_ref[slot]
        if masked:
            rows = lax.broadcasted_iota(jnp.int32, s.shape, 0) & (tq - 1)
            cols = lax.broadcasted_iota(jnp.int32, s.shape, 1)
            s = jnp.where(cols <= rows, s, NEG_BIG)
        m_prev = m_ref[...]
        m_new = jnp.maximum(m_prev, jnp.max(s, axis=-1, keepdims=True))
        alpha = jnp.exp2(m_prev - m_new)
        p = jnp.exp2(s - jnp.tile(m_new, (1, tq // LANES)))
        l_ref[...] = alpha * l_ref[...] + jnp.sum(p, axis=-1, keepdims=True)
        m_ref[...] = m_new
        a_ref[...] = alpha
        p_ref[slot] = p.astype(BF16)

    s_ref[0] = scores(0)

    def iteration(i, cur):
        s_ref[1 - cur] = scores(i + 1)
        add_values(jnp.maximum(i - 1, 0), 1 - cur)
        softmax(cur, False)

    def last(cur):
        add_values(jnp.maximum(qi - 1, 0), 1 - cur)
        softmax(cur, True)
        add_values(qi, cur)

    def pair(j, carry):
        iteration(2 * j, 0)
        iteration(2 * j + 1, 1)
        return carry

    lax.fori_loop(0, qi >> 1, pair, 0)

    @pl.when((qi & 1) == 1)
    def _():
        iteration(qi - 1, 0)
        last(1)

    @pl.when((qi & 1) == 0)
    def _():
        last(0)

    o_lat =(acc_ref[...] / jnp.tile(l_ref[...], (1, KV_LORA // LANES))).astype(BF16)
    for h in range(MLA_HEADS):
        o_ref[:, h * V_DIM:(h + 1) * V_DIM] = _dot(o_lat[h * tq:(h + 1) * tq, :], wuv_ref[h]).astype(o_ref.dtype)


def _attn_prompt(qlat, qrope, ckv_b, kr_b, w_uv, bsz, t):
    tq = 256
    assert tq & (tq - 1) == 0 and t % tq == 0
    nq = t // tq
    rows = MLA_HEADS * tq
    return pl.pallas_call(
        functools.partial(_attn_prompt_kernel, tq=tq),
        grid=(bsz, nq),
        in_specs=[
            pl.BlockSpec((tq, MLA_HEADS * KV_LORA), lambda b, i: (b * nq + i, 0)),
            pl.BlockSpec((tq, MLA_HEADS * QK_ROPE), lambda b, i: (b * nq + i, 0)),
            pl.BlockSpec((t, KV_LORA), lambda b, i: (b, 0)),
            pl.BlockSpec((t, QK_ROPE), lambda b, i: (b, 0)),
            pl.BlockSpec(w_uv.shape, lambda b, i: (0, 0, 0)),
        ],
        out_specs=pl.BlockSpec((tq, MLA_HEADS * V_DIM), lambda b, i: (b * nq + i, 0)),
        out_shape=jax.ShapeDtypeStruct((bsz * t, MLA_HEADS * V_DIM), BF16),
        scratch_shapes=[pltpu.VMEM((rows, LANES), F32), pltpu.VMEM((rows, LANES), F32),
                        pltpu.VMEM((rows, LANES), F32), pltpu.VMEM((rows, KV_LORA), F32),
                        pltpu.VMEM((2, rows, tq), F32), pltpu.VMEM((2, rows, tq), BF16)],
        compiler_params=_params("parallel", "parallel"),
        name="attn_prompt",
    )(qlat, qrope, ckv_b, kr_b, w_uv)


PAGES_PER_STEP = 16
PAGE_RING = 4
NEW_KEY_ROWS = 128


def _attn_sample_kernel(pt_ref, ql_ref, qr_ref, nc_ref, nr_ref, wuv_ref, cc_hbm, cr_hbm, o_ref,
                        cbuf, rbuf, sem, wq_ref, wr_ref, m_ref, l_ref, acc_ref, kv_ref, kq_ref, kr_ref,
                        *, t_new, n_q, n_steps):
    b = pl.program_id(0)
    nb = pl.num_programs(0)
    groups = LANES // n_q
    page = cbuf.shape[1] // PAGES_PER_STEP
    ppg = PAGES_PER_STEP // groups
    lat_rep = KV_LORA // LANES

    def page_copies(bb, step, slot):
        copies = []
        for i in range(PAGES_PER_STEP):
            pid = pt_ref[bb, step * PAGES_PER_STEP + i]
            copies.append(pltpu.make_async_copy(cc_hbm.at[pid], cbuf.at[slot, pl.ds(i * page, page), :],
                                                sem.at[slot]))
            copies.append(pltpu.make_async_copy(cr_hbm.at[pid], rbuf.at[slot, i], sem.at[slot]))
        return copies

    @pl.when(b == 0)
    def _():
        for step in range(PAGE_RING):
            for c in page_copies(0, step, step):
                c.start()

    q_shift = n_q.bit_length() - 1
    spread = (lax.broadcasted_iota(jnp.int32, (n_q, LANES), 1) & (n_q - 1)) == lax.broadcasted_iota(
        jnp.int32, (n_q, LANES), 0)
    spread = jnp.where(spread, 1.0, 0.0).astype(BF16)
    for src, dst in ((ql_ref, wq_ref), (qr_ref, wr_ref)):
        feat = src.shape[0]
        tiled = _dot(src[...], spread)
        lane_g = lax.broadcasted_iota(jnp.int32, tiled.shape, 1) >> q_shift
        for g in range(groups):
            dst[g * feat:(g + 1) * feat, :] = jnp.where(lane_g == g, tiled, 0.0).astype(BF16)

    m_ref[...] = jnp.full_like(m_ref, NEG_BIG)
    l_ref[...] = jnp.zeros_like(l_ref)
    acc_ref[...] = jnp.zeros_like(acc_ref)

    def lanes_to_rows(row):
        return jnp.broadcast_to(row, (LANES, LANES)).T

    def update(s, place, values):
        m_prev = m_ref[...]
        m_new = jnp.maximum(m_prev, jnp.max(s, axis=0, keepdims=True))
        alpha = jnp.exp2(m_prev - m_new)
        p = jnp.exp2(s - m_new)
        l_ref[...] = alpha * l_ref[...] + jnp.sum(p, axis=0, keepdims=True)
        m_ref[...] = m_new
        pv = _dot(place(p.T), values)
        acc_ref[...] = jnp.tile(lanes_to_rows(alpha), (1, lat_rep)) * acc_ref[...] + pv

    def block_diag(p_t):
        row_g = lax.broadcasted_iota(jnp.int32, p_t.shape, 0) >> q_shift
        return jnp.concatenate([jnp.where(row_g == g, p_t, 0.0).astype(BF16) for g in range(groups)], axis=1)

    def stage_and_score(j):
        ring = j % PAGE_RING
        slot = j & 1
        for c in page_copies(b, j, ring):
            c.wait()
        for i in range(PAGES_PER_STEP):
            g, r = divmod(i, ppg)
            x = cbuf[ring, i * page:(i + 1) * page, :].astype(BF16)
            kv_ref[slot, i * page:(i + 1) * page, :] = x
            kq_ref[slot, r * page:(r + 1) * page, g * KV_LORA:(g + 1) * KV_LORA] = x
            kr_ref[slot, g * QK_ROPE:(g + 1) * QK_ROPE, r * page:(r + 1) * page] = rbuf[ring, i].astype(BF16)
        if j + PAGE_RING < n_steps:
            nxt = page_copies(b, j + PAGE_RING, ring)
        else:
            nxt = page_copies(jnp.minimum(b + 1, nb - 1), j + PAGE_RING - n_steps, ring)
        for c in nxt:
            c.start()
        return (_dot(kq_ref[slot], wq_ref[...]) + _dot_tn(kr_ref[slot], wr_ref[...])) * ATTN_EXP2_SCALE

    s_cur = stage_and_score(0)
    for j in range(n_steps):
        s_next = stage_and_score(j + 1) if j + 1 < n_steps else None
        update(s_cur, block_diag, kv_ref[j & 1])
        s_cur = s_next

    @pl.when(b == nb - 1)
    def _():
        for step in range(PAGE_RING):
            for c in page_copies(b, step, step):
                c.wait()

    nc = nc_ref[...]
    s2 = (_dot(nc, wq_ref[:KV_LORA, :]) + _dot(nr_ref[...], wr_ref[:QK_ROPE, :])) * ATTN_EXP2_SCALE
    t_k = lax.broadcasted_iota(jnp.int32, s2.shape, 0)
    lane = lax.broadcasted_iota(jnp.int32, s2.shape, 1)
    ok = (lane < n_q) & (t_k <= (lane & (t_new - 1)))
    update(jnp.where(ok, s2, NEG_BIG), lambda p_t: p_t.astype(BF16), nc)

    grp = lambda a, g: a[g * n_q:(g + 1) * n_q]
    total = lambda a: functools.reduce(lambda x, y: x + y, [grp(a, g) for g in range(groups)])
    m_t = lanes_to_rows(m_ref[...])
    l_t = lanes_to_rows(l_ref[...])
    m_q = functools.reduce(jnp.maximum, [grp(m_t, g) for g in range(groups)])
    w = jnp.exp2(m_t - jnp.tile(m_q, (groups, 1)))
    l_q = total(w * l_t)
    o_lat = total(jnp.tile(w, (1, lat_rep)) * acc_ref[...]) / jnp.tile(l_q, (1, lat_rep))
    o_lat = o_lat.astype(BF16)
    head = lax.broadcasted_iota(jnp.int32, (n_q, V_DIM), 0) >> (t_new.bit_length() - 1)
    out = jnp.zeros((n_q, V_DIM), F32)
    for h in range(MLA_HEADS):
        out = out + jnp.where(head == h, _dot(o_lat, wuv_ref[h]), 0.0)
    o_ref[...] = out.astype(o_ref.dtype)


def _attn_sample(ql_t, qr_t, new_c, new_r, w_uv, cache_c, cache_r_t, page_table, t_new):
    bsz, _, n_q = ql_t.shape
    n_pages = page_table.shape[1]
    groups = LANES // n_q
    assert LANES % n_q == 0 and PAGES_PER_STEP % groups == 0 and n_pages % PAGES_PER_STEP == 0
    assert t_new & (t_new - 1) == 0 and n_q & (n_q - 1) == 0 and t_new <= NEW_KEY_ROWS and n_q >= SUBLANES
    page = cache_c.shape[1]
    keys = PAGES_PER_STEP * page
    n_steps = n_pages // PAGES_PER_STEP
    assert n_steps % PAGE_RING == 0

    per_b = lambda a: pl.BlockSpec((None,) + a.shape[1:], lambda b, pt: (b, 0, 0))
    grid_spec = pltpu.PrefetchScalarGridSpec(
        num_scalar_prefetch=1,
        grid=(bsz,),
        in_specs=[per_b(ql_t), per_b(qr_t), per_b(new_c), per_b(new_r),
                  pl.BlockSpec(w_uv.shape, lambda b, pt: (0, 0, 0)),
                  pl.BlockSpec(memory_space=pl.ANY), pl.BlockSpec(memory_space=pl.ANY)],
        out_specs=pl.BlockSpec((None, n_q, V_DIM), lambda b, pt: (b, 0, 0)),
        scratch_shapes=[pltpu.VMEM((PAGE_RING, keys, KV_LORA), cache_c.dtype),
                        pltpu.VMEM((PAGE_RING, PAGES_PER_STEP, QK_ROPE, page), cache_r_t.dtype),
                        pltpu.SemaphoreType.DMA((PAGE_RING,)),
                        pltpu.VMEM((groups * KV_LORA, LANES), BF16),
                        pltpu.VMEM((groups * QK_ROPE, LANES), BF16),
                        pltpu.VMEM((1, LANES), F32), pltpu.VMEM((1, LANES), F32),
                        pltpu.VMEM((LANES, KV_LORA), F32),
                        pltpu.VMEM((2, keys, KV_LORA), BF16),
                        pltpu.VMEM((2, keys // groups, groups * KV_LORA), BF16),
                        pltpu.VMEM((2, groups * QK_ROPE, keys // groups), BF16)],
    )
    return pl.pallas_call(
        functools.partial(_attn_sample_kernel, t_new=t_new, n_q=n_q, n_steps=n_steps),
        grid_spec=grid_spec,
        out_shape=jax.ShapeDtypeStruct((bsz, n_q, V_DIM), BF16),
        compiler_params=_params("arbitrary"),
        name="attn_sample",
    )(page_table, ql_t, qr_t, new_c, new_r, w_uv, cache_c, cache_r_t)


def _rot_cols(w):
    half = w.shape[-1] // 2
    return jnp.concatenate([-w[..., half:], w[..., :half]], axis=-1)


def _prep_weights(norm_g, w_ffn_gate, w_ffn_up, w_ffn_down, w_in, q_norm_g, w_q_up, kv_norm_g,
                  w_kv_up, hg_norm_g, w_out):
    bf = lambda a: a.astype(BF16)
    w_h, w_cq, w_ckv, w_kr = jnp.split(w_in, [4 * HG_W, 4 * HG_W + Q_LORA, 4 * HG_W + Q_LORA + KV_LORA], axis=-1)
    w_q_rope = w_q_up[..., QK_NOPE:]
    return {
        "norm_g": norm_g.reshape(-1, 1, D_MODEL),
        "wg": bf(w_ffn_gate), "wu": bf(w_ffn_up), "wd": bf(w_ffn_down),
        "w_h": bf(w_h), "w_cq": bf(w_cq), "w_ckv": bf(w_ckv),
        "w_kr": bf(jnp.concatenate([w_kr, _rot_cols(w_kr)], axis=-1)),
        "q_norm_g": q_norm_g.reshape(1, Q_LORA), "kv_norm_g": kv_norm_g.reshape(1, KV_LORA),
        "w_qn": bf(w_q_up[..., :QK_NOPE].reshape(Q_LORA, MLA_HEADS * QK_NOPE)),
        "w_qr": bf(w_q_rope.reshape(Q_LORA, MLA_HEADS * QK_ROPE)),
        "w_qrr": bf(_rot_cols(w_q_rope).reshape(Q_LORA, MLA_HEADS * QK_ROPE)),
        "w_uk": bf(jnp.transpose(w_kv_up[..., :QK_NOPE], (1, 2, 0))),
        "w_uv": bf(jnp.transpose(w_kv_up[..., QK_NOPE:], (1, 0, 2))),
        "hg_norm_g": hg_norm_g.reshape(1, HG_DV),
        "w_out1": bf(w_out[:HG_W]), "w_out2": bf(w_out[HG_W:]),
    }


def _rope_tables(pos):
    half = QK_ROPE // 2
    inv = ROPE_THETA ** (-jnp.arange(half, dtype=F32) / half)
    ang = pos.astype(F32)[:, None] * inv[None, :]
    cos = jnp.tile(jnp.cos(ang), (1, 2))
    sin = jnp.tile(jnp.sin(ang), (1, 2))
    return (jnp.concatenate([cos, sin], axis=-1), jnp.tile(cos, (1, MLA_HEADS)), jnp.tile(sin, (1, MLA_HEADS)))


def kernel(x_prompt, x_sample, cache_kv_latent, cache_k_rope, state_hgrn, page_table, hgrn_lb_logits,
           norm_g, w_ffn_gate, w_ffn_up, w_ffn_down, w_in, q_norm_g, w_q_up, kv_norm_g, w_kv_up,
           hg_norm_g, w_out):
    bp, tp, _ = x_prompt.shape
    bs, ts, _ = x_sample.shape
    depth = norm_g.shape[0]
    assert depth == 1
    past_len = page_table.shape[1] * cache_kv_latent.shape[2]
    w = _prep_weights(norm_g[0], w_ffn_gate[0], w_ffn_up[0], w_ffn_down[0], w_in[0], q_norm_g[0],
                      w_q_up[0], kv_norm_g[0], w_kv_up[0], hg_norm_g[0], w_out[0])
    ng = w["norm_g"]
    lb_logits = hgrn_lb_logits.astype(F32)

    def pre(x, tables):
        h1 = _ffn_block(x, ng[0], ng[1], w["wg"], w["wu"], w["wd"], 0)
        return (h1,) + tuple(_in_proj(h1, ng[2], *tables, w))

    def post(h1, o_h, o_m):
        return _ffn_block(h1, ng[4], ng[5], w["wg"], w["wu"], w["wd"], 1,
                          mix=(o_h, o_m, w["w_out1"], w["w_out2"], ng[3]))

    h1, zh, ckv, kr, ckv_b, kr_b, qlat, qrope = pre(
        x_prompt.reshape(bp * tp, D_MODEL), _rope_tables(jnp.arange(tp, dtype=jnp.int32)))
    o_h, st_p = _hgrn(zh.reshape(bp, tp, 4 * HG_W), lb_logits, w["hg_norm_g"], None, chunk=64, t_valid=None)
    o_m = _attn_prompt(qlat, qrope, ckv_b, kr_b, w["w_uv"], bp, tp)
    y_p = post(h1, o_h.reshape(bp * tp, HG_W), o_m).reshape(bp, tp, D_MODEL)
    ckv_p = ckv.reshape(1, bp, tp, KV_LORA)
    kr_p = kr.reshape(1, bp, tp, QK_ROPE)

    pos_s = past_len + jnp.arange(ts, dtype=jnp.int32)
    tabs = tuple(jnp.tile(a, (bs, 1)) for a in _rope_tables(pos_s))
    h1s, zhs, ckvs, krs, ckvs_b, krs_b, qlats, qropes = pre(x_sample.reshape(bs * ts, D_MODEL), tabs)
    t_pad = -(-ts // SUBLANES) * SUBLANES
    zhs_pad = jnp.pad(zhs.reshape(bs, ts, 4 * HG_W), ((0, 0), (0, t_pad - ts), (0, 0)))
    o_hs, st_s = _hgrn(zhs_pad, lb_logits, w["hg_norm_g"], state_hgrn[0], chunk=SUBLANES,
                       t_valid=ts if t_pad != ts else None)
    q_cols = lambda a, wd: a.reshape(bs, ts, MLA_HEADS, wd).transpose(0, 3, 2, 1).reshape(bs, wd, MLA_HEADS * ts)
    pad_new = lambda a, wd: jnp.pad(a.reshape(bs, ts, wd), ((0, 0), (0, NEW_KEY_ROWS - ts), (0, 0)))
    o_ms = _attn_sample(q_cols(qlats, KV_LORA), q_cols(qropes, QK_ROPE),
                        pad_new(ckvs_b, KV_LORA), pad_new(krs_b, QK_ROPE), w["w_uv"],
                        cache_kv_latent[0], jnp.swapaxes(cache_k_rope[0], 1, 2), page_table, ts)
    o_ms = o_ms.reshape(bs, MLA_HEADS, ts, V_DIM).transpose(0, 2, 1, 3).reshape(bs * ts, MLA_HEADS * V_DIM)
    y_s = post(h1s, o_hs[:, :ts].reshape(bs * ts, HG_W), o_ms).reshape(bs, ts, D_MODEL)

    return (y_p, y_s, ckv_p, kr_p, st_p[None], ckvs.reshape(1, bs, ts, KV_LORA),
            krs.reshape(1, bs, ts, QK_ROPE), st_s[None])
```

```python
import functools
import math

import jax
import jax.numpy as jnp
from jax import lax
from jax.experimental import pallas as pl
from jax.experimental.pallas import tpu as pltpu

D_MODEL = 1024
D_FF = 2816
HG_HEADS = 4
HG_DK = 128
HG_DV = 128
MLA_HEADS = 4
QK_NOPE = 128
QK_ROPE = 64
V_DIM = 128
Q_LORA = 768
KV_LORA = 256
ROPE_THETA = 10000.0
EPS = 1e-6
HG_W = HG_HEADS * HG_DK

VMEM_LIMIT_BYTES = 56 * 1024 * 1024
SUBLANES = 8
NEG_BIG = -1e30
LOG2E = math.log2(math.e)

F32 = jnp.float32
BF16 = jnp.bfloat16


def _dot(a, b):
    return jnp.dot(a, b, preferred_element_type=F32)


def _dot_nt(a, b):
    return lax.dot_general(a, b, (((1,), (1,)), ((), ())), preferred_element_type=F32)


def _dot_tn(a, b):
    return lax.dot_general(a, b, (((0,), (0,)), ((), ())), preferred_element_type=F32)


def _rms(x, g):
    return x * lax.rsqrt(jnp.mean(x * x, axis=-1, keepdims=True) + EPS) * g


def _sigmoid(x):
    return 1.0 / (1.0 + jnp.exp(-x))


def _silu(x):
    return x * _sigmoid(x)


def _params(*sem):
    return pltpu.CompilerParams(dimension_semantics=sem, vmem_limit_bytes=VMEM_LIMIT_BYTES)


FFN_CHUNK = 256
FFN_SUB_ROWS = 512


def _ffn_kernel(*refs, mix_in):
    if mix_in:
        h_ref, oh_ref, om_ref, w1_ref, w2_ref, gmix_ref = refs[:6]
        refs = refs[6:]
    else:
        h_ref = refs[0]
        refs = refs[1:]
    gpre_ref, gpost_ref, wg_ref, wu_ref, wd_ref, o_ref = refs
    tm = h_ref.shape[0]
    sub = min(tm, FFN_SUB_ROWS)
    tiles = [slice(r * sub, (r + 1) * sub) for r in range(tm // sub)]
    if mix_in:
        mixed = [_dot(oh_ref[rows, :], w1_ref[...]) + _dot(om_ref[rows, :], w2_ref[...]) for rows in tiles]
        xs = [h_ref[rows, :] + _rms(m, gmix_ref[...]) for rows, m in zip(tiles, mixed)]
    else:
        xs = [h_ref[rows, :] for rows in tiles]
    xns = [_rms(x, gpre_ref[...]).astype(BF16) for x in xs]
    for rows, x, xn in zip(tiles, xs, xns):
        acc = None
        for f in range(D_FF // FFN_CHUNK):
            cols = slice(f * FFN_CHUNK, (f + 1) * FFN_CHUNK)
            a = _dot(xn, wg_ref[:, cols])
            b = _dot(xn, wu_ref[:, cols])
            part = _dot((_silu(a) * b).astype(BF16), wd_ref[cols, :])
            acc = part if acc is None else acc + part
        o_ref[rows, :] = x + 0.5 * _rms(acc, gpost_ref[...])


def _ffn_block(x, g_pre, g_post, wg, wu, wd, which, mix=None):
    m = x.shape[0]
    tm = min(m, 2 * FFN_SUB_ROWS)

    def const(a):
        if a.ndim == 2:
            return pl.BlockSpec(a.shape, lambda i: (0, 0), pipeline_mode=pl.Buffered(1))
        return pl.BlockSpec((None,) + a.shape[1:], lambda i: (which, 0, 0), pipeline_mode=pl.Buffered(1))

    row = lambda a: pl.BlockSpec((tm, a.shape[1]), lambda i: (i, 0))
    args, in_specs = [x], [row(x)]
    if mix is not None:
        o_h, o_m, w1, w2, g_mix = mix
        args += [o_h, o_m, w1, w2, g_mix]
        in_specs += [row(o_h), row(o_m), const(w1), const(w2), const(g_mix)]
    args += [g_pre, g_post, wg, wu, wd]
    in_specs += [const(g_pre), const(g_post), const(wg), const(wu), const(wd)]
    return pl.pallas_call(
        functools.partial(_ffn_kernel, mix_in=mix is not None),
        grid=(m // tm,),
        in_specs=in_specs,
        out_specs=row(x),
        out_shape=jax.ShapeDtypeStruct((m, D_MODEL), F32),
        compiler_params=_params("parallel"),
        name="ffn_mix_block" if mix is not None else "ffn_block",
    )(*args)


def _inproj_kernel(x_ref, g_ref, csk_ref, cosq_ref, sinq_ref, wh_ref, wcq_ref, wckv_ref, wkr_ref,
                   qg_ref, kvg_ref, wqn_ref, wqr_ref, wqrr_ref, wuk_ref,
                   zh_ref, ckv_ref, kr_ref, ckvb_ref, krb_ref, qlat_ref, qrope_ref):
    u = _rms(x_ref[...], g_ref[...]).astype(BF16)
    cq = _dot(u, wcq_ref[...])
    ckv = _dot(u, wckv_ref[...])
    kz = _dot(u, wkr_ref[...])
    half = wh_ref.shape[1] // 2
    zh_ref[:, :half] = _dot(u, wh_ref[:, :half])
    cqn = _rms(cq, qg_ref[...]).astype(BF16)
    qn = _dot(cqn, wqn_ref[...])
    qr_a = _dot(cqn, wqr_ref[...])
    qr_b = _dot(cqn, wqrr_ref[...])
    zh_ref[:, half:] = _dot(u, wh_ref[:, half:])
    qn = qn.astype(BF16)
    for h in range(MLA_HEADS):
        qlat_ref[:, h * KV_LORA:(h + 1) * KV_LORA] = _dot(
            qn[:, h * QK_NOPE:(h + 1) * QK_NOPE], wuk_ref[h]).astype(BF16)
    ckv = _rms(ckv, kvg_ref[...])
    ckv_ref[...] = ckv
    ckvb_ref[...] = ckv.astype(BF16)
    csk = csk_ref[...]
    kr = kz[:, :QK_ROPE] * csk[:, :QK_ROPE] + kz[:, QK_ROPE:] * csk[:, QK_ROPE:]
    kr_ref[...] = kr
    krb_ref[...] = kr.astype(BF16)
    qrope_ref[...] = (qr_a * cosq_ref[...] + qr_b * sinq_ref[...]).astype(BF16)


def _in_proj(x, g, csk, cosq, sinq, w):
    m = x.shape[0]
    tm = min(m, 512)
    nrep = csk.shape[0] // tm
    full = lambda a: pl.BlockSpec(a.shape, lambda i: (0,) * a.ndim, pipeline_mode=pl.Buffered(1))
    row = lambda width: pl.BlockSpec((tm, width), lambda i: (i, 0))
    tab = lambda width: pl.BlockSpec((tm, width), lambda i: (i % nrep, 0))
    rope_w = MLA_HEADS * QK_ROPE
    return pl.pallas_call(
        _inproj_kernel,
        grid=(m // tm,),
        in_specs=[row(D_MODEL), full(g), tab(2 * QK_ROPE), tab(rope_w), tab(rope_w),
                  full(w["w_h"]), full(w["w_cq"]), full(w["w_ckv"]), full(w["w_kr"]),
                  full(w["q_norm_g"]), full(w["kv_norm_g"]), full(w["w_qn"]), full(w["w_qr"]),
                  full(w["w_qrr"]), full(w["w_uk"])],
        out_specs=[row(4 * HG_W), row(KV_LORA), row(QK_ROPE), row(KV_LORA), row(QK_ROPE),
                   row(MLA_HEADS * KV_LORA), row(rope_w)],
        out_shape=[
            jax.ShapeDtypeStruct((m, 4 * HG_W), F32),
            jax.ShapeDtypeStruct((m, KV_LORA), F32),
            jax.ShapeDtypeStruct((m, QK_ROPE), F32),
            jax.ShapeDtypeStruct((m, KV_LORA), BF16),
            jax.ShapeDtypeStruct((m, QK_ROPE), BF16),
            jax.ShapeDtypeStruct((m, MLA_HEADS * KV_LORA), BF16),
            jax.ShapeDtypeStruct((m, rope_w), BF16),
        ],
        compiler_params=_params("parallel"),
        name="in_proj",
    )(x, g, csk, cosq, sinq, w["w_h"], w["w_cq"], w["w_ckv"], w["w_kr"], w["q_norm_g"],
      w["kv_norm_g"], w["w_qn"], w["w_qr"], w["w_qrr"], w["w_uk"])


def _split3(g):
    g1 = g.astype(BF16).astype(F32)
    r1 = g - g1
    g2 = r1.astype(BF16).astype(F32)
    g3 = (r1 - g2).astype(BF16).astype(F32)
    return g1, g2, g3


HG_SUB = 16
HGRN_ROWS_PER_STEP = 32


def _hgrn_kernel(*refs, chunk, t_valid, has_s0):
    if has_s0:
        hq_ref, hf_ref, hi_ref, hg_ref, lbl_ref, ng_ref, s0_ref = refs[:7]
        refs = refs[7:]
    else:
        hq_ref, hf_ref, hi_ref, hg_ref, lbl_ref, ng_ref = refs[:6]
        s0_ref = None
        refs = refs[6:]
    o_ref, sfin_ref, st_ref, c_ref, v_ref = refs
    ti = pl.program_id(1)
    n_batch, t_tile = hq_ref.shape[:2]
    units = [divmod(u, HG_HEADS) for u in range(n_batch * HG_HEADS)]
    n_chunks = t_tile // chunk
    n_tiles = chunk // SUBLANES

    lbl = lbl_ref[...]
    e = jnp.exp(lbl - jnp.max(lbl, axis=0, keepdims=True))
    lb_all = e[0:1, :] / jnp.sum(e, axis=0, keepdims=True)

    @pl.when(ti == 0)
    def _():
        for u, (e, h) in enumerate(units):
            st_ref[u] = s0_ref[e, h].T if has_s0 else jnp.zeros((HG_DV, HG_DK), F32)

    r_i = lax.broadcasted_iota(jnp.int32, (chunk, chunk), 0)
    c_i = lax.broadcasted_iota(jnp.int32, (chunk, chunk), 1)
    tril = (c_i <= r_i).astype(F32)
    row8 = lax.broadcasted_iota(jnp.int32, (SUBLANES, HG_DK), 0)
    ng = ng_ref[...]
    ones = jnp.ones((HG_DK, HG_DV), BF16)
    sub = min(chunk, HG_SUB)
    n_sub = chunk // sub
    pairs = [(s, i) for s in range(chunk) for i in range(s // SUBLANES, (s // sub + 1) * sub // SUBLANES)]
    if n_sub > 1:
        assert sub & (sub - 1) == 0 and chunk & (chunk - 1) == 0
        rows_sub = lax.broadcasted_iota(jnp.int32, (chunk, (n_sub - 1) * chunk), 0) >> (sub.bit_length() - 1)
        cols_blk = lax.broadcasted_iota(jnp.int32, (chunk, (n_sub - 1) * chunk), 1) >> (chunk.bit_length() - 1)
        sub_mask = cols_blk == rows_sub - 1

    def gates(u, start):
        e, h = units[u]
        sl = pl.ds(start, chunk)
        hl = slice(h * HG_DK, (h + 1) * HG_DK)
        lb = lb_all[:, hl]
        q = _silu(hq_ref[e, sl, hl])
        f = lb + (1.0 - lb) * _sigmoid(hf_ref[e, sl, hl])
        g = jnp.log(f)
        k = 1.0 - f
        v = hi_ref[e, sl, hl]
        if t_valid is not None:
            pos = lax.broadcasted_iota(jnp.int32, (chunk, HG_DK), 0) + (ti * t_tile + start)
            valid = pos < t_valid
            g = jnp.where(valid, g, 0.0)
            k = jnp.where(valid, k, 0.0)
            v = jnp.where(valid, v, 0.0)
        g1, g2, g3 = _split3(g)
        b = _dot(tril, g1) + _dot(tril, g2) + _dot(tril, g3)
        return q, k, v, b * LOG2E

    def products(h, q, k, v, b2):
        c_ref[h] = b2 - jnp.log(k) * LOG2E
        v_ref[h] = v
        q_t = [q[i * SUBLANES:(i + 1) * SUBLANES, :] for i in range(n_tiles)]
        b_t = [b2[i * SUBLANES:(i + 1) * SUBLANES, :] for i in range(n_tiles)]
        p_tiles = []
        for s, i in pairs:
            d = b_t[i] - c_ref[h, s:s + 1, :]
            if i == s // SUBLANES and s % SUBLANES:
                d = jnp.where(row8 >= s % SUBLANES, d, NEG_BIG)
            p_tiles.append(q_t[i] * jnp.exp2(d))
        a_rep = _dot(jnp.concatenate(p_tiles, axis=0).astype(BF16), ones)

        a_sub = None
        if n_sub > 1:
            zeros_sub = jnp.zeros((sub, HG_DK), F32)
            q_parts, k_parts = [zeros_sub], []
            for j in range(1, n_sub):
                lo = j * sub
                r_j = b2[lo - 1:lo, :]
                q_parts.append(q[lo:lo + sub, :] * jnp.exp2(b2[lo:lo + sub, :] - r_j))
                k_parts += [k[:lo, :] * jnp.exp2(r_j - b2[:lo, :])] + [zeros_sub] * (n_sub - j)
            a_sub = _dot_nt(jnp.concatenate(q_parts, axis=0).astype(BF16),
                            jnp.concatenate(k_parts, axis=0).astype(BF16))

        st = st_ref[h]
        o_state = _dot_nt((q * jnp.exp2(b2)).astype(BF16), st.astype(BF16))
        bl = b2[chunk - 1:chunk, :]
        v_b = v.astype(BF16)
        st_ref[h] = st * jnp.exp2(bl) + _dot_tn(v_b, (k * jnp.exp2(bl - b2)).astype(BF16))
        return a_rep, a_sub, o_state, v_b

    def outputs(u, start, a_rep, a_sub, o_state, v_b):
        e, h = units[u]
        sl = pl.ds(start, chunk)
        hl = slice(h * HG_DK, (h + 1) * HG_DK)
        o_t = [o_state[i * SUBLANES:(i + 1) * SUBLANES, :] for i in range(n_tiles)]
        for j, (s, i) in enumerate(pairs):
            o_t[i] = o_t[i] + a_rep[j * SUBLANES:(j + 1) * SUBLANES, :] * v_ref[u, s:s + 1, :]
        o = jnp.concatenate(o_t, axis=0) if n_tiles > 1 else o_t[0]
        if a_sub is not None:
            o = o + _dot(jnp.where(sub_mask, a_sub, 0.0).astype(BF16),
                         jnp.concatenate([v_b] * (n_sub - 1), axis=0))
        o_ref[e, sl, hl] = (_rms(o, ng) * _silu(hg_ref[e, sl, hl])).astype(o_ref.dtype)

    def chunk_body(c, carry):
        start = pl.multiple_of(c * chunk, chunk)
        gated = [gates(u, start) for u in range(len(units))]
        prods = [products(u, *gated[u]) for u in range(len(units))]
        for u in range(len(units)):
            outputs(u, start, *prods[u])
        return carry

    lax.fori_loop(0, n_chunks, chunk_body, 0, unroll=4 if n_chunks % 4 == 0 else 1)

    @pl.when(ti == pl.num_programs(1) - 1)
    def _():
        for u, (e, h) in enumerate(units):
            sfin_ref[e, h] = st_ref[u].T


def _hgrn(zh, lb_logits, ng, s0, chunk, t_valid):
    bsz, t, _ = zh.shape
    has_s0 = s0 is not None
    t_tile = min(t, 512)
    assert t % t_tile == 0 and t_tile % chunk == 0
    n_batch = math.gcd(bsz, max(1, HGRN_ROWS_PER_STEP // t_tile))
    n_units = n_batch * HG_HEADS
    col = lambda j: pl.BlockSpec((n_batch, t_tile, HG_W), lambda b, i: (b, i, j))
    in_specs = [col(0), col(1), col(2), col(3),
                pl.BlockSpec(lb_logits.shape, lambda b, i: (0, 0)),
                pl.BlockSpec((1, HG_DV), lambda b, i: (0, 0))]
    args = [zh, zh, zh, zh, lb_logits, ng]
    st_spec = pl.BlockSpec((n_batch, HG_HEADS, HG_DK, HG_DV), lambda b, i: (b, 0, 0, 0))
    if has_s0:
        in_specs.append(st_spec)
        args.append(s0)
    return pl.pallas_call(
        functools.partial(_hgrn_kernel, chunk=chunk, t_valid=t_valid, has_s0=has_s0),
        grid=(bsz // n_batch, t // t_tile),
        in_specs=in_specs,
        out_specs=[pl.BlockSpec((n_batch, t_tile, HG_W), lambda b, i: (b, i, 0)), st_spec],
        out_shape=[jax.ShapeDtypeStruct((bsz, t, HG_W), BF16),
                   jax.ShapeDtypeStruct((bsz, HG_HEADS, HG_DK, HG_DV), F32)],
        scratch_shapes=[pltpu.VMEM((n_units, HG_DV, HG_DK), F32),
                        pltpu.VMEM((n_units, chunk, HG_DK), F32),
                        pltpu.VMEM((n_units, chunk, HG_DV), F32)],
        compiler_params=_params("parallel", "arbitrary"),
        name="hgrn",
    )(*args)


ATTN_SCALE = 1.0 / math.sqrt(QK_NOPE + QK_ROPE)
ATTN_EXP2_SCALE = ATTN_SCALE * LOG2E

LANES = 128


def _attn_prompt_kernel(ql_ref, qr_ref, kc_ref, kr_ref, wuv_ref, o_ref,
                        m_ref, l_ref, a_ref, acc_ref, s_ref, p_ref, *, tq):
    qi = pl.program_id(1)
    ql = jnp.concatenate([ql_ref[:, h * KV_LORA:(h + 1) * KV_LORA] for h in range(MLA_HEADS)], axis=0)
    qr = jnp.concatenate([qr_ref[:, h * QK_ROPE:(h + 1) * QK_ROPE] for h in range(MLA_HEADS)], axis=0)
    m_ref[...] = jnp.full_like(m_ref, NEG_BIG)
    l_ref[...] = jnp.zeros_like(l_ref)
    a_ref[...] = jnp.zeros_like(a_ref)
    acc_ref[...] = jnp.zeros_like(acc_ref)
    p_ref[1] = jnp.zeros(p_ref.shape[1:], BF16)

    def keys(kb):
        return pl.ds(pl.multiple_of(kb * tq, tq), tq)

    def scores(kb):
        return (_dot_nt(ql, kc_ref[keys(kb), :]) + _dot_nt(qr, kr_ref[keys(kb), :])) * ATTN_EXP2_SCALE

    def add_values(kb, slot):
        acc_ref[...] = (jnp.tile(a_ref[...], (1, KV_LORA // LANES)) * acc_ref[...]
                        + _dot(p_ref[slot], kc_ref[keys(kb), :]))

    def softmax(slot, masked):
        s = s_ref[slot]
        if masked:
            rows = lax.broadcasted_iota(jnp.int32, s.shape, 0) & (tq - 1)
            cols = lax.broadcasted_iota(jnp.int32, s.shape, 1)
            s = jnp.where(cols <= rows, s, NEG_BIG)
        m_prev = m_ref[...]
        m_new = jnp.maximum(m_prev, jnp.max(s, axis=-1, keepdims=True))
        alpha = jnp.exp2(m_prev - m_new)
        p = jnp.exp2(s - jnp.tile(m_new, (1, tq // LANES)))
        l_ref[...] = alpha * l_ref[...] + jnp.sum(p, axis=-1, keepdims=True)
        m_ref[...] = m_new
        a_ref[...] = alpha
        p_ref[slot] = p.astype(BF16)

    s_ref[0] = scores(0)

    def iteration(i, cur):
        s_ref[1 - cur] = scores(i + 1)
        add_values(jnp.maximum(i - 1, 0), 1 - cur)
        softmax(cur, False)

    def last(cur):
        add_values(jnp.maximum(qi - 1, 0), 1 - cur)
        softmax(cur, True)
        add_values(qi, cur)

    def pair(j, carry):
        iteration(2 * j, 0)
        iteration(2 * j + 1, 1)
        return carry

    lax.fori_loop(0, qi >> 1, pair, 0)

    @pl.when((qi & 1) == 1)
    def _():
        iteration(qi - 1, 0)
        last(1)

    @pl.when((qi & 1) == 0)
    def _():
        last(0)

    o_lat =(acc_ref[...] / jnp.tile(l_ref[...], (1, KV_LORA // LANES))).astype(BF16)
    for h in range(MLA_HEADS):
        o_ref[:, h * V_DIM:(h + 1) * V_DIM] = _dot(o_lat[h * tq:(h + 1) * tq, :], wuv_ref[h]).astype(o_ref.dtype)


def _attn_prompt(qlat, qrope, ckv_b, kr_b, w_uv, bsz, t):
    tq = 256
    assert tq & (tq - 1) == 0 and t % tq == 0
    nq = t // tq
    rows = MLA_HEADS * tq
    return pl.pallas_call(
        functools.partial(_attn_prompt_kernel, tq=tq),
        grid=(bsz, nq),
        in_specs=[
            pl.BlockSpec((tq, MLA_HEADS * KV_LORA), lambda b, i: (b * nq + i, 0)),
            pl.BlockSpec((tq, MLA_HEADS * QK_ROPE), lambda b, i: (b * nq + i, 0)),
            pl.BlockSpec((t, KV_LORA), lambda b, i: (b, 0)),
            pl.BlockSpec((t, QK_ROPE), lambda b, i: (b, 0)),
            pl.BlockSpec(w_uv.shape, lambda b, i: (0, 0, 0)),
        ],
        out_specs=pl.BlockSpec((tq, MLA_HEADS * V_DIM), lambda b, i: (b * nq + i, 0)),
        out_shape=jax.ShapeDtypeStruct((bsz * t, MLA_HEADS * V_DIM), BF16),
        scratch_shapes=[pltpu.VMEM((rows, LANES), F32), pltpu.VMEM((rows, LANES), F32),
                        pltpu.VMEM((rows, LANES), F32), pltpu.VMEM((rows, KV_LORA), F32),
                        pltpu.VMEM((2, rows, tq), F32), pltpu.VMEM((2, rows, tq), BF16)],
        compiler_params=_params("parallel", "parallel"),
        name="attn_prompt",
    )(qlat, qrope, ckv_b, kr_b, w_uv)


PAGES_PER_STEP = 16
PAGE_RING = 4
NEW_KEY_ROWS = 128


def _attn_sample_kernel(pt_ref, ql_ref, qr_ref, nc_ref, nr_ref, wuv_ref, cc_hbm, cr_hbm, o_ref,
                        cbuf, rbuf, sem, wq_ref, wr_ref, m_ref, l_ref, acc_ref, kv_ref, kq_ref, kr_ref,
                        *, t_new, n_q, n_steps):
    b = pl.program_id(0)
    nb = pl.num_programs(0)
    groups = LANES // n_q
    page = cbuf.shape[1] // PAGES_PER_STEP
    ppg = PAGES_PER_STEP // groups
    lat_rep = KV_LORA // LANES

    def page_copies(bb, step, slot):
        copies = []
        for i in range(PAGES_PER_STEP):
            pid = pt_ref[bb, step * PAGES_PER_STEP + i]
            copies.append(pltpu.make_async_copy(cc_hbm.at[pid], cbuf.at[slot, pl.ds(i * page, page), :],
                                                sem.at[slot]))
            copies.append(pltpu.make_async_copy(cr_hbm.at[pid], rbuf.at[slot, i], sem.at[slot]))
        return copies

    @pl.when(b == 0)
    def _():
        for step in range(PAGE_RING):
            for c in page_copies(0, step, step):
                c.start()

    q_shift = n_q.bit_length() - 1
    spread = (lax.broadcasted_iota(jnp.int32, (n_q, LANES), 1) & (n_q - 1)) == lax.broadcasted_iota(
        jnp.int32, (n_q, LANES), 0)
    spread = jnp.where(spread, 1.0, 0.0).astype(BF16)
    for src, dst in ((ql_ref, wq_ref), (qr_ref, wr_ref)):
        feat = src.shape[0]
        tiled = _dot(src[...], spread)
        lane_g = lax.broadcasted_iota(jnp.int32, tiled.shape, 1) >> q_shift
        for g in range(groups):
            dst[g * feat:(g + 1) * feat, :] = jnp.where(lane_g == g, tiled, 0.0).astype(BF16)

    m_ref[...] = jnp.full_like(m_ref, NEG_BIG)
    l_ref[...] = jnp.zeros_like(l_ref)
    acc_ref[...] = jnp.zeros_like(acc_ref)

    def lanes_to_rows(row):
        return jnp.broadcast_to(row, (LANES, LANES)).T

    def update(s, place, values):
        m_prev = m_ref[...]
        m_new = jnp.maximum(m_prev, jnp.max(s, axis=0, keepdims=True))
        alpha = jnp.exp2(m_prev - m_new)
        p = jnp.exp2(s - m_new)
        l_ref[...] = alpha * l_ref[...] + jnp.sum(p, axis=0, keepdims=True)
        m_ref[...] = m_new
        pv = _dot(place(p.T), values)
        acc_ref[...] = jnp.tile(lanes_to_rows(alpha), (1, lat_rep)) * acc_ref[...] + pv

    def block_diag(p_t):
        row_g = lax.broadcasted_iota(jnp.int32, p_t.shape, 0) >> q_shift
        return jnp.concatenate([jnp.where(row_g == g, p_t, 0.0).astype(BF16) for g in range(groups)], axis=1)

    def stage_and_score(j):
        ring = j % PAGE_RING
        slot = j & 1
        for c in page_copies(b, j, ring):
            c.wait()
        for i in range(PAGES_PER_STEP):
            g, r = divmod(i, ppg)
            x = cbuf[ring, i * page:(i + 1) * page, :].astype(BF16)
            kv_ref[slot, i * page:(i + 1) * page, :] = x
            kq_ref[slot, r * page:(r + 1) * page, g * KV_LORA:(g + 1) * KV_LORA] = x
            kr_ref[slot, g * QK_ROPE:(g + 1) * QK_ROPE, r * page:(r + 1) * page] = rbuf[ring, i].astype(BF16)
        if j + PAGE_RING < n_steps:
            nxt = page_copies(b, j + PAGE_RING, ring)
        else:
            nxt = page_copies(jnp.minimum(b + 1, nb - 1), j + PAGE_RING - n_steps, ring)
        for c in nxt:
            c.start()
        return (_dot(kq_ref[slot], wq_ref[...]) + _dot_tn(kr_ref[slot], wr_ref[...])) * ATTN_EXP2_SCALE

    s_cur = stage_and_score(0)
    for j in range(n_steps):
        s_next = stage_and_score(j + 1) if j + 1 < n_steps else None
        update(s_cur, block_diag, kv_ref[j & 1])
        s_cur = s_next

    @pl.when(b == nb - 1)
    def _():
        for step in range(PAGE_RING):
            for c in page_copies(b, step, step):
                c.wait()

    nc = nc_ref[...]
    s2 = (_dot(nc, wq_ref[:KV_LORA, :]) + _dot(nr_ref[...], wr_ref[:QK_ROPE, :])) * ATTN_EXP2_SCALE
    t_k = lax.broadcasted_iota(jnp.int32, s2.shape, 0)
    lane = lax.broadcasted_iota(jnp.int32, s2.shape, 1)
    ok = (lane < n_q) & (t_k <= (lane & (t_new - 1)))
    update(jnp.where(ok, s2, NEG_BIG), lambda p_t: p_t.astype(BF16), nc)

    grp = lambda a, g: a[g * n_q:(g + 1) * n_q]
    total = lambda a: functools.reduce(lambda x, y: x + y, [grp(a, g) for g in range(groups)])
    m_t = lanes_to_rows(m_ref[...])
    l_t = lanes_to_rows(l_ref[...])
    m_q = functools.reduce(jnp.maximum, [grp(m_t, g) for g in range(groups)])
    w = jnp.exp2(m_t - jnp.tile(m_q, (groups, 1)))
    l_q = total(w * l_t)
    o_lat = total(jnp.tile(w, (1, lat_rep)) * acc_ref[...]) / jnp.tile(l_q, (1, lat_rep))
    o_lat = o_lat.astype(BF16)
    head = lax.broadcasted_iota(jnp.int32, (n_q, V_DIM), 0) >> (t_new.bit_length() - 1)
    out = jnp.zeros((n_q, V_DIM), F32)
    for h in range(MLA_HEADS):
        out = out + jnp.where(head == h, _dot(o_lat, wuv_ref[h]), 0.0)
    o_ref[...] = out.astype(o_ref.dtype)


def _attn_sample(ql_t, qr_t, new_c, new_r, w_uv, cache_c, cache_r_t, page_table, t_new):
    bsz, _, n_q = ql_t.shape
    n_pages = page_table.shape[1]
    groups = LANES // n_q
    assert LANES % n_q == 0 and PAGES_PER_STEP % groups == 0 and n_pages % PAGES_PER_STEP == 0
    assert t_new & (t_new - 1) == 0 and n_q & (n_q - 1) == 0 and t_new <= NEW_KEY_ROWS and n_q >= SUBLANES
    page = cache_c.shape[1]
    keys = PAGES_PER_STEP * page
    n_steps = n_pages // PAGES_PER_STEP
    assert n_steps % PAGE_RING == 0

    per_b = lambda a: pl.BlockSpec((None,) + a.shape[1:], lambda b, pt: (b, 0, 0))
    grid_spec = pltpu.PrefetchScalarGridSpec(
        num_scalar_prefetch=1,
        grid=(bsz,),
        in_specs=[per_b(ql_t), per_b(qr_t), per_b(new_c), per_b(new_r),
                  pl.BlockSpec(w_uv.shape, lambda b, pt: (0, 0, 0)),
                  pl.BlockSpec(memory_space=pl.ANY), pl.BlockSpec(memory_space=pl.ANY)],
        out_specs=pl.BlockSpec((None, n_q, V_DIM), lambda b, pt: (b, 0, 0)),
        scratch_shapes=[pltpu.VMEM((PAGE_RING, keys, KV_LORA), cache_c.dtype),
                        pltpu.VMEM((PAGE_RING, PAGES_PER_STEP, QK_ROPE, page), cache_r_t.dtype),
                        pltpu.SemaphoreType.DMA((PAGE_RING,)),
                        pltpu.VMEM((groups * KV_LORA, LANES), BF16),
                        pltpu.VMEM((groups * QK_ROPE, LANES), BF16),
                        pltpu.VMEM((1, LANES), F32), pltpu.VMEM((1, LANES), F32),
                        pltpu.VMEM((LANES, KV_LORA), F32),
                        pltpu.VMEM((2, keys, KV_LORA), BF16),
                        pltpu.VMEM((2, keys // groups, groups * KV_LORA), BF16),
                        pltpu.VMEM((2, groups * QK_ROPE, keys // groups), BF16)],
    )
    return pl.pallas_call(
        functools.partial(_attn_sample_kernel, t_new=t_new, n_q=n_q, n_steps=n_steps),
        grid_spec=grid_spec,
        out_shape=jax.ShapeDtypeStruct((bsz, n_q, V_DIM), BF16),
        compiler_params=_params("arbitrary"),
        name="attn_sample",
    )(page_table, ql_t, qr_t, new_c, new_r, w_uv, cache_c, cache_r_t)


def _rot_cols(w):
    half = w.shape[-1] // 2
    return jnp.concatenate([-w[..., half:], w[..., :half]], axis=-1)


def _prep_weights(norm_g, w_ffn_gate, w_ffn_up, w_ffn_down, w_in, q_norm_g, w_q_up, kv_norm_g,
                  w_kv_up, hg_norm_g, w_out):
    bf = lambda a: a.astype(BF16)
    w_h, w_cq, w_ckv, w_kr = jnp.split(w_in, [4 * HG_W, 4 * HG_W + Q_LORA, 4 * HG_W + Q_LORA + KV_LORA], axis=-1)
    w_q_rope = w_q_up[..., QK_NOPE:]
    return {
        "norm_g": norm_g.reshape(-1, 1, D_MODEL),
        "wg": bf(w_ffn_gate), "wu": bf(w_ffn_up), "wd": bf(w_ffn_down),
        "w_h": bf(w_h), "w_cq": bf(w_cq), "w_ckv": bf(w_ckv),
        "w_kr": bf(jnp.concatenate([w_kr, _rot_cols(w_kr)], axis=-1)),
        "q_norm_g": q_norm_g.reshape(1, Q_LORA), "kv_norm_g": kv_norm_g.reshape(1, KV_LORA),
        "w_qn": bf(w_q_up[..., :QK_NOPE].reshape(Q_LORA, MLA_HEADS * QK_NOPE)),
        "w_qr": bf(w_q_rope.reshape(Q_LORA, MLA_HEADS * QK_ROPE)),
        "w_qrr": bf(_rot_cols(w_q_rope).reshape(Q_LORA, MLA_HEADS * QK_ROPE)),
        "w_uk": bf(jnp.transpose(w_kv_up[..., :QK_NOPE], (1, 2, 0))),
        "w_uv": bf(jnp.transpose(w_kv_up[..., QK_NOPE:], (1, 0, 2))),
        "hg_norm_g": hg_norm_g.reshape(1, HG_DV),
        "w_out1": bf(w_out[:HG_W]), "w_out2": bf(w_out[HG_W:]),
    }


def _rope_tables(pos):
    half = QK_ROPE // 2
    inv = ROPE_THETA ** (-jnp.arange(half, dtype=F32) / half)
    ang = pos.astype(F32)[:, None] * inv[None, :]
    cos = jnp.tile(jnp.cos(ang), (1, 2))
    sin = jnp.tile(jnp.sin(ang), (1, 2))
    return (jnp.concatenate([cos, sin], axis=-1), jnp.tile(cos, (1, MLA_HEADS)), jnp.tile(sin, (1, MLA_HEADS)))


def kernel(x_prompt, x_sample, cache_kv_latent, cache_k_rope, state_hgrn, page_table, hgrn_lb_logits,
           norm_g, w_ffn_gate, w_ffn_up, w_ffn_down, w_in, q_norm_g, w_q_up, kv_norm_g, w_kv_up,
           hg_norm_g, w_out):
    bp, tp, _ = x_prompt.shape
    bs, ts, _ = x_sample.shape
    depth = norm_g.shape[0]
    assert depth == 1
    past_len = page_table.shape[1] * cache_kv_latent.shape[2]
    w = _prep_weights(norm_g[0], w_ffn_gate[0], w_ffn_up[0], w_ffn_down[0], w_in[0], q_norm_g[0],
                      w_q_up[0], kv_norm_g[0], w_kv_up[0], hg_norm_g[0], w_out[0])
    ng = w["norm_g"]
    lb_logits = hgrn_lb_logits.astype(F32)

    def pre(x, tables):
        h1 = _ffn_block(x, ng[0], ng[1], w["wg"], w["wu"], w["wd"], 0)
        return (h1,) + tuple(_in_proj(h1, ng[2], *tables, w))

    def post(h1, o_h, o_m):
        return _ffn_block(h1, ng[4], ng[5], w["wg"], w["wu"], w["wd"], 1,
                          mix=(o_h, o_m, w["w_out1"], w["w_out2"], ng[3]))

    h1, zh, ckv, kr, ckv_b, kr_b, qlat, qrope = pre(
        x_prompt.reshape(bp * tp, D_MODEL), _rope_tables(jnp.arange(tp, dtype=jnp.int32)))
    o_h, st_p = _hgrn(zh.reshape(bp, tp, 4 * HG_W), lb_logits, w["hg_norm_g"], None, chunk=64, t_valid=None)
    o_m = _attn_prompt(qlat, qrope, ckv_b, kr_b, w["w_uv"], bp, tp)
    y_p = post(h1, o_h.reshape(bp * tp, HG_W), o_m).reshape(bp, tp, D_MODEL)
    ckv_p = ckv.reshape(1, bp, tp, KV_LORA)
    kr_p = kr.reshape(1, bp, tp, QK_ROPE)

    pos_s = past_len + jnp.arange(ts, dtype=jnp.int32)
    tabs = tuple(jnp.tile(a, (bs, 1)) for a in _rope_tables(pos_s))
    h1s, zhs, ckvs, krs, ckvs_b, krs_b, qlats, qropes = pre(x_sample.reshape(bs * ts, D_MODEL), tabs)
    t_pad = -(-ts // SUBLANES) * SUBLANES
    zhs_pad = jnp.pad(zhs.reshape(bs, ts, 4 * HG_W), ((0, 0), (0, t_pad - ts), (0, 0)))
    o_hs, st_s = _hgrn(zhs_pad, lb_logits, w["hg_norm_g"], state_hgrn[0], chunk=SUBLANES,
                       t_valid=ts if t_pad != ts else None)
    q_cols = lambda a, wd: a.reshape(bs, ts, MLA_HEADS, wd).transpose(0, 3, 2, 1).reshape(bs, wd, MLA_HEADS * ts)
    pad_new = lambda a, wd: jnp.pad(a.reshape(bs, ts, wd), ((0, 0), (0, NEW_KEY_ROWS - ts), (0, 0)))
    o_ms = _attn_sample(q_cols(qlats, KV_LORA), q_cols(qropes, QK_ROPE),
                        pad_new(ckvs_b, KV_LORA), pad_new(krs_b, QK_ROPE), w["w_uv"],
                        cache_kv_latent[0], jnp.swapaxes(cache_k_rope[0], 1, 2), page_table, ts)
    o_ms = o_ms.reshape(bs, MLA_HEADS, ts, V_DIM).transpose(0, 2, 1, 3).reshape(bs * ts, MLA_HEADS * V_DIM)
    y_s = post(h1s, o_hs[:, :ts].reshape(bs * ts, HG_W), o_ms).reshape(bs, ts, D_MODEL)

    return (y_p, y_s, ckv_p, kr_p, st_p[None], ckvs.reshape(1, bs, ts, KV_LORA),
            krs.reshape(1, bs, ts, QK_ROPE), st_s[None])
```

```python
import functools
import math

import jax
import jax.numpy as jnp
from jax import lax
from jax.experimental import pallas as pl
from jax.experimental.pallas import tpu as pltpu

D_MODEL = 1024
D_FF = 2816
HG_HEADS = 4
HG_DK = 128
HG_DV = 128
MLA_HEADS = 4
QK_NOPE = 128
QK_ROPE = 64
V_DIM = 128
Q_LORA = 768
KV_LORA = 256
ROPE_THETA = 10000.0
EPS = 1e-6
HG_W = HG_HEADS * HG_DK

VMEM_LIMIT_BYTES = 56 * 1024 * 1024
SUBLANES = 8
NEG_BIG = -1e30
LOG2E = math.log2(math.e)

F32 = jnp.float32
BF16 = jnp.bfloat16


def _dot(a, b):
    return jnp.dot(a, b, preferred_element_type=F32)


def _dot_nt(a, b):
    return lax.dot_general(a, b, (((1,), (1,)), ((), ())), preferred_element_type=F32)


def _dot_tn(a, b):
    return lax.dot_general(a, b, (((0,), (0,)), ((), ())), preferred_element_type=F32)


def _rms(x, g):
    return x * lax.rsqrt(jnp.mean(x * x, axis=-1, keepdims=True) + EPS) * g


def _sigmoid(x):
    return 1.0 / (1.0 + jnp.exp(-x))


def _silu(x):
    return x * _sigmoid(x)


def _params(*sem):
    return pltpu.CompilerParams(dimension_semantics=sem, vmem_limit_bytes=VMEM_LIMIT_BYTES)


FFN_CHUNK = 256
FFN_SUB_ROWS = 512


def _ffn_kernel(*refs, mix_in):
    if mix_in:
        h_ref, oh_ref, om_ref, w1_ref, w2_ref, gmix_ref = refs[:6]
        refs = refs[6:]
    else:
        h_ref = refs[0]
        refs = refs[1:]
    gpre_ref, gpost_ref, wg_ref, wu_ref, wd_ref, o_ref = refs
    tm = h_ref.shape[0]
    sub = min(tm, FFN_SUB_ROWS)
    tiles = [slice(r * sub, (r + 1) * sub) for r in range(tm // sub)]
    if mix_in:
        mixed = [_dot(oh_ref[rows, :], w1_ref[...]) + _dot(om_ref[rows, :], w2_ref[...]) for rows in tiles]
        xs = [h_ref[rows, :] + _rms(m, gmix_ref[...]) for rows, m in zip(tiles, mixed)]
    else:
        xs = [h_ref[rows, :] for rows in tiles]
    xns = [_rms(x, gpre_ref[...]).astype(BF16) for x in xs]
    for rows, x, xn in zip(tiles, xs, xns):
        acc = None
        for f in range(D_FF // FFN_CHUNK):
            cols = slice(f * FFN_CHUNK, (f + 1) * FFN_CHUNK)
            a = _dot(xn, wg_ref[:, cols])
            b = _dot(xn, wu_ref[:, cols])
            part = _dot((_silu(a) * b).astype(BF16), wd_ref[cols, :])
            acc = part if acc is None else acc + part
        o_ref[rows, :] = x + 0.5 * _rms(acc, gpost_ref[...])


def _ffn_block(x, g_pre, g_post, wg, wu, wd, which, mix=None):
    m = x.shape[0]
    tm = min(m, 2 * FFN_SUB_ROWS)

    def const(a):
        if a.ndim == 2:
            return pl.BlockSpec(a.shape, lambda i: (0, 0), pipeline_mode=pl.Buffered(1))
        return pl.BlockSpec((None,) + a.shape[1:], lambda i: (which, 0, 0), pipeline_mode=pl.Buffered(1))

    row = lambda a: pl.BlockSpec((tm, a.shape[1]), lambda i: (i, 0))
    args, in_specs = [x], [row(x)]
    if mix is not None:
        o_h, o_m, w1, w2, g_mix = mix
        args += [o_h, o_m, w1, w2, g_mix]
        in_specs += [row(o_h), row(o_m), const(w1), const(w2), const(g_mix)]
    args += [g_pre, g_post, wg, wu, wd]
    in_specs += [const(g_pre), const(g_post), const(wg), const(wu), const(wd)]
    return pl.pallas_call(
        functools.partial(_ffn_kernel, mix_in=mix is not None),
        grid=(m // tm,),
        in_specs=in_specs,
        out_specs=row(x),
        out_shape=jax.ShapeDtypeStruct((m, D_MODEL), F32),
        compiler_params=_params("parallel"),
        name="ffn_mix_block" if mix is not None else "ffn_block",
    )(*args)


def _inproj_kernel(x_ref, g_ref, csk_ref, cosq_ref, sinq_ref, wcq_ref, wckv_ref, wkr_ref,
                   qg_ref, kvg_ref, wqn_ref, wqr_ref, wqrr_ref, wuk_ref, *rest, with_h):
    if with_h:
        wh_ref, ckv_ref, kr_ref, ckvb_ref, krb_ref, qlat_ref, qrope_ref, zh_ref = rest
        half = wh_ref.shape[1] // 2
    else:
        ckv_ref, kr_ref, ckvb_ref, krb_ref, qlat_ref, qrope_ref = rest
    u = _rms(x_ref[...], g_ref[...]).astype(BF16)
    cq = _dot(u, wcq_ref[...])
    ckv = _dot(u, wckv_ref[...])
    kz = _dot(u, wkr_ref[...])
    if with_h:
        zh_ref[:, :half] = _dot(u, wh_ref[:, :half])
    cqn = _rms(cq, qg_ref[...]).astype(BF16)
    qn = _dot(cqn, wqn_ref[...])
    qr_a = _dot(cqn, wqr_ref[...])
    qr_b = _dot(cqn, wqrr_ref[...])
    if with_h:
        zh_ref[:, half:] = _dot(u, wh_ref[:, half:])
    qn = qn.astype(BF16)
    for h in range(MLA_HEADS):
        qlat_ref[:, h * KV_LORA:(h + 1) * KV_LORA] = _dot(
            qn[:, h * QK_NOPE:(h + 1) * QK_NOPE], wuk_ref[h]).astype(BF16)
    ckv = _rms(ckv, kvg_ref[...])
    ckv_ref[...] = ckv
    ckvb_ref[...] = ckv.astype(BF16)
    csk = csk_ref[...]
    kr = kz[:, :QK_ROPE] * csk[:, :QK_ROPE] + kz[:, QK_ROPE:] * csk[:, QK_ROPE:]
    kr_ref[...] = kr
    krb_ref[...] = kr.astype(BF16)
    qrope_ref[...] = (qr_a * cosq_ref[...] + qr_b * sinq_ref[...]).astype(BF16)


def _in_proj(x, g, csk, cosq, sinq, w, with_h):
    m = x.shape[0]
    tm = min(m, 512)
    nrep = csk.shape[0] // tm
    full = lambda a: pl.BlockSpec(a.shape, lambda i: (0,) * a.ndim, pipeline_mode=pl.Buffered(1))
    row = lambda width: pl.BlockSpec((tm, width), lambda i: (i, 0))
    tab = lambda width: pl.BlockSpec((tm, width), lambda i: (i % nrep, 0))
    rope_w = MLA_HEADS * QK_ROPE
    names = ["w_cq", "w_ckv", "w_kr", "q_norm_g", "kv_norm_g", "w_qn", "w_qr", "w_qrr", "w_uk"]
    names += ["w_h"] if with_h else []
    outs = [(KV_LORA, F32), (QK_ROPE, F32), (KV_LORA, BF16), (QK_ROPE, BF16),
            (MLA_HEADS * KV_LORA, BF16), (rope_w, BF16)]
    outs += [(4 * HG_W, F32)] if with_h else []
    return pl.pallas_call(
        functools.partial(_inproj_kernel, with_h=with_h),
        grid=(m // tm,),
        in_specs=[row(D_MODEL), full(g), tab(2 * QK_ROPE), tab(rope_w), tab(rope_w)]
        + [full(w[n]) for n in names],
        out_specs=[row(width) for width, _ in outs],
        out_shape=[jax.ShapeDtypeStruct((m, width), dt) for width, dt in outs],
        compiler_params=_params("parallel"),
        name="in_proj",
    )(x, g, csk, cosq, sinq, *[w[n] for n in names])


def _split3(g):
    g1 = g.astype(BF16).astype(F32)
    r1 = g - g1
    g2 = r1.astype(BF16).astype(F32)
    g3 = (r1 - g2).astype(BF16).astype(F32)
    return g1, g2, g3


HG_SUB = 16
HGRN_ROWS_PER_STEP = 32


def _hgrn_chunk_fn(load, store, st_ref, c_ref, v_ref, lbl_ref, ng_ref, units, chunk, valid_rows):
    n_tiles = chunk // SUBLANES

    lbl = lbl_ref[...]
    e = jnp.exp(lbl - jnp.max(lbl, axis=0, keepdims=True))
    lb_all = e[0:1, :] / jnp.sum(e, axis=0, keepdims=True)

    r_i = lax.broadcasted_iota(jnp.int32, (chunk, chunk), 0)
    c_i = lax.broadcasted_iota(jnp.int32, (chunk, chunk), 1)
    tril = (c_i <= r_i).astype(F32)
    row8 = lax.broadcasted_iota(jnp.int32, (SUBLANES, HG_DK), 0)
    ng = ng_ref[...]
    ones = jnp.ones((HG_DK, HG_DV), BF16)
    sub = min(chunk, HG_SUB)
    n_sub = chunk // sub
    pairs = [(s, i) for s in range(chunk) for i in range(s // SUBLANES, (s // sub + 1) * sub // SUBLANES)]
    if n_sub > 1:
        assert sub & (sub - 1) == 0 and chunk & (chunk - 1) == 0
        rows_sub = lax.broadcasted_iota(jnp.int32, (chunk, (n_sub - 1) * chunk), 0) >> (sub.bit_length() - 1)
        cols_blk = lax.broadcasted_iota(jnp.int32, (chunk, (n_sub - 1) * chunk), 1) >> (chunk.bit_length() - 1)
        sub_mask = cols_blk == rows_sub - 1

    def gates(u, start):
        h = units[u][1]
        lb = lb_all[:, h * HG_DK:(h + 1) * HG_DK]
        q = _silu(load(0, u, start))
        f = lb + (1.0 - lb) * _sigmoid(load(1, u, start))
        g = jnp.log(f)
        k = 1.0 - f
        v = load(2, u, start)
        valid = valid_rows(start)
        if valid is not None:
            g = jnp.where(valid, g, 0.0)
            k = jnp.where(valid, k, 0.0)
            v = jnp.where(valid, v, 0.0)
        g1, g2, g3 = _split3(g)
        b = _dot(tril, g1) + _dot(tril, g2) + _dot(tril, g3)
        return q, k, v, b * LOG2E

    def products(h, q, k, v, b2):
        c_ref[h] = b2 - jnp.log(k) * LOG2E
        v_ref[h] = v
        q_t = [q[i * SUBLANES:(i + 1) * SUBLANES, :] for i in range(n_tiles)]
        b_t = [b2[i * SUBLANES:(i + 1) * SUBLANES, :] for i in range(n_tiles)]
        p_tiles = []
        for s, i in pairs:
            d = b_t[i] - c_ref[h, s:s + 1, :]
            if i == s // SUBLANES and s % SUBLANES:
                d = jnp.where(row8 >= s % SUBLANES, d, NEG_BIG)
            p_tiles.append(q_t[i] * jnp.exp2(d))
        a_rep = _dot(jnp.concatenate(p_tiles, axis=0).astype(BF16), ones)

        a_sub = None
        if n_sub > 1:
            zeros_sub = jnp.zeros((sub, HG_DK), F32)
            q_parts, k_parts = [zeros_sub], []
            for j in range(1, n_sub):
                lo = j * sub
                r_j = b2[lo - 1:lo, :]
                q_parts.append(q[lo:lo + sub, :] * jnp.exp2(b2[lo:lo + sub, :] - r_j))
                k_parts += [k[:lo, :] * jnp.exp2(r_j - b2[:lo, :])] + [zeros_sub] * (n_sub - j)
            a_sub = _dot_nt(jnp.concatenate(q_parts, axis=0).astype(BF16),
                            jnp.concatenate(k_parts, axis=0).astype(BF16))

        st = st_ref[h]
        o_state = _dot_nt((q * jnp.exp2(b2)).astype(BF16), st.astype(BF16))
        bl = b2[chunk - 1:chunk, :]
        v_b = v.astype(BF16)
        st_ref[h] = st * jnp.exp2(bl) + _dot_tn(v_b, (k * jnp.exp2(bl - b2)).astype(BF16))
        return a_rep, a_sub, o_state, v_b

    def outputs(u, start, a_rep, a_sub, o_state, v_b):
        o_t = [o_state[i * SUBLANES:(i + 1) * SUBLANES, :] for i in range(n_tiles)]
        for j, (s, i) in enumerate(pairs):
            o_t[i] = o_t[i] + a_rep[j * SUBLANES:(j + 1) * SUBLANES, :] * v_ref[u, s:s + 1, :]
        o = jnp.concatenate(o_t, axis=0) if n_tiles > 1 else o_t[0]
        if a_sub is not None:
            o = o + _dot(jnp.where(sub_mask, a_sub, 0.0).astype(BF16),
                         jnp.concatenate([v_b] * (n_sub - 1), axis=0))
        store(u, start, _rms(o, ng) * _silu(load(3, u, start)))

    def chunk_fn(start):
        gated = [gates(u, start) for u in range(len(units))]
        prods = [products(u, *gated[u]) for u in range(len(units))]
        for u in range(len(units)):
            outputs(u, start, *prods[u])

    return chunk_fn


def _hgrn_kernel(*refs, chunk, t_valid, has_s0):
    if has_s0:
        hq_ref, hf_ref, hi_ref, hg_ref, lbl_ref, ng_ref, s0_ref = refs[:7]
        refs = refs[7:]
    else:
        hq_ref, hf_ref, hi_ref, hg_ref, lbl_ref, ng_ref = refs[:6]
        s0_ref = None
        refs = refs[6:]
    o_ref, sfin_ref, st_ref, c_ref, v_ref = refs
    ti = pl.program_id(1)
    n_batch, t_tile = hq_ref.shape[:2]
    units = [divmod(u, HG_HEADS) for u in range(n_batch * HG_HEADS)]
    n_chunks = t_tile // chunk
    fields = (hq_ref, hf_ref, hi_ref, hg_ref)

    @pl.when(ti == 0)
    def _():
        for u, (e, h) in enumerate(units):
            st_ref[u] = s0_ref[e, h].T if has_s0 else jnp.zeros((HG_DV, HG_DK), F32)

    def where(u, start):
        e, h = units[u]
        return e, pl.ds(start, chunk), slice(h * HG_DK, (h + 1) * HG_DK)

    def load(field, u, start):
        return fields[field][where(u, start)]

    def store(u, start, value):
        o_ref[where(u, start)] = value.astype(o_ref.dtype)

    def valid_rows(start):
        if t_valid is None:
            return None
        pos = lax.broadcasted_iota(jnp.int32, (chunk, HG_DK), 0) + (ti * t_tile + start)
        return pos < t_valid

    chunk_fn = _hgrn_chunk_fn(load, store, st_ref, c_ref, v_ref, lbl_ref, ng_ref, units, chunk, valid_rows)

    def chunk_body(c, carry):
        chunk_fn(pl.multiple_of(c * chunk, chunk))
        return carry

    lax.fori_loop(0, n_chunks, chunk_body, 0, unroll=4 if n_chunks % 4 == 0 else 1)

    @pl.when(ti == pl.num_programs(1) - 1)
    def _():
        for u, (e, h) in enumerate(units):
            sfin_ref[e, h] = st_ref[u].T


def _hgrn(zh, lb_logits, ng, s0, chunk, t_valid):
    bsz, t, _ = zh.shape
    has_s0 = s0 is not None
    t_tile = min(t, 512)
    assert t % t_tile == 0 and t_tile % chunk == 0
    n_batch = math.gcd(bsz, max(1, HGRN_ROWS_PER_STEP // t_tile))
    n_units = n_batch * HG_HEADS
    col = lambda j: pl.BlockSpec((n_batch, t_tile, HG_W), lambda b, i: (b, i, j))
    in_specs = [col(0), col(1), col(2), col(3),
                pl.BlockSpec(lb_logits.shape, lambda b, i: (0, 0)),
                pl.BlockSpec((1, HG_DV), lambda b, i: (0, 0))]
    args = [zh, zh, zh, zh, lb_logits, ng]
    st_spec = pl.BlockSpec((n_batch, HG_HEADS, HG_DK, HG_DV), lambda b, i: (b, 0, 0, 0))
    if has_s0:
        in_specs.append(st_spec)
        args.append(s0)
    return pl.pallas_call(
        functools.partial(_hgrn_kernel, chunk=chunk, t_valid=t_valid, has_s0=has_s0),
        grid=(bsz // n_batch, t // t_tile),
        in_specs=in_specs,
        out_specs=[pl.BlockSpec((n_batch, t_tile, HG_W), lambda b, i: (b, i, 0)), st_spec],
        out_shape=[jax.ShapeDtypeStruct((bsz, t, HG_W), BF16),
                   jax.ShapeDtypeStruct((bsz, HG_HEADS, HG_DK, HG_DV), F32)],
        scratch_shapes=[pltpu.VMEM((n_units, HG_DV, HG_DK), F32),
                        pltpu.VMEM((n_units, chunk, HG_DK), F32),
                        pltpu.VMEM((n_units, chunk, HG_DV), F32)],
        compiler_params=_params("parallel", "arbitrary"),
        name="hgrn",
    )(*args)


HGRN_PIECES = 8


def _hgrn_proj_kernel(xa_ref, xb_ref, x0_ref, g_ref, wh_ref, lbl_ref, ng_ref, o_ref, sfin_ref,
                      zh_ref, u_ref, st_ref, c_ref, v_ref, *, chunk, tiles_per_seq):
    g = pl.program_id(0)
    t_tile = xa_ref.shape[0]
    n_chunks = t_tile // chunk
    assert n_chunks == HGRN_PIECES and tiles_per_seq % 2 == 0
    piece = wh_ref.shape[1] // HGRN_PIECES
    units = [(0, h) for h in range(HG_HEADS)]

    def normed(x_ref):
        u_ref[...] = _rms(x_ref[...], g_ref[...]).astype(BF16)

    def project(slot, k):
        cols = slice(k * piece, (k + 1) * piece)
        zh_ref[slot, :, cols] = _dot(u_ref[...], wh_ref[:, cols])

    @pl.when(g == 0)
    def _():
        st_ref[...] = jnp.zeros_like(st_ref)
        normed(x0_ref)
        for k in range(HGRN_PIECES):
            project(0, k)

    def run_tile(slot, x_next_ref, row0, new_seq):
        if new_seq is not None:
            for u in range(len(units)):
                st_ref[u] = jnp.where(new_seq, 0.0, st_ref[u])

        def load(field, u, start):
            col = (field * HG_HEADS + units[u][1]) * HG_DK
            return zh_ref[slot, start:start + chunk, col:col + HG_DK]

        def store(u, start, value):
            h = units[u][1]
            o_ref[row0 + start:row0 + start + chunk, h * HG_DV:(h + 1) * HG_DV] = value.astype(o_ref.dtype)

        chunk_fn = _hgrn_chunk_fn(load, store, st_ref, c_ref, v_ref, lbl_ref, ng_ref, units, chunk,
                                  lambda start: None)
        normed(x_next_ref)
        for c in range(n_chunks):
            project(1 - slot, c)
            chunk_fn(c * chunk)

    tile_a = 2 * g
    run_tile(0, xa_ref, 0, (tile_a % tiles_per_seq) == 0)
    run_tile(1, xb_ref, t_tile, None)

    @pl.when((tile_a + 1) % tiles_per_seq == tiles_per_seq - 1)
    def _():
        for u, (_, h) in enumerate(units):
            sfin_ref[h] = st_ref[u].T


def _hgrn_proj(x, g, w_h, lb_logits, ng, bsz, t, chunk):
    t_tile = chunk * HGRN_PIECES
    assert t % t_tile == 0
    tiles_per_seq = t // t_tile
    n_tiles = bsz * tiles_per_seq
    assert n_tiles % 2 == 0
    const = lambda a: pl.BlockSpec(a.shape, lambda i: (0, 0), pipeline_mode=pl.Buffered(1))
    tile = lambda f: pl.BlockSpec((t_tile, D_MODEL), lambda i: (jnp.minimum(f(i), n_tiles - 1), 0))
    return pl.pallas_call(
        functools.partial(_hgrn_proj_kernel, chunk=chunk, tiles_per_seq=tiles_per_seq),
        grid=(n_tiles // 2,),
        in_specs=[tile(lambda i: 2 * i + 1), tile(lambda i: 2 * i + 2), tile(lambda i: 0),
                  const(g), const(w_h), const(lb_logits), const(ng)],
        out_specs=[pl.BlockSpec((2 * t_tile, HG_W), lambda i: (i, 0)),
                   pl.BlockSpec((None, HG_HEADS, HG_DK, HG_DV), lambda i: ((2 * i + 1) // tiles_per_seq, 0, 0, 0))],
        out_shape=[jax.ShapeDtypeStruct((bsz * t, HG_W), BF16),
                   jax.ShapeDtypeStruct((bsz, HG_HEADS, HG_DK, HG_DV), F32)],
        scratch_shapes=[pltpu.VMEM((2, t_tile, 4 * HG_W), F32), pltpu.VMEM((t_tile, D_MODEL), BF16),
                        pltpu.VMEM((HG_HEADS, HG_DV, HG_DK), F32),
                        pltpu.VMEM((HG_HEADS, chunk, HG_DK), F32),
                        pltpu.VMEM((HG_HEADS, chunk, HG_DV), F32)],
        compiler_params=_params("arbitrary"),
        name="hgrn_proj",
    )(x, x, x, g, w_h, lb_logits, ng)


ATTN_SCALE = 1.0 / math.sqrt(QK_NOPE + QK_ROPE)
ATTN_EXP2_SCALE = ATTN_SCALE * LOG2E

LANES = 128


def _attn_prompt_kernel(ql_ref, qr_ref, kc_ref, kr_ref, wuv_ref, o_ref,
                        m_ref, l_ref, a_ref, acc_ref, s_ref, p_ref, *, tq):
    qi = pl.program_id(1)
    ql = jnp.concatenate([ql_ref[:, h * KV_LORA:(h + 1) * KV_LORA] for h in range(MLA_HEADS)], axis=0)
    qr = jnp.concatenate([qr_ref[:, h * QK_ROPE:(h + 1) * QK_ROPE] for h in range(MLA_HEADS)], axis=0)
    m_ref[...] = jnp.full_like(m_ref, NEG_BIG)
    l_ref[...] = jnp.zeros_like(l_ref)
    a_ref[...] = jnp.zeros_like(a_ref)
    acc_ref[...] = jnp.zeros_like(acc_ref)
    p_ref[1] = jnp.zeros(p_ref.shape[1:], BF16)

    def keys(kb):
        return pl.ds(pl.multiple_of(kb * tq, tq), tq)

    def scores(kb):
        return (_dot_nt(ql, kc_ref[keys(kb), :]) + _dot_nt(qr, kr_ref[keys(kb), :])) * ATTN_EXP2_SCALE

    def add_values(kb, slot):
        acc_ref[...] = (jnp.tile(a_ref[...], (1, KV_LORA // LANES)) * acc_ref[...]
                        + _dot(p_ref[slot], kc_ref[keys(kb), :]))

    def softmax(slot, masked):
        s = s_ref[slot]
        if masked:
            rows = lax.broadcasted_iota(jnp.int32, s.shape, 0) & (tq - 1)
            cols = lax.broadcasted_iota(jnp.int32, s.shape, 1)
            s = jnp.where(cols <= rows, s, NEG_BIG)
        m_prev = m_ref[...]
        m_new = jnp.maximum(m_prev, jnp.max(s, axis=-1, keepdims=True))
        alpha = jnp.exp2(m_prev - m_new)
        p = jnp.exp2(s - jnp.tile(m_new, (1, tq // LANES)))
        l_ref[...] = alpha * l_ref[...] + jnp.sum(p, axis=-1, keepdims=True)
        m_ref[...] = m_new
        a_ref[...] = alpha
        p_ref[slot] = p.astype(BF16)

    s_ref[0] = scores(0)

    def iteration(i, cur):
        s_ref[1 - cur] = scores(i + 1)
        add_values(jnp.maximum(i - 1, 0), 1 - cur)
        softmax(cur, False)

    def last(cur):
        add_values(jnp.maximum(qi - 1, 0), 1 - cur)
        softmax(cur, True)
        add_values(qi, cur)

    def pair(j, carry):
        iteration(2 * j, 0)
        iteration(2 * j + 1, 1)
        return carry

    lax.fori_loop(0, qi >> 1, pair, 0)

    @pl.when((qi & 1) == 1)
    def _():
        iteration(qi - 1, 0)
        last(1)

    @pl.when((qi & 1) == 0)
    def _():
        last(0)

    o_lat =(acc_ref[...] / jnp.tile(l_ref[...], (1, KV_LORA // LANES))).astype(BF16)
    for h in range(MLA_HEADS):
        o_ref[:, h * V_DIM:(h + 1) * V_DIM] = _dot(o_lat[h * tq:(h + 1) * tq, :], wuv_ref[h]).astype(o_ref.dtype)


def _attn_prompt(qlat, qrope, ckv_b, kr_b, w_uv, bsz, t):
    tq = 256
    assert tq & (tq - 1) == 0 and t % tq == 0
    nq = t // tq
    rows = MLA_HEADS * tq
    return pl.pallas_call(
        functools.partial(_attn_prompt_kernel, tq=tq),
        grid=(bsz, nq),
        in_specs=[
            pl.BlockSpec((tq, MLA_HEADS * KV_LORA), lambda b, i: (b * nq + i, 0)),
            pl.BlockSpec((tq, MLA_HEADS * QK_ROPE), lambda b, i: (b * nq + i, 0)),
            pl.BlockSpec((t, KV_LORA), lambda b, i: (b, 0)),
            pl.BlockSpec((t, QK_ROPE), lambda b, i: (b, 0)),
            pl.BlockSpec(w_uv.shape, lambda b, i: (0, 0, 0)),
        ],
        out_specs=pl.BlockSpec((tq, MLA_HEADS * V_DIM), lambda b, i: (b * nq + i, 0)),
        out_shape=jax.ShapeDtypeStruct((bsz * t, MLA_HEADS * V_DIM), BF16),
        scratch_shapes=[pltpu.VMEM((rows, LANES), F32), pltpu.VMEM((rows, LANES), F32),
                        pltpu.VMEM((rows, LANES), F32), pltpu.VMEM((rows, KV_LORA), F32),
                        pltpu.VMEM((2, rows, tq), F32), pltpu.VMEM((2, rows, tq), BF16)],
        compiler_params=_params("parallel", "parallel"),
        name="attn_prompt",
    )(qlat, qrope, ckv_b, kr_b, w_uv)


PAGES_PER_STEP = 16
PAGE_RING = 4
NEW_KEY_ROWS = 128


def _attn_sample_kernel(pt_ref, ql_ref, qr_ref, nc_ref, nr_ref, wuv_ref, cc_hbm, cr_hbm, o_ref,
                        cbuf, rbuf, sem, wq_ref, wr_ref, m_ref, l_ref, acc_ref, kv_ref, kq_ref, kr_ref,
                        *, t_new, n_q, n_steps):
    b = pl.program_id(0)
    nb = pl.num_programs(0)
    groups = LANES // n_q
    page = cbuf.shape[1] // PAGES_PER_STEP
    ppg = PAGES_PER_STEP // groups
    lat_rep = KV_LORA // LANES

    def page_copies(bb, step, slot):
        copies = []
        for i in range(PAGES_PER_STEP):
            pid = pt_ref[bb, step * PAGES_PER_STEP + i]
            copies.append(pltpu.make_async_copy(cc_hbm.at[pid], cbuf.at[slot, pl.ds(i * page, page), :],
                                                sem.at[slot]))
            copies.append(pltpu.make_async_copy(cr_hbm.at[pid], rbuf.at[slot, i], sem.at[slot]))
        return copies

    @pl.when(b == 0)
    def _():
        for step in range(PAGE_RING):
            for c in page_copies(0, step, step):
                c.start()

    q_shift = n_q.bit_length() - 1
    spread = (lax.broadcasted_iota(jnp.int32, (n_q, LANES), 1) & (n_q - 1)) == lax.broadcasted_iota(
        jnp.int32, (n_q, LANES), 0)
    spread = jnp.where(spread, 1.0, 0.0).astype(BF16)
    for src, dst in ((ql_ref, wq_ref), (qr_ref, wr_ref)):
        feat = src.shape[0]
        tiled = _dot(src[...], spread)
        lane_g = lax.broadcasted_iota(jnp.int32, tiled.shape, 1) >> q_shift
        for g in range(groups):
            dst[g * feat:(g + 1) * feat, :] = jnp.where(lane_g == g, tiled, 0.0).astype(BF16)

    m_ref[...] = jnp.full_like(m_ref, NEG_BIG)
    l_ref[...] = jnp.zeros_like(l_ref)
    acc_ref[...] = jnp.zeros_like(acc_ref)

    def lanes_to_rows(row):
        return jnp.broadcast_to(row, (LANES, LANES)).T

    def update(s, place, values):
        m_prev = m_ref[...]
        m_new = jnp.maximum(m_prev, jnp.max(s, axis=0, keepdims=True))
        alpha = jnp.exp2(m_prev - m_new)
        p = jnp.exp2(s - m_new)
        l_ref[...] = alpha * l_ref[...] + jnp.sum(p, axis=0, keepdims=True)
        m_ref[...] = m_new
        pv = _dot(place(p.T), values)
        acc_ref[...] = jnp.tile(lanes_to_rows(alpha), (1, lat_rep)) * acc_ref[...] + pv

    def block_diag(p_t):
        row_g = lax.broadcasted_iota(jnp.int32, p_t.shape, 0) >> q_shift
        return jnp.concatenate([jnp.where(row_g == g, p_t, 0.0).astype(BF16) for g in range(groups)], axis=1)

    def stage_and_score(j):
        ring = j % PAGE_RING
        slot = j & 1
        for c in page_copies(b, j, ring):
            c.wait()
        for i in range(PAGES_PER_STEP):
            g, r = divmod(i, ppg)
            x = cbuf[ring, i * page:(i + 1) * page, :].astype(BF16)
            kv_ref[slot, i * page:(i + 1) * page, :] = x
            kq_ref[slot, r * page:(r + 1) * page, g * KV_LORA:(g + 1) * KV_LORA] = x
            kr_ref[slot, g * QK_ROPE:(g + 1) * QK_ROPE, r * page:(r + 1) * page] = rbuf[ring, i].astype(BF16)
        if j + PAGE_RING < n_steps:
            nxt = page_copies(b, j + PAGE_RING, ring)
        else:
            nxt = page_copies(jnp.minimum(b + 1, nb - 1), j + PAGE_RING - n_steps, ring)
        for c in nxt:
            c.start()
        return (_dot(kq_ref[slot], wq_ref[...]) + _dot_tn(kr_ref[slot], wr_ref[...])) * ATTN_EXP2_SCALE

    s_cur = stage_and_score(0)
    for j in range(n_steps):
        s_next = stage_and_score(j + 1) if j + 1 < n_steps else None
        update(s_cur, block_diag, kv_ref[j & 1])
        s_cur = s_next

    @pl.when(b == nb - 1)
    def _():
        for step in range(PAGE_RING):
            for c in page_copies(b, step, step):
                c.wait()

    nc = nc_ref[...]
    s2 = (_dot(nc, wq_ref[:KV_LORA, :]) + _dot(nr_ref[...], wr_ref[:QK_ROPE, :])) * ATTN_EXP2_SCALE
    t_k = lax.broadcasted_iota(jnp.int32, s2.shape, 0)
    lane = lax.broadcasted_iota(jnp.int32, s2.shape, 1)
    ok = (lane < n_q) & (t_k <= (lane & (t_new - 1)))
    update(jnp.where(ok, s2, NEG_BIG), lambda p_t: p_t.astype(BF16), nc)

    grp = lambda a, g: a[g * n_q:(g + 1) * n_q]
    total = lambda a: functools.reduce(lambda x, y: x + y, [grp(a, g) for g in range(groups)])
    m_t = lanes_to_rows(m_ref[...])
    l_t = lanes_to_rows(l_ref[...])
    m_q = functools.reduce(jnp.maximum, [grp(m_t, g) for g in range(groups)])
    w = jnp.exp2(m_t - jnp.tile(m_q, (groups, 1)))
    l_q = total(w * l_t)
    o_lat = total(jnp.tile(w, (1, lat_rep)) * acc_ref[...]) / jnp.tile(l_q, (1, lat_rep))
    o_lat = o_lat.astype(BF16)
    head = lax.broadcasted_iota(jnp.int32, (n_q, V_DIM), 0) >> (t_new.bit_length() - 1)
    out = jnp.zeros((n_q, V_DIM), F32)
    for h in range(MLA_HEADS):
        out = out + jnp.where(head == h, _dot(o_lat, wuv_ref[h]), 0.0)
    o_ref[...] = out.astype(o_ref.dtype)


def _attn_sample(ql_t, qr_t, new_c, new_r, w_uv, cache_c, cache_r_t, page_table, t_new):
    bsz, _, n_q = ql_t.shape
    n_pages = page_table.shape[1]
    groups = LANES // n_q
    assert LANES % n_q == 0 and PAGES_PER_STEP % groups == 0 and n_pages % PAGES_PER_STEP == 0
    assert t_new & (t_new - 1) == 0 and n_q & (n_q - 1) == 0 and t_new <= NEW_KEY_ROWS and n_q >= SUBLANES
    page = cache_c.shape[1]
    keys = PAGES_PER_STEP * page
    n_steps = n_pages // PAGES_PER_STEP
    assert n_steps % PAGE_RING == 0

    per_b = lambda a: pl.BlockSpec((None,) + a.shape[1:], lambda b, pt: (b, 0, 0))
    grid_spec = pltpu.PrefetchScalarGridSpec(
        num_scalar_prefetch=1,
        grid=(bsz,),
        in_specs=[per_b(ql_t), per_b(qr_t), per_b(new_c), per_b(new_r),
                  pl.BlockSpec(w_uv.shape, lambda b, pt: (0, 0, 0)),
                  pl.BlockSpec(memory_space=pl.ANY), pl.BlockSpec(memory_space=pl.ANY)],
        out_specs=pl.BlockSpec((None, n_q, V_DIM), lambda b, pt: (b, 0, 0)),
        scratch_shapes=[pltpu.VMEM((PAGE_RING, keys, KV_LORA), cache_c.dtype),
                        pltpu.VMEM((PAGE_RING, PAGES_PER_STEP, QK_ROPE, page), cache_r_t.dtype),
                        pltpu.SemaphoreType.DMA((PAGE_RING,)),
                        pltpu.VMEM((groups * KV_LORA, LANES), BF16),
                        pltpu.VMEM((groups * QK_ROPE, LANES), BF16),
                        pltpu.VMEM((1, LANES), F32), pltpu.VMEM((1, LANES), F32),
                        pltpu.VMEM((LANES, KV_LORA), F32),
                        pltpu.VMEM((2, keys, KV_LORA), BF16),
                        pltpu.VMEM((2, keys // groups, groups * KV_LORA), BF16),
                        pltpu.VMEM((2, groups * QK_ROPE, keys // groups), BF16)],
    )
    return pl.pallas_call(
        functools.partial(_attn_sample_kernel, t_new=t_new, n_q=n_q, n_steps=n_steps),
        grid_spec=grid_spec,
        out_shape=jax.ShapeDtypeStruct((bsz, n_q, V_DIM), BF16),
        compiler_params=_params("arbitrary"),
        name="attn_sample",
    )(page_table, ql_t, qr_t, new_c, new_r, w_uv, cache_c, cache_r_t)


def _rot_cols(w):
    half = w.shape[-1] // 2
    return jnp.concatenate([-w[..., half:], w[..., :half]], axis=-1)


def _prep_weights(norm_g, w_ffn_gate, w_ffn_up, w_ffn_down, w_in, q_norm_g, w_q_up, kv_norm_g,
                  w_kv_up, hg_norm_g, w_out):
    bf = lambda a: a.astype(BF16)
    w_h, w_cq, w_ckv, w_kr = jnp.split(w_in, [4 * HG_W, 4 * HG_W + Q_LORA, 4 * HG_W + Q_LORA + KV_LORA], axis=-1)
    w_q_rope = w_q_up[..., QK_NOPE:]
    return {
        "norm_g": norm_g.reshape(-1, 1, D_MODEL),
        "wg": bf(w_ffn_gate), "wu": bf(w_ffn_up), "wd": bf(w_ffn_down),
        "w_h": bf(w_h), "w_cq": bf(w_cq), "w_ckv": bf(w_ckv),
        "w_kr": bf(jnp.concatenate([w_kr, _rot_cols(w_kr)], axis=-1)),
        "q_norm_g": q_norm_g.reshape(1, Q_LORA), "kv_norm_g": kv_norm_g.reshape(1, KV_LORA),
        "w_qn": bf(w_q_up[..., :QK_NOPE].reshape(Q_LORA, MLA_HEADS * QK_NOPE)),
        "w_qr": bf(w_q_rope.reshape(Q_LORA, MLA_HEADS * QK_ROPE)),
        "w_qrr": bf(_rot_cols(w_q_rope).reshape(Q_LORA, MLA_HEADS * QK_ROPE)),
        "w_uk": bf(jnp.transpose(w_kv_up[..., :QK_NOPE], (1, 2, 0))),
        "w_uv": bf(jnp.transpose(w_kv_up[..., QK_NOPE:], (1, 0, 2))),
        "hg_norm_g": hg_norm_g.reshape(1, HG_DV),
        "w_out1": bf(w_out[:HG_W]), "w_out2": bf(w_out[HG_W:]),
    }


def _rope_tables(pos):
    half = QK_ROPE // 2
    inv = ROPE_THETA ** (-jnp.arange(half, dtype=F32) / half)
    ang = pos.astype(F32)[:, None] * inv[None, :]
    cos = jnp.tile(jnp.cos(ang), (1, 2))
    sin = jnp.tile(jnp.sin(ang), (1, 2))
    return (jnp.concatenate([cos, sin], axis=-1), jnp.tile(cos, (1, MLA_HEADS)), jnp.tile(sin, (1, MLA_HEADS)))


def kernel(x_prompt, x_sample, cache_kv_latent, cache_k_rope, state_hgrn, page_table, hgrn_lb_logits,
           norm_g, w_ffn_gate, w_ffn_up, w_ffn_down, w_in, q_norm_g, w_q_up, kv_norm_g, w_kv_up,
           hg_norm_g, w_out):
    bp, tp, _ = x_prompt.shape
    bs, ts, _ = x_sample.shape
    depth = norm_g.shape[0]
    assert depth == 1
    past_len = page_table.shape[1] * cache_kv_latent.shape[2]
    w = _prep_weights(norm_g[0], w_ffn_gate[0], w_ffn_up[0], w_ffn_down[0], w_in[0], q_norm_g[0],
                      w_q_up[0], kv_norm_g[0], w_kv_up[0], hg_norm_g[0], w_out[0])
    ng = w["norm_g"]
    lb_logits = hgrn_lb_logits.astype(F32)

    def pre(x, tables, with_h):
        h1 = _ffn_block(x, ng[0], ng[1], w["wg"], w["wu"], w["wd"], 0)
        return (h1,) + tuple(_in_proj(h1, ng[2], *tables, w, with_h))

    def post(h1, o_h, o_m):
        return _ffn_block(h1, ng[4], ng[5], w["wg"], w["wu"], w["wd"], 1,
                          mix=(o_h, o_m, w["w_out1"], w["w_out2"], ng[3]))

    h1, ckv, kr, ckv_b, kr_b, qlat, qrope = pre(
        x_prompt.reshape(bp * tp, D_MODEL), _rope_tables(jnp.arange(tp, dtype=jnp.int32)), False)
    o_h, st_p = _hgrn_proj(h1, ng[2], w["w_h"], lb_logits, w["hg_norm_g"], bp, tp, chunk=64)
    o_m = _attn_prompt(qlat, qrope, ckv_b, kr_b, w["w_uv"], bp, tp)
    y_p = post(h1, o_h, o_m).reshape(bp, tp, D_MODEL)
    ckv_p = ckv.reshape(1, bp, tp, KV_LORA)
    kr_p = kr.reshape(1, bp, tp, QK_ROPE)

    pos_s = past_len + jnp.arange(ts, dtype=jnp.int32)
    tabs = tuple(jnp.tile(a, (bs, 1)) for a in _rope_tables(pos_s))
    h1s, ckvs, krs, ckvs_b, krs_b, qlats, qropes, zhs = pre(x_sample.reshape(bs * ts, D_MODEL), tabs, True)
    t_pad = -(-ts // SUBLANES) * SUBLANES
    zhs_pad = jnp.pad(zhs.reshape(bs, ts, 4 * HG_W), ((0, 0), (0, t_pad - ts), (0, 0)))
    o_hs, st_s = _hgrn(zhs_pad, lb_logits, w["hg_norm_g"], state_hgrn[0], chunk=SUBLANES,
                       t_valid=ts if t_pad != ts else None)
    q_cols = lambda a, wd: a.reshape(bs, ts, MLA_HEADS, wd).transpose(0, 3, 2, 1).reshape(bs, wd, MLA_HEADS * ts)
    pad_new = lambda a, wd: jnp.pad(a.reshape(bs, ts, wd), ((0, 0), (0, NEW_KEY_ROWS - ts), (0, 0)))
    o_ms = _attn_sample(q_cols(qlats, KV_LORA), q_cols(qropes, QK_ROPE),
                        pad_new(ckvs_b, KV_LORA), pad_new(krs_b, QK_ROPE), w["w_uv"],
                        cache_kv_latent[0], jnp.swapaxes(cache_k_rope[0], 1, 2), page_table, ts)
    o_ms = o_ms.reshape(bs, MLA_HEADS, ts, V_DIM).transpose(0, 2, 1, 3).reshape(bs * ts, MLA_HEADS * V_DIM)
    y_s = post(h1s, o_hs[:, :ts].reshape(bs * ts, HG_W), o_ms).reshape(bs, ts, D_MODEL)

    return (y_p, y_s, ckv_p, kr_p, st_p[None], ckvs.reshape(1, bs, ts, KV_LORA),
            krs.reshape(1, bs, ts, QK_ROPE), st_s[None])
```

```python
import functools
import math

import jax
import jax.numpy as jnp
from jax import lax
from jax.experimental import pallas as pl
from jax.experimental.pallas import tpu as pltpu

D_MODEL = 1024
D_FF = 2816
HG_HEADS = 4
HG_DK = 128
HG_DV = 128
MLA_HEADS = 4
QK_NOPE = 128
QK_ROPE = 64
V_DIM = 128
Q_LORA = 768
KV_LORA = 256
ROPE_THETA = 10000.0
EPS = 1e-6
HG_W = HG_HEADS * HG_DK

VMEM_LIMIT_BYTES = 56 * 1024 * 1024
SUBLANES = 8
LANES = 128
IN_PROJ_ROWS = 512
HGRN_CHUNK = 64
HGRN_TILE_ROWS = 512
ATTN_Q_ROWS = 256
NEG_BIG = -1e30
LOG2E = math.log2(math.e)

F32 = jnp.float32
BF16 = jnp.bfloat16


def _dot(a, b):
    return jnp.dot(a, b, preferred_element_type=F32)


def _dot_nt(a, b):
    return lax.dot_general(a, b, (((1,), (1,)), ((), ())), preferred_element_type=F32)


def _dot_tn(a, b):
    return lax.dot_general(a, b, (((0,), (0,)), ((), ())), preferred_element_type=F32)


def _rms(x, g):
    return x * lax.rsqrt(jnp.mean(x * x, axis=-1, keepdims=True) + EPS) * g


def _sigmoid(x):
    return 1.0 / (1.0 + jnp.exp(-x))


def _silu(x):
    return x * _sigmoid(x)


def _params(*sem):
    return pltpu.CompilerParams(dimension_semantics=sem, vmem_limit_bytes=VMEM_LIMIT_BYTES)


FFN_CHUNK = 256
FFN_SUB_ROWS = 512


def _ffn_kernel(*refs, mix_in):
    if mix_in:
        h_ref, oh_ref, om_ref, w1_ref, w2_ref, gmix_ref = refs[:6]
        refs = refs[6:]
    else:
        h_ref = refs[0]
        refs = refs[1:]
    gpre_ref, gpost_ref, wg_ref, wu_ref, wd_ref, o_ref = refs
    tm = h_ref.shape[0]
    sub = min(tm, FFN_SUB_ROWS)
    tiles = [slice(r * sub, (r + 1) * sub) for r in range(tm // sub)]
    if mix_in:
        mixed = [_dot(oh_ref[rows, :], w1_ref[...]) + _dot(om_ref[rows, :], w2_ref[...]) for rows in tiles]
        xs = [h_ref[rows, :] + _rms(m, gmix_ref[...]) for rows, m in zip(tiles, mixed)]
    else:
        xs = [h_ref[rows, :] for rows in tiles]
    xns = [_rms(x, gpre_ref[...]).astype(BF16) for x in xs]
    for rows, x, xn in zip(tiles, xs, xns):
        acc = None
        for f in range(D_FF // FFN_CHUNK):
            cols = slice(f * FFN_CHUNK, (f + 1) * FFN_CHUNK)
            a = _dot(xn, wg_ref[:, cols])
            b = _dot(xn, wu_ref[:, cols])
            part = _dot((_silu(a) * b).astype(BF16), wd_ref[cols, :])
            acc = part if acc is None else acc + part
        o_ref[rows, :] = x + 0.5 * _rms(acc, gpost_ref[...])


def _ffn_block(x, g_pre, g_post, wg, wu, wd, which, mix=None):
    m = x.shape[0]
    tm = min(m, 2 * FFN_SUB_ROWS)

    def const(a):
        if a.ndim == 2:
            return pl.BlockSpec(a.shape, lambda i: (0, 0), pipeline_mode=pl.Buffered(1))
        return pl.BlockSpec((None,) + a.shape[1:], lambda i: (which, 0, 0), pipeline_mode=pl.Buffered(1))

    row = lambda a: pl.BlockSpec((tm, a.shape[1]), lambda i: (i, 0))
    args, in_specs = [x], [row(x)]
    if mix is not None:
        o_h, o_m, w1, w2, g_mix = mix
        args += [o_h, o_m, w1, w2, g_mix]
        in_specs += [row(o_h), row(o_m), const(w1), const(w2), const(g_mix)]
    args += [g_pre, g_post, wg, wu, wd]
    in_specs += [const(g_pre), const(g_post), const(wg), const(wu), const(wd)]
    return pl.pallas_call(
        functools.partial(_ffn_kernel, mix_in=mix is not None),
        grid=(m // tm,),
        in_specs=in_specs,
        out_specs=row(x),
        out_shape=jax.ShapeDtypeStruct((m, D_MODEL), F32),
        compiler_params=_params("parallel"),
        name="ffn_mix_block" if mix is not None else "ffn_block",
    )(*args)


def _inproj_kernel(x_ref, g_ref, csk_ref, cosq_ref, sinq_ref, wcq_ref, wckv_ref, wkr_ref,
                   qg_ref, kvg_ref, wqn_ref, wqr_ref, wqrr_ref, wuk_ref, *rest, with_h):
    if with_h:
        wh_ref, ckv_ref, kr_ref, ckvb_ref, krb_ref, qlat_ref, qrope_ref, zh_ref = rest
        half = wh_ref.shape[1] // 2
    else:
        ckv_ref, kr_ref, ckvb_ref, krb_ref, qlat_ref, qrope_ref = rest
    u = _rms(x_ref[...], g_ref[...]).astype(BF16)
    cq = _dot(u, wcq_ref[...])
    ckv = _dot(u, wckv_ref[...])
    kz = _dot(u, wkr_ref[...])
    if with_h:
        zh_ref[:, :half] = _dot(u, wh_ref[:, :half])
    cqn = _rms(cq, qg_ref[...]).astype(BF16)
    qn = _dot(cqn, wqn_ref[...])
    qr_a = _dot(cqn, wqr_ref[...])
    qr_b = _dot(cqn, wqrr_ref[...])
    if with_h:
        zh_ref[:, half:] = _dot(u, wh_ref[:, half:])
    qn = qn.astype(BF16)
    for h in range(MLA_HEADS):
        qlat_ref[:, h * KV_LORA:(h + 1) * KV_LORA] = _dot(
            qn[:, h * QK_NOPE:(h + 1) * QK_NOPE], wuk_ref[h]).astype(BF16)
    ckv = _rms(ckv, kvg_ref[...])
    ckv_ref[...] = ckv
    ckvb_ref[...] = ckv.astype(BF16)
    csk = csk_ref[...]
    kr = kz[:, :QK_ROPE] * csk[:, :QK_ROPE] + kz[:, QK_ROPE:] * csk[:, QK_ROPE:]
    kr_ref[...] = kr
    krb_ref[...] = kr.astype(BF16)
    qrope_ref[...] = (qr_a * cosq_ref[...] + qr_b * sinq_ref[...]).astype(BF16)


def _in_proj(x, g, csk, cosq, sinq, w, with_h):
    m = x.shape[0]
    tm = min(m, IN_PROJ_ROWS)
    nrep = csk.shape[0] // tm
    full = lambda a: pl.BlockSpec(a.shape, lambda i: (0,) * a.ndim, pipeline_mode=pl.Buffered(1))
    row = lambda width: pl.BlockSpec((tm, width), lambda i: (i, 0))
    tab = lambda width: pl.BlockSpec((tm, width), lambda i: (i % nrep, 0))
    rope_w = MLA_HEADS * QK_ROPE
    names = ["w_cq", "w_ckv", "w_kr", "q_norm_g", "kv_norm_g", "w_qn", "w_qr", "w_qrr", "w_uk"]
    names += ["w_h"] if with_h else []
    outs = [(KV_LORA, F32), (QK_ROPE, F32), (KV_LORA, BF16), (QK_ROPE, BF16),
            (MLA_HEADS * KV_LORA, BF16), (rope_w, BF16)]
    outs += [(4 * HG_W, F32)] if with_h else []
    return pl.pallas_call(
        functools.partial(_inproj_kernel, with_h=with_h),
        grid=(m // tm,),
        in_specs=[row(D_MODEL), full(g), tab(2 * QK_ROPE), tab(rope_w), tab(rope_w)]
        + [full(w[n]) for n in names],
        out_specs=[row(width) for width, _ in outs],
        out_shape=[jax.ShapeDtypeStruct((m, width), dt) for width, dt in outs],
        compiler_params=_params("parallel"),
        name="in_proj",
    )(x, g, csk, cosq, sinq, *[w[n] for n in names])


def _split3(g):
    g1 = g.astype(BF16).astype(F32)
    r1 = g - g1
    g2 = r1.astype(BF16).astype(F32)
    g3 = (r1 - g2).astype(BF16).astype(F32)
    return g1, g2, g3


HG_SUB = 16
HGRN_ROWS_PER_STEP = 32


def _hgrn_chunk_fn(load, store, st_ref, c_ref, v_ref, lbl_ref, ng_ref, units, chunk, valid_rows):
    n_tiles = chunk // SUBLANES

    lbl = lbl_ref[...]
    e = jnp.exp(lbl - jnp.max(lbl, axis=0, keepdims=True))
    lb_all = e[0:1, :] / jnp.sum(e, axis=0, keepdims=True)

    r_i = lax.broadcasted_iota(jnp.int32, (chunk, chunk), 0)
    c_i = lax.broadcasted_iota(jnp.int32, (chunk, chunk), 1)
    tril = (c_i <= r_i).astype(F32)
    row8 = lax.broadcasted_iota(jnp.int32, (SUBLANES, HG_DK), 0)
    ng = ng_ref[...]
    ones = jnp.ones((HG_DK, HG_DV), BF16)
    sub = min(chunk, HG_SUB)
    n_sub = chunk // sub
    pairs = [(s, i) for s in range(chunk) for i in range(s // SUBLANES, (s // sub + 1) * sub // SUBLANES)]
    if n_sub > 1:
        assert sub & (sub - 1) == 0 and chunk & (chunk - 1) == 0
        rows_sub = lax.broadcasted_iota(jnp.int32, (chunk, (n_sub - 1) * chunk), 0) >> (sub.bit_length() - 1)
        cols_blk = lax.broadcasted_iota(jnp.int32, (chunk, (n_sub - 1) * chunk), 1) >> (chunk.bit_length() - 1)
        sub_mask = cols_blk == rows_sub - 1

    def gates(u, start):
        h = units[u][1]
        lb = lb_all[:, h * HG_DK:(h + 1) * HG_DK]
        q = _silu(load(0, u, start))
        f = lb + (1.0 - lb) * _sigmoid(load(1, u, start))
        g = jnp.log(f)
        k = 1.0 - f
        v = load(2, u, start)
        valid = valid_rows(start)
        if valid is not None:
            g = jnp.where(valid, g, 0.0)
            k = jnp.where(valid, k, 0.0)
            v = jnp.where(valid, v, 0.0)
        g1, g2, g3 = _split3(g)
        b = _dot(tril, g1) + _dot(tril, g2) + _dot(tril, g3)
        return q, k, v, b * LOG2E

    def products(u, q, k, v, b2):
        c_ref[u] = b2 - jnp.log(k) * LOG2E
        v_ref[u] = v
        q_t = [q[i * SUBLANES:(i + 1) * SUBLANES, :] for i in range(n_tiles)]
        b_t = [b2[i * SUBLANES:(i + 1) * SUBLANES, :] for i in range(n_tiles)]
        p_tiles = []
        for s, i in pairs:
            d = b_t[i] - c_ref[u, s:s + 1, :]
            if i == s // SUBLANES and s % SUBLANES:
                d = jnp.where(row8 >= s % SUBLANES, d, NEG_BIG)
            p_tiles.append(q_t[i] * jnp.exp2(d))
        a_rep = _dot(jnp.concatenate(p_tiles, axis=0).astype(BF16), ones)

        a_sub = None
        if n_sub > 1:
            zeros_sub = jnp.zeros((sub, HG_DK), F32)
            q_parts, k_parts = [zeros_sub], []
            for j in range(1, n_sub):
                lo = j * sub
                r_j = b2[lo - 1:lo, :]
                q_parts.append(q[lo:lo + sub, :] * jnp.exp2(b2[lo:lo + sub, :] - r_j))
                k_parts += [k[:lo, :] * jnp.exp2(r_j - b2[:lo, :])] + [zeros_sub] * (n_sub - j)
            a_sub = _dot_nt(jnp.concatenate(q_parts, axis=0).astype(BF16),
                            jnp.concatenate(k_parts, axis=0).astype(BF16))

        st = st_ref[u]
        o_state = _dot_nt((q * jnp.exp2(b2)).astype(BF16), st.astype(BF16))
        bl = b2[chunk - 1:chunk, :]
        v_b = v.astype(BF16)
        st_ref[u] = st * jnp.exp2(bl) + _dot_tn(v_b, (k * jnp.exp2(bl - b2)).astype(BF16))
        return a_rep, a_sub, o_state, v_b

    def outputs(u, start, a_rep, a_sub, o_state, v_b):
        o_t = [o_state[i * SUBLANES:(i + 1) * SUBLANES, :] for i in range(n_tiles)]
        for j, (s, i) in enumerate(pairs):
            o_t[i] = o_t[i] + a_rep[j * SUBLANES:(j + 1) * SUBLANES, :] * v_ref[u, s:s + 1, :]
        o = jnp.concatenate(o_t, axis=0) if n_tiles > 1 else o_t[0]
        if a_sub is not None:
            o = o + _dot(jnp.where(sub_mask, a_sub, 0.0).astype(BF16),
                         jnp.concatenate([v_b] * (n_sub - 1), axis=0))
        store(u, start, _rms(o, ng) * _silu(load(3, u, start)))

    def chunk_fn(start, after_gates=None):
        gated = [gates(u, start) for u in range(len(units))]
        if after_gates is not None:
            after_gates()
        prods = [products(u, *gated[u]) for u in range(len(units))]
        for u in range(len(units)):
            outputs(u, start, *prods[u])

    return chunk_fn


def _hgrn_kernel(*refs, chunk, t_valid, has_s0):
    if has_s0:
        hq_ref, hf_ref, hi_ref, hg_ref, lbl_ref, ng_ref, s0_ref = refs[:7]
        refs = refs[7:]
    else:
        hq_ref, hf_ref, hi_ref, hg_ref, lbl_ref, ng_ref = refs[:6]
        s0_ref = None
        refs = refs[6:]
    o_ref, sfin_ref, st_ref, c_ref, v_ref = refs
    ti = pl.program_id(1)
    n_batch, t_tile = hq_ref.shape[:2]
    units = [divmod(u, HG_HEADS) for u in range(n_batch * HG_HEADS)]
    n_chunks = t_tile // chunk
    fields = (hq_ref, hf_ref, hi_ref, hg_ref)

    @pl.when(ti == 0)
    def _():
        for u, (e, h) in enumerate(units):
            st_ref[u] = s0_ref[e, h].T if has_s0 else jnp.zeros((HG_DV, HG_DK), F32)

    def where(u, start):
        e, h = units[u]
        return e, pl.ds(start, chunk), slice(h * HG_DK, (h + 1) * HG_DK)

    def load(field, u, start):
        return fields[field][where(u, start)]

    def store(u, start, value):
        o_ref[where(u, start)] = value.astype(o_ref.dtype)

    def valid_rows(start):
        if t_valid is None:
            return None
        pos = lax.broadcasted_iota(jnp.int32, (chunk, HG_DK), 0) + (ti * t_tile + start)
        return pos < t_valid

    chunk_fn = _hgrn_chunk_fn(load, store, st_ref, c_ref, v_ref, lbl_ref, ng_ref, units, chunk, valid_rows)

    def chunk_body(c, carry):
        chunk_fn(pl.multiple_of(c * chunk, chunk))
        return carry

    lax.fori_loop(0, n_chunks, chunk_body, 0, unroll=4 if n_chunks % 4 == 0 else 1)

    @pl.when(ti == pl.num_programs(1) - 1)
    def _():
        for u, (e, h) in enumerate(units):
            sfin_ref[e, h] = st_ref[u].T


def _hgrn(zh, lb_logits, ng, s0, chunk, t_valid):
    bsz, t, _ = zh.shape
    has_s0 = s0 is not None
    t_tile = min(t, HGRN_TILE_ROWS)
    assert t % t_tile == 0 and t_tile % chunk == 0
    n_batch = math.gcd(bsz, max(1, HGRN_ROWS_PER_STEP // t_tile))
    n_units = n_batch * HG_HEADS
    col = lambda j: pl.BlockSpec((n_batch, t_tile, HG_W), lambda b, i: (b, i, j))
    in_specs = [col(0), col(1), col(2), col(3),
                pl.BlockSpec(lb_logits.shape, lambda b, i: (0, 0)),
                pl.BlockSpec((1, HG_DV), lambda b, i: (0, 0))]
    args = [zh, zh, zh, zh, lb_logits, ng]
    st_spec = pl.BlockSpec((n_batch, HG_HEADS, HG_DK, HG_DV), lambda b, i: (b, 0, 0, 0))
    if has_s0:
        in_specs.append(st_spec)
        args.append(s0)
    return pl.pallas_call(
        functools.partial(_hgrn_kernel, chunk=chunk, t_valid=t_valid, has_s0=has_s0),
        grid=(bsz // n_batch, t // t_tile),
        in_specs=in_specs,
        out_specs=[pl.BlockSpec((n_batch, t_tile, HG_W), lambda b, i: (b, i, 0)), st_spec],
        out_shape=[jax.ShapeDtypeStruct((bsz, t, HG_W), BF16),
                   jax.ShapeDtypeStruct((bsz, HG_HEADS, HG_DK, HG_DV), F32)],
        scratch_shapes=[pltpu.VMEM((n_units, HG_DV, HG_DK), F32),
                        pltpu.VMEM((n_units, chunk, HG_DK), F32),
                        pltpu.VMEM((n_units, chunk, HG_DV), F32)],
        compiler_params=_params("parallel", "arbitrary"),
        name="hgrn",
    )(*args)


HGRN_PIECES = 8


def _hgrn_proj_kernel(xa_ref, xb_ref, x0_ref, g_ref, wh_ref, lbl_ref, ng_ref, o_ref, sfin_ref,
                      zh_ref, u_ref, st_ref, c_ref, v_ref, *, chunk, tiles_per_seq):
    g = pl.program_id(0)
    t_tile = xa_ref.shape[0]
    n_chunks = t_tile // chunk
    assert n_chunks == HGRN_PIECES and tiles_per_seq % 2 == 0
    piece = wh_ref.shape[1] // HGRN_PIECES
    units = [(0, h) for h in range(HG_HEADS)]

    def normed(x_ref):
        u_ref[...] = _rms(x_ref[...], g_ref[...]).astype(BF16)

    def project(slot, k):
        cols = slice(k * piece, (k + 1) * piece)
        zh_ref[slot, :, cols] = _dot(u_ref[...], wh_ref[:, cols])

    @pl.when(g == 0)
    def _():
        st_ref[...] = jnp.zeros_like(st_ref)
        normed(x0_ref)
        for k in range(HGRN_PIECES):
            project(0, k)

    def run_tile(slot, x_next_ref, row0, new_seq):
        if new_seq is not None:
            for u in range(len(units)):
                st_ref[u] = jnp.where(new_seq, 0.0, st_ref[u])

        def load(field, u, start):
            col = (field * HG_HEADS + units[u][1]) * HG_DK
            return zh_ref[slot, start:start + chunk, col:col + HG_DK]

        def store(u, start, value):
            h = units[u][1]
            o_ref[row0 + start:row0 + start + chunk, h * HG_DV:(h + 1) * HG_DV] = value.astype(o_ref.dtype)

        chunk_fn = _hgrn_chunk_fn(load, store, st_ref, c_ref, v_ref, lbl_ref, ng_ref, units, chunk,
                                  lambda start: None)
        normed(x_next_ref)
        for c in range(n_chunks):
            chunk_fn(c * chunk, functools.partial(project, 1 - slot, c))

    tile_a = 2 * g
    run_tile(0, xa_ref, 0, (tile_a % tiles_per_seq) == 0)
    run_tile(1, xb_ref, t_tile, None)

    @pl.when((tile_a + 1) % tiles_per_seq == tiles_per_seq - 1)
    def _():
        for u, (_, h) in enumerate(units):
            sfin_ref[h] = st_ref[u].T


def _hgrn_proj(x, g, w_h, lb_logits, ng, bsz, t, chunk):
    t_tile = chunk * HGRN_PIECES
    assert t % t_tile == 0
    tiles_per_seq = t // t_tile
    n_tiles = bsz * tiles_per_seq
    assert n_tiles % 2 == 0
    const = lambda a: pl.BlockSpec(a.shape, lambda i: (0, 0), pipeline_mode=pl.Buffered(1))
    tile = lambda f: pl.BlockSpec((t_tile, D_MODEL), lambda i: (jnp.minimum(f(i), n_tiles - 1), 0))
    return pl.pallas_call(
        functools.partial(_hgrn_proj_kernel, chunk=chunk, tiles_per_seq=tiles_per_seq),
        grid=(n_tiles // 2,),
        in_specs=[tile(lambda i: 2 * i + 1), tile(lambda i: 2 * i + 2), tile(lambda i: 0),
                  const(g), const(w_h), const(lb_logits), const(ng)],
        out_specs=[pl.BlockSpec((2 * t_tile, HG_W), lambda i: (i, 0)),
                   pl.BlockSpec((None, HG_HEADS, HG_DK, HG_DV), lambda i: ((2 * i + 1) // tiles_per_seq, 0, 0, 0))],
        out_shape=[jax.ShapeDtypeStruct((bsz * t, HG_W), BF16),
                   jax.ShapeDtypeStruct((bsz, HG_HEADS, HG_DK, HG_DV), F32)],
        scratch_shapes=[pltpu.VMEM((2, t_tile, 4 * HG_W), F32), pltpu.VMEM((t_tile, D_MODEL), BF16),
                        pltpu.VMEM((HG_HEADS, HG_DV, HG_DK), F32),
                        pltpu.VMEM((HG_HEADS, chunk, HG_DK), F32),
                        pltpu.VMEM((HG_HEADS, chunk, HG_DV), F32)],
        compiler_params=_params("arbitrary"),
        name="hgrn_proj",
    )(x, x, x, g, w_h, lb_logits, ng)


ATTN_SCALE = 1.0 / math.sqrt(QK_NOPE + QK_ROPE)
ATTN_EXP2_SCALE = ATTN_SCALE * LOG2E


def _attn_prompt_kernel(ql_ref, qr_ref, kc_ref, kr_ref, wuv_ref, o_ref,
                        m_ref, l_ref, a_ref, acc_ref, s_ref, p_ref, *, tq):
    qi = pl.program_id(1)
    ql = jnp.concatenate([ql_ref[:, h * KV_LORA:(h + 1) * KV_LORA] for h in range(MLA_HEADS)], axis=0)
    qr = jnp.concatenate([qr_ref[:, h * QK_ROPE:(h + 1) * QK_ROPE] for h in range(MLA_HEADS)], axis=0)
    m_ref[...] = jnp.full_like(m_ref, NEG_BIG)
    l_ref[...] = jnp.zeros_like(l_ref)
    a_ref[...] = jnp.zeros_like(a_ref)
    acc_ref[...] = jnp.zeros_like(acc_ref)
    p_ref[1] = jnp.zeros(p_ref.shape[1:], BF16)

    def keys(kb):
        return pl.ds(pl.multiple_of(kb * tq, tq), tq)

    def scores(kb):
        return (_dot_nt(ql, kc_ref[keys(kb), :]) + _dot_nt(qr, kr_ref[keys(kb), :])) * ATTN_EXP2_SCALE

    def add_values(kb, slot):
        acc_ref[...] = (jnp.tile(a_ref[...], (1, KV_LORA // LANES)) * acc_ref[...]
                        + _dot(p_ref[slot], kc_ref[keys(kb), :]))

    def softmax(slot, masked):
        s = s_ref[slot]
        if masked:
            rows = lax.broadcasted_iota(jnp.int32, s.shape, 0) & (tq - 1)
            cols = lax.broadcasted_iota(jnp.int32, s.shape, 1)
            s = jnp.where(cols <= rows, s, NEG_BIG)
        m_prev = m_ref[...]
        m_new = jnp.maximum(m_prev, jnp.max(s, axis=-1, keepdims=True))
        alpha = jnp.exp2(m_prev - m_new)
        p = jnp.exp2(s - jnp.tile(m_new, (1, tq // LANES)))
        l_ref[...] = alpha * l_ref[...] + jnp.sum(p, axis=-1, keepdims=True)
        m_ref[...] = m_new
        a_ref[...] = alpha
        p_ref[slot] = p.astype(BF16)

    s_ref[0] = scores(0)

    def iteration(i, cur):
        s_ref[1 - cur] = scores(i + 1)
        add_values(jnp.maximum(i - 1, 0), 1 - cur)
        softmax(cur, False)

    def last(cur):
        add_values(jnp.maximum(qi - 1, 0), 1 - cur)
        softmax(cur, True)
        add_values(qi, cur)

    def pair(j, carry):
        iteration(2 * j, 0)
        iteration(2 * j + 1, 1)
        return carry

    lax.fori_loop(0, qi >> 1, pair, 0)

    @pl.when((qi & 1) == 1)
    def _():
        iteration(qi - 1, 0)
        last(1)

    @pl.when((qi & 1) == 0)
    def _():
        last(0)

    o_lat =(acc_ref[...] / jnp.tile(l_ref[...], (1, KV_LORA // LANES))).astype(BF16)
    for h in range(MLA_HEADS):
        o_ref[:, h * V_DIM:(h + 1) * V_DIM] = _dot(o_lat[h * tq:(h + 1) * tq, :], wuv_ref[h]).astype(o_ref.dtype)


def _attn_prompt(qlat, qrope, ckv_b, kr_b, w_uv, bsz, t):
    tq = ATTN_Q_ROWS
    assert tq & (tq - 1) == 0 and t % tq == 0
    nq = t // tq
    rows = MLA_HEADS * tq
    return pl.pallas_call(
        functools.partial(_attn_prompt_kernel, tq=tq),
        grid=(bsz, nq),
        in_specs=[
            pl.BlockSpec((tq, MLA_HEADS * KV_LORA), lambda b, i: (b * nq + i, 0)),
            pl.BlockSpec((tq, MLA_HEADS * QK_ROPE), lambda b, i: (b * nq + i, 0)),
            pl.BlockSpec((t, KV_LORA), lambda b, i: (b, 0)),
            pl.BlockSpec((t, QK_ROPE), lambda b, i: (b, 0)),
            pl.BlockSpec(w_uv.shape, lambda b, i: (0, 0, 0)),
        ],
        out_specs=pl.BlockSpec((tq, MLA_HEADS * V_DIM), lambda b, i: (b * nq + i, 0)),
        out_shape=jax.ShapeDtypeStruct((bsz * t, MLA_HEADS * V_DIM), BF16),
        scratch_shapes=[pltpu.VMEM((rows, LANES), F32), pltpu.VMEM((rows, LANES), F32),
                        pltpu.VMEM((rows, LANES), F32), pltpu.VMEM((rows, KV_LORA), F32),
                        pltpu.VMEM((2, rows, tq), F32), pltpu.VMEM((2, rows, tq), BF16)],
        compiler_params=_params("parallel", "parallel"),
        name="attn_prompt",
    )(qlat, qrope, ckv_b, kr_b, w_uv)


PAGES_PER_STEP = 16
PAGE_RING = 4
NEW_KEY_ROWS = 128


def _attn_sample_kernel(pt_ref, ql_ref, qr_ref, nc_ref, nr_ref, wuv_ref, cc_hbm, cr_hbm, o_ref,
                        cbuf, rbuf, sem, wq_ref, wr_ref, m_ref, l_ref, acc_ref, kv_ref, kq_ref, kr_ref,
                        *, t_new, n_q, n_steps):
    b = pl.program_id(0)
    nb = pl.num_programs(0)
    groups = LANES // n_q
    page = cbuf.shape[1] // PAGES_PER_STEP
    ppg = PAGES_PER_STEP // groups
    lat_rep = KV_LORA // LANES

    def page_copies(bb, step, slot):
        copies = []
        for i in range(PAGES_PER_STEP):
            pid = pt_ref[bb, step * PAGES_PER_STEP + i]
            copies.append(pltpu.make_async_copy(cc_hbm.at[pid], cbuf.at[slot, pl.ds(i * page, page), :],
                                                sem.at[slot]))
            copies.append(pltpu.make_async_copy(cr_hbm.at[pid], rbuf.at[slot, i], sem.at[slot]))
        return copies

    @pl.when(b == 0)
    def _():
        for step in range(PAGE_RING):
            for c in page_copies(0, step, step):
                c.start()

    q_shift = n_q.bit_length() - 1
    spread = (lax.broadcasted_iota(jnp.int32, (n_q, LANES), 1) & (n_q - 1)) == lax.broadcasted_iota(
        jnp.int32, (n_q, LANES), 0)
    spread = jnp.where(spread, 1.0, 0.0).astype(BF16)
    for src, dst in ((ql_ref, wq_ref), (qr_ref, wr_ref)):
        feat = src.shape[0]
        tiled = _dot(src[...], spread)
        lane_g = lax.broadcasted_iota(jnp.int32, tiled.shape, 1) >> q_shift
        for g in range(groups):
            dst[g * feat:(g + 1) * feat, :] = jnp.where(lane_g == g, tiled, 0.0).astype(BF16)

    m_ref[...] = jnp.full_like(m_ref, NEG_BIG)
    l_ref[...] = jnp.zeros_like(l_ref)
    acc_ref[...] = jnp.zeros_like(acc_ref)

    def lanes_to_rows(row):
        return jnp.broadcast_to(row, (LANES, LANES)).T

    def update(s, place, values):
        m_prev = m_ref[...]
        m_new = jnp.maximum(m_prev, jnp.max(s, axis=0, keepdims=True))
        alpha = jnp.exp2(m_prev - m_new)
        p = jnp.exp2(s - m_new)
        l_ref[...] = alpha * l_ref[...] + jnp.sum(p, axis=0, keepdims=True)
        m_ref[...] = m_new
        pv = _dot(place(p.T), values)
        acc_ref[...] = jnp.tile(lanes_to_rows(alpha), (1, lat_rep)) * acc_ref[...] + pv

    def block_diag(p_t):
        row_g = lax.broadcasted_iota(jnp.int32, p_t.shape, 0) >> q_shift
        return jnp.concatenate([jnp.where(row_g == g, p_t, 0.0).astype(BF16) for g in range(groups)], axis=1)

    def stage_and_score(j):
        ring = j % PAGE_RING
        slot = j & 1
        for c in page_copies(b, j, ring):
            c.wait()
        for i in range(PAGES_PER_STEP):
            g, r = divmod(i, ppg)
            x = cbuf[ring, i * page:(i + 1) * page, :].astype(BF16)
            kv_ref[slot, i * page:(i + 1) * page, :] = x
            kq_ref[slot, r * page:(r + 1) * page, g * KV_LORA:(g + 1) * KV_LORA] = x
            kr_ref[slot, g * QK_ROPE:(g + 1) * QK_ROPE, r * page:(r + 1) * page] = rbuf[ring, i].astype(BF16)
        if j + PAGE_RING < n_steps:
            nxt = page_copies(b, j + PAGE_RING, ring)
        else:
            nxt = page_copies(jnp.minimum(b + 1, nb - 1), j + PAGE_RING - n_steps, ring)
        for c in nxt:
            c.start()
        return (_dot(kq_ref[slot], wq_ref[...]) + _dot_tn(kr_ref[slot], wr_ref[...])) * ATTN_EXP2_SCALE

    s_cur = stage_and_score(0)
    for j in range(n_steps):
        s_next = stage_and_score(j + 1) if j + 1 < n_steps else None
        update(s_cur, block_diag, kv_ref[j & 1])
        s_cur = s_next

    @pl.when(b == nb - 1)
    def _():
        for step in range(PAGE_RING):
            for c in page_copies(b, step, step):
                c.wait()

    nc = nc_ref[...]
    s2 = (_dot(nc, wq_ref[:KV_LORA, :]) + _dot(nr_ref[...], wr_ref[:QK_ROPE, :])) * ATTN_EXP2_SCALE
    t_k = lax.broadcasted_iota(jnp.int32, s2.shape, 0)
    lane = lax.broadcasted_iota(jnp.int32, s2.shape, 1)
    ok = (lane < n_q) & (t_k <= (lane & (t_new - 1)))
    update(jnp.where(ok, s2, NEG_BIG), lambda p_t: p_t.astype(BF16), nc)

    grp = lambda a, g: a[g * n_q:(g + 1) * n_q]
    total = lambda a: functools.reduce(lambda x, y: x + y, [grp(a, g) for g in range(groups)])
    m_t = lanes_to_rows(m_ref[...])
    l_t = lanes_to_rows(l_ref[...])
    m_q = functools.reduce(jnp.maximum, [grp(m_t, g) for g in range(groups)])
    w = jnp.exp2(m_t - jnp.tile(m_q, (groups, 1)))
    l_q = total(w * l_t)
    o_lat = total(jnp.tile(w, (1, lat_rep)) * acc_ref[...]) / jnp.tile(l_q, (1, lat_rep))
    o_lat = o_lat.astype(BF16)
    head = lax.broadcasted_iota(jnp.int32, (n_q, V_DIM), 0) >> (t_new.bit_length() - 1)
    out = jnp.zeros((n_q, V_DIM), F32)
    for h in range(MLA_HEADS):
        out = out + jnp.where(head == h, _dot(o_lat, wuv_ref[h]), 0.0)
    o_ref[...] = out.astype(o_ref.dtype)


def _attn_sample(ql_t, qr_t, new_c, new_r, w_uv, cache_c, cache_r_t, page_table, t_new):
    bsz, _, n_q = ql_t.shape
    n_pages = page_table.shape[1]
    groups = LANES // n_q
    assert LANES % n_q == 0 and PAGES_PER_STEP % groups == 0 and n_pages % PAGES_PER_STEP == 0
    assert t_new & (t_new - 1) == 0 and n_q & (n_q - 1) == 0 and t_new <= NEW_KEY_ROWS and n_q >= SUBLANES
    page = cache_c.shape[1]
    keys = PAGES_PER_STEP * page
    n_steps = n_pages // PAGES_PER_STEP
    assert n_steps % PAGE_RING == 0

    per_b = lambda a: pl.BlockSpec((None,) + a.shape[1:], lambda b, pt: (b, 0, 0))
    grid_spec = pltpu.PrefetchScalarGridSpec(
        num_scalar_prefetch=1,
        grid=(bsz,),
        in_specs=[per_b(ql_t), per_b(qr_t), per_b(new_c), per_b(new_r),
                  pl.BlockSpec(w_uv.shape, lambda b, pt: (0, 0, 0)),
                  pl.BlockSpec(memory_space=pl.ANY), pl.BlockSpec(memory_space=pl.ANY)],
        out_specs=pl.BlockSpec((None, n_q, V_DIM), lambda b, pt: (b, 0, 0)),
        scratch_shapes=[pltpu.VMEM((PAGE_RING, keys, KV_LORA), cache_c.dtype),
                        pltpu.VMEM((PAGE_RING, PAGES_PER_STEP, QK_ROPE, page), cache_r_t.dtype),
                        pltpu.SemaphoreType.DMA((PAGE_RING,)),
                        pltpu.VMEM((groups * KV_LORA, LANES), BF16),
                        pltpu.VMEM((groups * QK_ROPE, LANES), BF16),
                        pltpu.VMEM((1, LANES), F32), pltpu.VMEM((1, LANES), F32),
                        pltpu.VMEM((LANES, KV_LORA), F32),
                        pltpu.VMEM((2, keys, KV_LORA), BF16),
                        pltpu.VMEM((2, keys // groups, groups * KV_LORA), BF16),
                        pltpu.VMEM((2, groups * QK_ROPE, keys // groups), BF16)],
    )
    return pl.pallas_call(
        functools.partial(_attn_sample_kernel, t_new=t_new, n_q=n_q, n_steps=n_steps),
        grid_spec=grid_spec,
        out_shape=jax.ShapeDtypeStruct((bsz, n_q, V_DIM), BF16),
        compiler_params=_params("arbitrary"),
        name="attn_sample",
    )(page_table, ql_t, qr_t, new_c, new_r, w_uv, cache_c, cache_r_t)


def _rot_cols(w):
    half = w.shape[-1] // 2
    return jnp.concatenate([-w[..., half:], w[..., :half]], axis=-1)


def _prep_weights(norm_g, w_ffn_gate, w_ffn_up, w_ffn_down, w_in, q_norm_g, w_q_up, kv_norm_g,
                  w_kv_up, hg_norm_g, w_out):
    bf = lambda a: a.astype(BF16)
    w_h, w_cq, w_ckv, w_kr = jnp.split(w_in, [4 * HG_W, 4 * HG_W + Q_LORA, 4 * HG_W + Q_LORA + KV_LORA], axis=-1)
    w_q_rope = w_q_up[..., QK_NOPE:]
    return {
        "norm_g": norm_g.reshape(-1, 1, D_MODEL),
        "wg": bf(w_ffn_gate), "wu": bf(w_ffn_up), "wd": bf(w_ffn_down),
        "w_h": bf(w_h), "w_cq": bf(w_cq), "w_ckv": bf(w_ckv),
        "w_kr": bf(jnp.concatenate([w_kr, _rot_cols(w_kr)], axis=-1)),
        "q_norm_g": q_norm_g.reshape(1, Q_LORA), "kv_norm_g": kv_norm_g.reshape(1, KV_LORA),
        "w_qn": bf(w_q_up[..., :QK_NOPE].reshape(Q_LORA, MLA_HEADS * QK_NOPE)),
        "w_qr": bf(w_q_rope.reshape(Q_LORA, MLA_HEADS * QK_ROPE)),
        "w_qrr": bf(_rot_cols(w_q_rope).reshape(Q_LORA, MLA_HEADS * QK_ROPE)),
        "w_uk": bf(jnp.transpose(w_kv_up[..., :QK_NOPE], (1, 2, 0))),
        "w_uv": bf(jnp.transpose(w_kv_up[..., QK_NOPE:], (1, 0, 2))),
        "hg_norm_g": hg_norm_g.reshape(1, HG_DV),
        "w_out1": bf(w_out[:HG_W]), "w_out2": bf(w_out[HG_W:]),
    }


def _rope_tables(pos):
    half = QK_ROPE // 2
    inv = ROPE_THETA ** (-jnp.arange(half, dtype=F32) / half)
    ang = pos.astype(F32)[:, None] * inv[None, :]
    cos = jnp.tile(jnp.cos(ang), (1, 2))
    sin = jnp.tile(jnp.sin(ang), (1, 2))
    return (jnp.concatenate([cos, sin], axis=-1), jnp.tile(cos, (1, MLA_HEADS)), jnp.tile(sin, (1, MLA_HEADS)))


def kernel(x_prompt, x_sample, cache_kv_latent, cache_k_rope, state_hgrn, page_table, hgrn_lb_logits,
           norm_g, w_ffn_gate, w_ffn_up, w_ffn_down, w_in, q_norm_g, w_q_up, kv_norm_g, w_kv_up,
           hg_norm_g, w_out):
    bp, tp, _ = x_prompt.shape
    bs, ts, _ = x_sample.shape
    depth = norm_g.shape[0]
    assert depth == 1
    past_len = page_table.shape[1] * cache_kv_latent.shape[2]
    w = _prep_weights(norm_g[0], w_ffn_gate[0], w_ffn_up[0], w_ffn_down[0], w_in[0], q_norm_g[0],
                      w_q_up[0], kv_norm_g[0], w_kv_up[0], hg_norm_g[0], w_out[0])
    ng = w["norm_g"]
    lb_logits = hgrn_lb_logits.astype(F32)

    def pre(x, tables, with_h):
        h1 = _ffn_block(x, ng[0], ng[1], w["wg"], w["wu"], w["wd"], 0)
        return (h1,) + tuple(_in_proj(h1, ng[2], *tables, w, with_h))

    def post(h1, o_h, o_m):
        return _ffn_block(h1, ng[4], ng[5], w["wg"], w["wu"], w["wd"], 1,
                          mix=(o_h, o_m, w["w_out1"], w["w_out2"], ng[3]))

    h1, ckv, kr, ckv_b, kr_b, qlat, qrope = pre(
        x_prompt.reshape(bp * tp, D_MODEL), _rope_tables(jnp.arange(tp, dtype=jnp.int32)), False)
    o_h, st_p = _hgrn_proj(h1, ng[2], w["w_h"], lb_logits, w["hg_norm_g"], bp, tp, chunk=HGRN_CHUNK)
    o_m = _attn_prompt(qlat, qrope, ckv_b, kr_b, w["w_uv"], bp, tp)
    y_p = post(h1, o_h, o_m).reshape(bp, tp, D_MODEL)
    ckv_p = ckv.reshape(1, bp, tp, KV_LORA)
    kr_p = kr.reshape(1, bp, tp, QK_ROPE)

    pos_s = past_len + jnp.arange(ts, dtype=jnp.int32)
    tabs = tuple(jnp.tile(a, (bs, 1)) for a in _rope_tables(pos_s))
    h1s, ckvs, krs, ckvs_b, krs_b, qlats, qropes, zhs = pre(x_sample.reshape(bs * ts, D_MODEL), tabs, True)
    t_pad = -(-ts // SUBLANES) * SUBLANES
    zhs_pad = jnp.pad(zhs.reshape(bs, ts, 4 * HG_W), ((0, 0), (0, t_pad - ts), (0, 0)))
    o_hs, st_s = _hgrn(zhs_pad, lb_logits, w["hg_norm_g"], state_hgrn[0], chunk=SUBLANES,
                       t_valid=ts if t_pad != ts else None)
    q_cols = lambda a, wd: a.reshape(bs, ts, MLA_HEADS, wd).transpose(0, 3, 2, 1).reshape(bs, wd, MLA_HEADS * ts)
    pad_new = lambda a, wd: jnp.pad(a.reshape(bs, ts, wd), ((0, 0), (0, NEW_KEY_ROWS - ts), (0, 0)))
    o_ms = _attn_sample(q_cols(qlats, KV_LORA), q_cols(qropes, QK_ROPE),
                        pad_new(ckvs_b, KV_LORA), pad_new(krs_b, QK_ROPE), w["w_uv"],
                        cache_kv_latent[0], jnp.swapaxes(cache_k_rope[0], 1, 2), page_table, ts)
    o_ms = o_ms.reshape(bs, MLA_HEADS, ts, V_DIM).transpose(0, 2, 1, 3).reshape(bs * ts, MLA_HEADS * V_DIM)
    y_s = post(h1s, o_hs[:, :ts].reshape(bs * ts, HG_W), o_ms).reshape(bs, ts, D_MODEL)

    return (y_p, y_s, ckv_p, kr_p, st_p[None], ckvs.reshape(1, bs, ts, KV_LORA),
            krs.reshape(1, bs, ts, QK_ROPE), st_s[None])
```

```python
import functools
import math

import jax
import jax.numpy as jnp
from jax import lax
from jax.experimental import pallas as pl
from jax.experimental.pallas import tpu as pltpu

D_MODEL = 1024
D_FF = 2816
HG_HEADS = 4
HG_DK = 128
HG_DV = 128
MLA_HEADS = 4
QK_NOPE = 128
QK_ROPE = 64
V_DIM = 128
Q_LORA = 768
KV_LORA = 256
ROPE_THETA = 10000.0
EPS = 1e-6
HG_W = HG_HEADS * HG_DK

VMEM_LIMIT_BYTES = 56 * 1024 * 1024
SUBLANES = 8
LANES = 128
IN_PROJ_ROWS = 512
HGRN_CHUNK = 64
HGRN_TILE_ROWS = 512
ATTN_Q_ROWS = 256
NEG_BIG = -1e30
LOG2E = math.log2(math.e)

F32 = jnp.float32
BF16 = jnp.bfloat16


def _dot(a, b):
    return jnp.dot(a, b, preferred_element_type=F32)


def _dot_nt(a, b):
    return lax.dot_general(a, b, (((1,), (1,)), ((), ())), preferred_element_type=F32)


def _dot_tn(a, b):
    return lax.dot_general(a, b, (((0,), (0,)), ((), ())), preferred_element_type=F32)


def _rms(x, g):
    return x * lax.rsqrt(jnp.mean(x * x, axis=-1, keepdims=True) + EPS) * g


def _sigmoid(x):
    return 1.0 / (1.0 + jnp.exp(-x))


def _silu(x):
    return x * _sigmoid(x)


def _params(*sem):
    return pltpu.CompilerParams(dimension_semantics=sem, vmem_limit_bytes=VMEM_LIMIT_BYTES)


FFN_CHUNK = 256
FFN_SUB_ROWS = 512


def _ffn_kernel(*refs, mix_in, norm_out):
    if mix_in:
        h_ref, oh_ref, om_ref, w1_ref, w2_ref, gmix_ref = refs[:6]
        refs = refs[6:]
    else:
        h_ref = refs[0]
        refs = refs[1:]
    if norm_out:
        gpre_ref, gpost_ref, wg_ref, wu_ref, wd_ref, gnext_ref, o_ref, u_ref = refs
    else:
        gpre_ref, gpost_ref, wg_ref, wu_ref, wd_ref, o_ref = refs
    tm = h_ref.shape[0]
    sub = min(tm, FFN_SUB_ROWS)
    tiles = [slice(r * sub, (r + 1) * sub) for r in range(tm // sub)]
    if mix_in:
        mixed = [_dot(oh_ref[rows, :], w1_ref[...]) + _dot(om_ref[rows, :], w2_ref[...]) for rows in tiles]
        xs = [h_ref[rows, :] + _rms(m, gmix_ref[...]) for rows, m in zip(tiles, mixed)]
    else:
        xs = [h_ref[rows, :] for rows in tiles]
    xns = [_rms(x, gpre_ref[...]).astype(BF16) for x in xs]
    for rows, x, xn in zip(tiles, xs, xns):
        acc = None
        for f in range(D_FF // FFN_CHUNK):
            cols = slice(f * FFN_CHUNK, (f + 1) * FFN_CHUNK)
            a = _dot(xn, wg_ref[:, cols])
            b = _dot(xn, wu_ref[:, cols])
            part = _dot((_silu(a) * b).astype(BF16), wd_ref[cols, :])
            acc = part if acc is None else acc + part
        out = x + 0.5 * _rms(acc, gpost_ref[...])
        o_ref[rows, :] = out
        if norm_out:
            u_ref[rows, :] = _rms(out, gnext_ref[...]).astype(BF16)


def _ffn_block(x, g_pre, g_post, wg, wu, wd, which, mix=None, g_next=None):
    m = x.shape[0]
    tm = min(m, 2 * FFN_SUB_ROWS)

    def const(a):
        if a.ndim == 2:
            return pl.BlockSpec(a.shape, lambda i: (0, 0), pipeline_mode=pl.Buffered(1))
        return pl.BlockSpec((None,) + a.shape[1:], lambda i: (which, 0, 0), pipeline_mode=pl.Buffered(1))

    row = lambda a: pl.BlockSpec((tm, a.shape[1]), lambda i: (i, 0))
    args, in_specs = [x], [row(x)]
    if mix is not None:
        o_h, o_m, w1, w2, g_mix = mix
        args += [o_h, o_m, w1, w2, g_mix]
        in_specs += [row(o_h), row(o_m), const(w1), const(w2), const(g_mix)]
    args += [g_pre, g_post, wg, wu, wd]
    in_specs += [const(g_pre), const(g_post), const(wg), const(wu), const(wd)]
    out_specs, out_shape = row(x), jax.ShapeDtypeStruct((m, D_MODEL), F32)
    if g_next is not None:
        args.append(g_next)
        in_specs.append(const(g_next))
        out_specs = [out_specs, row(x)]
        out_shape = [out_shape, jax.ShapeDtypeStruct((m, D_MODEL), BF16)]
    return pl.pallas_call(
        functools.partial(_ffn_kernel, mix_in=mix is not None, norm_out=g_next is not None),
        grid=(m // tm,),
        in_specs=in_specs,
        out_specs=out_specs,
        out_shape=out_shape,
        compiler_params=_params("parallel"),
        name="ffn_mix_block" if mix is not None else "ffn_block",
    )(*args)


def _inproj_kernel(u_ref, csk_ref, cosq_ref, sinq_ref, wcq_ref, wckv_ref, wkr_ref,
                   qg_ref, kvg_ref, wqn_ref, wqr_ref, wqrr_ref, wuk_ref, *rest, with_h):
    if with_h:
        wh_ref, ckv_ref, kr_ref, ckvb_ref, krb_ref, qlat_ref, qrope_ref, zh_ref = rest
        half = wh_ref.shape[1] // 2
    else:
        ckv_ref, kr_ref, ckvb_ref, krb_ref, qlat_ref, qrope_ref = rest
    u = u_ref[...]
    cq = _dot(u, wcq_ref[...])
    ckv = _dot(u, wckv_ref[...])
    kz = _dot(u, wkr_ref[...])
    if with_h:
        zh_ref[:, :half] = _dot(u, wh_ref[:, :half])
    cqn = _rms(cq, qg_ref[...]).astype(BF16)
    qn = _dot(cqn, wqn_ref[...])
    qr_a = _dot(cqn, wqr_ref[...])
    qr_b = _dot(cqn, wqrr_ref[...])
    if with_h:
        zh_ref[:, half:] = _dot(u, wh_ref[:, half:])
    qn = qn.astype(BF16)
    for h in range(MLA_HEADS):
        qlat_ref[:, h * KV_LORA:(h + 1) * KV_LORA] = _dot(
            qn[:, h * QK_NOPE:(h + 1) * QK_NOPE], wuk_ref[h]).astype(BF16)
    ckv = _rms(ckv, kvg_ref[...])
    ckv_ref[...] = ckv
    ckvb_ref[...] = ckv.astype(BF16)
    csk = csk_ref[...]
    kr = kz[:, :QK_ROPE] * csk[:, :QK_ROPE] + kz[:, QK_ROPE:] * csk[:, QK_ROPE:]
    kr_ref[...] = kr
    krb_ref[...] = kr.astype(BF16)
    qrope_ref[...] = (qr_a * cosq_ref[...] + qr_b * sinq_ref[...]).astype(BF16)


def _in_proj(u, csk, cosq, sinq, w, with_h):
    m = u.shape[0]
    tm = min(m, IN_PROJ_ROWS)
    nrep = csk.shape[0] // tm
    full = lambda a: pl.BlockSpec(a.shape, lambda i: (0,) * a.ndim, pipeline_mode=pl.Buffered(1))
    row = lambda width: pl.BlockSpec((tm, width), lambda i: (i, 0))
    tab = lambda width: pl.BlockSpec((tm, width), lambda i: (i % nrep, 0))
    rope_w = MLA_HEADS * QK_ROPE
    names = ["w_cq", "w_ckv", "w_kr", "q_norm_g", "kv_norm_g", "w_qn", "w_qr", "w_qrr", "w_uk"]
    names += ["w_h"] if with_h else []
    outs = [(KV_LORA, F32), (QK_ROPE, F32), (KV_LORA, BF16), (QK_ROPE, BF16),
            (MLA_HEADS * KV_LORA, BF16), (rope_w, BF16)]
    outs += [(4 * HG_W, F32)] if with_h else []
    return pl.pallas_call(
        functools.partial(_inproj_kernel, with_h=with_h),
        grid=(m // tm,),
        in_specs=[row(D_MODEL), tab(2 * QK_ROPE), tab(rope_w), tab(rope_w)] + [full(w[n]) for n in names],
        out_specs=[row(width) for width, _ in outs],
        out_shape=[jax.ShapeDtypeStruct((m, width), dt) for width, dt in outs],
        compiler_params=_params("parallel"),
        name="in_proj",
    )(u, csk, cosq, sinq, *[w[n] for n in names])


def _split3(g):
    g1 = g.astype(BF16).astype(F32)
    r1 = g - g1
    g2 = r1.astype(BF16).astype(F32)
    g3 = (r1 - g2).astype(BF16).astype(F32)
    return g1, g2, g3


HG_SUB = 16
HGRN_ROWS_PER_STEP = 32


def _hgrn_chunk_fn(load, store, st_ref, c_ref, v_ref, lbl_ref, ng_ref, units, chunk, valid_rows):
    n_tiles = chunk // SUBLANES

    lbl = lbl_ref[...]
    e = jnp.exp(lbl - jnp.max(lbl, axis=0, keepdims=True))
    lb_all = e[0:1, :] / jnp.sum(e, axis=0, keepdims=True)

    r_i = lax.broadcasted_iota(jnp.int32, (chunk, chunk), 0)
    c_i = lax.broadcasted_iota(jnp.int32, (chunk, chunk), 1)
    tril = (c_i <= r_i).astype(F32)
    row8 = lax.broadcasted_iota(jnp.int32, (SUBLANES, HG_DK), 0)
    ng = ng_ref[...]
    ones = jnp.ones((HG_DK, HG_DV), BF16)
    sub = min(chunk, HG_SUB)
    n_sub = chunk // sub
    pairs = [(s, i) for s in range(chunk) for i in range(s // SUBLANES, (s // sub + 1) * sub // SUBLANES)]
    if n_sub > 1:
        assert sub & (sub - 1) == 0 and chunk & (chunk - 1) == 0
        rows_sub = lax.broadcasted_iota(jnp.int32, (chunk, (n_sub - 1) * chunk), 0) >> (sub.bit_length() - 1)
        cols_blk = lax.broadcasted_iota(jnp.int32, (chunk, (n_sub - 1) * chunk), 1) >> (chunk.bit_length() - 1)
        sub_mask = cols_blk == rows_sub - 1

    def gates(u, start):
        h = units[u][1]
        lb = lb_all[:, h * HG_DK:(h + 1) * HG_DK]
        q = _silu(load(0, u, start))
        f = lb + (1.0 - lb) * _sigmoid(load(1, u, start))
        g = jnp.log(f)
        k = 1.0 - f
        v = load(2, u, start)
        valid = valid_rows(start)
        if valid is not None:
            g = jnp.where(valid, g, 0.0)
            k = jnp.where(valid, k, 0.0)
            v = jnp.where(valid, v, 0.0)
        g1, g2, g3 = _split3(g)
        b = _dot(tril, g1) + _dot(tril, g2) + _dot(tril, g3)
        return q, k, v, b * LOG2E

    def products(u, q, k, v, b2):
        c_ref[u] = b2 - jnp.log(k) * LOG2E
        v_ref[u] = v
        q_t = [q[i * SUBLANES:(i + 1) * SUBLANES, :] for i in range(n_tiles)]
        b_t = [b2[i * SUBLANES:(i + 1) * SUBLANES, :] for i in range(n_tiles)]
        p_tiles = []
        for s, i in pairs:
            d = b_t[i] - c_ref[u, s:s + 1, :]
            if i == s // SUBLANES and s % SUBLANES:
                d = jnp.where(row8 >= s % SUBLANES, d, NEG_BIG)
            p_tiles.append(q_t[i] * jnp.exp2(d))
        a_rep = _dot(jnp.concatenate(p_tiles, axis=0).astype(BF16), ones)

        a_sub = None
        if n_sub > 1:
            zeros_sub = jnp.zeros((sub, HG_DK), F32)
            q_parts, k_parts = [zeros_sub], []
            for j in range(1, n_sub):
                lo = j * sub
                r_j = b2[lo - 1:lo, :]
                q_parts.append(q[lo:lo + sub, :] * jnp.exp2(b2[lo:lo + sub, :] - r_j))
                k_parts += [k[:lo, :] * jnp.exp2(r_j - b2[:lo, :])] + [zeros_sub] * (n_sub - j)
            a_sub = _dot_nt(jnp.concatenate(q_parts, axis=0).astype(BF16),
                            jnp.concatenate(k_parts, axis=0).astype(BF16))

        st = st_ref[u]
        o_state = _dot_nt((q * jnp.exp2(b2)).astype(BF16), st.astype(BF16))
        bl = b2[chunk - 1:chunk, :]
        v_b = v.astype(BF16)
        st_ref[u] = st * jnp.exp2(bl) + _dot_tn(v_b, (k * jnp.exp2(bl - b2)).astype(BF16))
        return a_rep, a_sub, o_state, v_b

    def outputs(u, start, a_rep, a_sub, o_state, v_b):
        o_t = [o_state[i * SUBLANES:(i + 1) * SUBLANES, :] for i in range(n_tiles)]
        for j, (s, i) in enumerate(pairs):
            o_t[i] = o_t[i] + a_rep[j * SUBLANES:(j + 1) * SUBLANES, :] * v_ref[u, s:s + 1, :]
        o = jnp.concatenate(o_t, axis=0) if n_tiles > 1 else o_t[0]
        if a_sub is not None:
            o = o + _dot(jnp.where(sub_mask, a_sub, 0.0).astype(BF16),
                         jnp.concatenate([v_b] * (n_sub - 1), axis=0))
        store(u, start, _rms(o, ng) * _silu(load(3, u, start)))

    def chunk_fn(start, after_gates=None):
        gated = [gates(u, start) for u in range(len(units))]
        if after_gates is not None:
            after_gates()
        prods = [products(u, *gated[u]) for u in range(len(units))]
        for u in range(len(units)):
            outputs(u, start, *prods[u])

    return chunk_fn


def _hgrn_kernel(*refs, chunk, t_valid, has_s0):
    if has_s0:
        hq_ref, hf_ref, hi_ref, hg_ref, lbl_ref, ng_ref, s0_ref = refs[:7]
        refs = refs[7:]
    else:
        hq_ref, hf_ref, hi_ref, hg_ref, lbl_ref, ng_ref = refs[:6]
        s0_ref = None
        refs = refs[6:]
    o_ref, sfin_ref, st_ref, c_ref, v_ref = refs
    ti = pl.program_id(1)
    n_batch, t_tile = hq_ref.shape[:2]
    units = [divmod(u, HG_HEADS) for u in range(n_batch * HG_HEADS)]
    n_chunks = t_tile // chunk
    fields = (hq_ref, hf_ref, hi_ref, hg_ref)

    @pl.when(ti == 0)
    def _():
        for u, (e, h) in enumerate(units):
            st_ref[u] = s0_ref[e, h].T if has_s0 else jnp.zeros((HG_DV, HG_DK), F32)

    def where(u, start):
        e, h = units[u]
        return e, pl.ds(start, chunk), slice(h * HG_DK, (h + 1) * HG_DK)

    def load(field, u, start):
        return fields[field][where(u, start)]

    def store(u, start, value):
        o_ref[where(u, start)] = value.astype(o_ref.dtype)

    def valid_rows(start):
        if t_valid is None:
            return None
        pos = lax.broadcasted_iota(jnp.int32, (chunk, HG_DK), 0) + (ti * t_tile + start)
        return pos < t_valid

    chunk_fn = _hgrn_chunk_fn(load, store, st_ref, c_ref, v_ref, lbl_ref, ng_ref, units, chunk, valid_rows)

    def chunk_body(c, carry):
        chunk_fn(pl.multiple_of(c * chunk, chunk))
        return carry

    lax.fori_loop(0, n_chunks, chunk_body, 0, unroll=4 if n_chunks % 4 == 0 else 1)

    @pl.when(ti == pl.num_programs(1) - 1)
    def _():
        for u, (e, h) in enumerate(units):
            sfin_ref[e, h] = st_ref[u].T


def _hgrn(zh, lb_logits, ng, s0, chunk, t_valid):
    bsz, t, _ = zh.shape
    has_s0 = s0 is not None
    t_tile = min(t, HGRN_TILE_ROWS)
    assert t % t_tile == 0 and t_tile % chunk == 0
    n_batch = math.gcd(bsz, max(1, HGRN_ROWS_PER_STEP // t_tile))
    n_units = n_batch * HG_HEADS
    col = lambda j: pl.BlockSpec((n_batch, t_tile, HG_W), lambda b, i: (b, i, j))
    in_specs = [col(0), col(1), col(2), col(3),
                pl.BlockSpec(lb_logits.shape, lambda b, i: (0, 0)),
                pl.BlockSpec((1, HG_DV), lambda b, i: (0, 0))]
    args = [zh, zh, zh, zh, lb_logits, ng]
    st_spec = pl.BlockSpec((n_batch, HG_HEADS, HG_DK, HG_DV), lambda b, i: (b, 0, 0, 0))
    if has_s0:
        in_specs.append(st_spec)
        args.append(s0)
    return pl.pallas_call(
        functools.partial(_hgrn_kernel, chunk=chunk, t_valid=t_valid, has_s0=has_s0),
        grid=(bsz // n_batch, t // t_tile),
        in_specs=in_specs,
        out_specs=[pl.BlockSpec((n_batch, t_tile, HG_W), lambda b, i: (b, i, 0)), st_spec],
        out_shape=[jax.ShapeDtypeStruct((bsz, t, HG_W), BF16),
                   jax.ShapeDtypeStruct((bsz, HG_HEADS, HG_DK, HG_DV), F32)],
        scratch_shapes=[pltpu.VMEM((n_units, HG_DV, HG_DK), F32),
                        pltpu.VMEM((n_units, chunk, HG_DK), F32),
                        pltpu.VMEM((n_units, chunk, HG_DV), F32)],
        compiler_params=_params("parallel", "arbitrary"),
        name="hgrn",
    )(*args)


HGRN_PIECES = 8


def _hgrn_proj_kernel(ua_ref, ub_ref, u0_ref, wh_ref, lbl_ref, ng_ref, o_ref, sfin_ref,
                      zh_ref, st_ref, c_ref, v_ref, *, chunk, tiles_per_seq):
    g = pl.program_id(0)
    t_tile = ua_ref.shape[0]
    n_chunks = t_tile // chunk
    assert n_chunks == HGRN_PIECES and tiles_per_seq % 2 == 0
    piece = wh_ref.shape[1] // HGRN_PIECES
    units = [(0, h) for h in range(HG_HEADS)]

    def project(u_ref, slot, k):
        cols = slice(k * piece, (k + 1) * piece)
        zh_ref[slot, :, cols] = _dot(u_ref[...], wh_ref[:, cols])

    @pl.when(g == 0)
    def _():
        st_ref[...] = jnp.zeros_like(st_ref)
        for k in range(HGRN_PIECES):
            project(u0_ref, 0, k)

    def run_tile(slot, u_next_ref, row0, new_seq):
        if new_seq is not None:
            for u in range(len(units)):
                st_ref[u] = jnp.where(new_seq, 0.0, st_ref[u])

        def load(field, u, start):
            col = (field * HG_HEADS + units[u][1]) * HG_DK
            return zh_ref[slot, start:start + chunk, col:col + HG_DK]

        def store(u, start, value):
            h = units[u][1]
            o_ref[row0 + start:row0 + start + chunk, h * HG_DV:(h + 1) * HG_DV] = value.astype(o_ref.dtype)

        chunk_fn = _hgrn_chunk_fn(load, store, st_ref, c_ref, v_ref, lbl_ref, ng_ref, units, chunk,
                                  lambda start: None)
        for c in range(n_chunks):
            chunk_fn(c * chunk, functools.partial(project, u_next_ref, 1 - slot, c))

    tile_a = 2 * g
    run_tile(0, ua_ref, 0, (tile_a % tiles_per_seq) == 0)
    run_tile(1, ub_ref, t_tile, None)

    @pl.when((tile_a + 1) % tiles_per_seq == tiles_per_seq - 1)
    def _():
        for u, (_, h) in enumerate(units):
            sfin_ref[h] = st_ref[u].T


def _hgrn_proj(u, w_h, lb_logits, ng, bsz, t, chunk):
    t_tile = chunk * HGRN_PIECES
    assert t % t_tile == 0
    tiles_per_seq = t // t_tile
    n_tiles = bsz * tiles_per_seq
    assert n_tiles % 2 == 0
    const = lambda a: pl.BlockSpec(a.shape, lambda i: (0, 0), pipeline_mode=pl.Buffered(1))
    tile = lambda f: pl.BlockSpec((t_tile, D_MODEL), lambda i: (jnp.minimum(f(i), n_tiles - 1), 0))
    return pl.pallas_call(
        functools.partial(_hgrn_proj_kernel, chunk=chunk, tiles_per_seq=tiles_per_seq),
        grid=(n_tiles // 2,),
        in_specs=[tile(lambda i: 2 * i + 1), tile(lambda i: 2 * i + 2), tile(lambda i: 0),
                  const(w_h), const(lb_logits), const(ng)],
        out_specs=[pl.BlockSpec((2 * t_tile, HG_W), lambda i: (i, 0)),
                   pl.BlockSpec((None, HG_HEADS, HG_DK, HG_DV), lambda i: ((2 * i + 1) // tiles_per_seq, 0, 0, 0))],
        out_shape=[jax.ShapeDtypeStruct((bsz * t, HG_W), BF16),
                   jax.ShapeDtypeStruct((bsz, HG_HEADS, HG_DK, HG_DV), F32)],
        scratch_shapes=[pltpu.VMEM((2, t_tile, 4 * HG_W), F32),
                        pltpu.VMEM((HG_HEADS, HG_DV, HG_DK), F32),
                        pltpu.VMEM((HG_HEADS, chunk, HG_DK), F32),
                        pltpu.VMEM((HG_HEADS, chunk, HG_DV), F32)],
        compiler_params=_params("arbitrary"),
        name="hgrn_proj",
    )(u, u, u, w_h, lb_logits, ng)


ATTN_SCALE = 1.0 / math.sqrt(QK_NOPE + QK_ROPE)
ATTN_EXP2_SCALE = ATTN_SCALE * LOG2E


def _attn_prompt_kernel(ql_ref, qr_ref, kc_ref, kr_ref, wuv_ref, o_ref,
                        m_ref, l_ref, a_ref, acc_ref, s_ref, p_ref, *, tq):
    qi = pl.program_id(1)
    ql = jnp.concatenate([ql_ref[:, h * KV_LORA:(h + 1) * KV_LORA] for h in range(MLA_HEADS)], axis=0)
    qr = jnp.concatenate([qr_ref[:, h * QK_ROPE:(h + 1) * QK_ROPE] for h in range(MLA_HEADS)], axis=0)
    m_ref[...] = jnp.full_like(m_ref, NEG_BIG)
    l_ref[...] = jnp.zeros_like(l_ref)
    a_ref[...] = jnp.zeros_like(a_ref)
    acc_ref[...] = jnp.zeros_like(acc_ref)
    p_ref[1] = jnp.zeros(p_ref.shape[1:], BF16)

    def keys(kb):
        return pl.ds(pl.multiple_of(kb * tq, tq), tq)

    def scores(kb):
        return (_dot_nt(ql, kc_ref[keys(kb), :]) + _dot_nt(qr, kr_ref[keys(kb), :])) * ATTN_EXP2_SCALE

    def add_values(kb, slot):
        acc_ref[...] = (jnp.tile(a_ref[...], (1, KV_LORA // LANES)) * acc_ref[...]
                        + _dot(p_ref[slot], kc_ref[keys(kb), :]))

    def softmax(slot, masked):
        s = s_ref[slot]
        if masked:
            rows = lax.broadcasted_iota(jnp.int32, s.shape, 0) & (tq - 1)
            cols = lax.broadcasted_iota(jnp.int32, s.shape, 1)
            s = jnp.where(cols <= rows, s, NEG_BIG)
        m_prev = m_ref[...]
        m_new = jnp.maximum(m_prev, jnp.max(s, axis=-1, keepdims=True))
        alpha = jnp.exp2(m_prev - m_new)
        p = jnp.exp2(s - jnp.tile(m_new, (1, tq // LANES)))
        l_ref[...] = alpha * l_ref[...] + jnp.sum(p, axis=-1, keepdims=True)
        m_ref[...] = m_new
        a_ref[...] = alpha
        p_ref[slot] = p.astype(BF16)

    s_ref[0] = scores(0)

    def iteration(i, cur):
        s_ref[1 - cur] = scores(i + 1)
        add_values(jnp.maximum(i - 1, 0), 1 - cur)
        softmax(cur, False)

    def last(cur):
        add_values(jnp.maximum(qi - 1, 0), 1 - cur)
        softmax(cur, True)
        add_values(qi, cur)

    def pair(j, carry):
        iteration(2 * j, 0)
        iteration(2 * j + 1, 1)
        return carry

    lax.fori_loop(0, qi >> 1, pair, 0)

    @pl.when((qi & 1) == 1)
    def _():
        iteration(qi - 1, 0)
        last(1)

    @pl.when((qi & 1) == 0)
    def _():
        last(0)

    o_lat =(acc_ref[...] / jnp.tile(l_ref[...], (1, KV_LORA // LANES))).astype(BF16)
    for h in range(MLA_HEADS):
        o_ref[:, h * V_DIM:(h + 1) * V_DIM] = _dot(o_lat[h * tq:(h + 1) * tq, :], wuv_ref[h]).astype(o_ref.dtype)


def _attn_prompt(qlat, qrope, ckv_b, kr_b, w_uv, bsz, t):
    tq = ATTN_Q_ROWS
    assert tq & (tq - 1) == 0 and t % tq == 0
    nq = t // tq
    rows = MLA_HEADS * tq
    return pl.pallas_call(
        functools.partial(_attn_prompt_kernel, tq=tq),
        grid=(bsz, nq),
        in_specs=[
            pl.BlockSpec((tq, MLA_HEADS * KV_LORA), lambda b, i: (b * nq + i, 0)),
            pl.BlockSpec((tq, MLA_HEADS * QK_ROPE), lambda b, i: (b * nq + i, 0)),
            pl.BlockSpec((t, KV_LORA), lambda b, i: (b, 0)),
            pl.BlockSpec((t, QK_ROPE), lambda b, i: (b, 0)),
            pl.BlockSpec(w_uv.shape, lambda b, i: (0, 0, 0)),
        ],
        out_specs=pl.BlockSpec((tq, MLA_HEADS * V_DIM), lambda b, i: (b * nq + i, 0)),
        out_shape=jax.ShapeDtypeStruct((bsz * t, MLA_HEADS * V_DIM), BF16),
        scratch_shapes=[pltpu.VMEM((rows, LANES), F32), pltpu.VMEM((rows, LANES), F32),
                        pltpu.VMEM((rows, LANES), F32), pltpu.VMEM((rows, KV_LORA), F32),
                        pltpu.VMEM((2, rows, tq), F32), pltpu.VMEM((2, rows, tq), BF16)],
        compiler_params=_params("parallel", "parallel"),
        name="attn_prompt",
    )(qlat, qrope, ckv_b, kr_b, w_uv)


PAGES_PER_STEP = 16
PAGE_RING = 4
NEW_KEY_ROWS = 128


def _attn_sample_kernel(pt_ref, ql_ref, qr_ref, nc_ref, nr_ref, wuv_ref, cc_hbm, cr_hbm, o_ref,
                        cbuf, rbuf, sem, wq_ref, wr_ref, m_ref, l_ref, acc_ref, kv_ref, kq_ref, kr_ref,
                        *, t_new, n_q, n_steps):
    b = pl.program_id(0)
    nb = pl.num_programs(0)
    groups = LANES // n_q
    page = cbuf.shape[1] // PAGES_PER_STEP
    ppg = PAGES_PER_STEP // groups
    lat_rep = KV_LORA // LANES

    def page_copies(bb, step, slot):
        copies = []
        for i in range(PAGES_PER_STEP):
            pid = pt_ref[bb, step * PAGES_PER_STEP + i]
            copies.append(pltpu.make_async_copy(cc_hbm.at[pid], cbuf.at[slot, pl.ds(i * page, page), :],
                                                sem.at[slot]))
            copies.append(pltpu.make_async_copy(cr_hbm.at[pid], rbuf.at[slot, i], sem.at[slot]))
        return copies

    @pl.when(b == 0)
    def _():
        for step in range(PAGE_RING):
            for c in page_copies(0, step, step):
                c.start()

    q_shift = n_q.bit_length() - 1
    spread = (lax.broadcasted_iota(jnp.int32, (n_q, LANES), 1) & (n_q - 1)) == lax.broadcasted_iota(
        jnp.int32, (n_q, LANES), 0)
    spread = jnp.where(spread, 1.0, 0.0).astype(BF16)
    for src, dst in ((ql_ref, wq_ref), (qr_ref, wr_ref)):
        feat = src.shape[0]
        tiled = _dot(src[...], spread)
        lane_g = lax.broadcasted_iota(jnp.int32, tiled.shape, 1) >> q_shift
        for g in range(groups):
            dst[g * feat:(g + 1) * feat, :] = jnp.where(lane_g == g, tiled, 0.0).astype(BF16)

    m_ref[...] = jnp.full_like(m_ref, NEG_BIG)
    l_ref[...] = jnp.zeros_like(l_ref)
    acc_ref[...] = jnp.zeros_like(acc_ref)

    def lanes_to_rows(row):
        return jnp.broadcast_to(row, (LANES, LANES)).T

    def update(s, place, values):
        m_prev = m_ref[...]
        m_new = jnp.maximum(m_prev, jnp.max(s, axis=0, keepdims=True))
        alpha = jnp.exp2(m_prev - m_new)
        p = jnp.exp2(s - m_new)
        l_ref[...] = alpha * l_ref[...] + jnp.sum(p, axis=0, keepdims=True)
        m_ref[...] = m_new
        pv = _dot(place(p.T), values)
        acc_ref[...] = jnp.tile(lanes_to_rows(alpha), (1, lat_rep)) * acc_ref[...] + pv

    def block_diag(p_t):
        row_g = lax.broadcasted_iota(jnp.int32, p_t.shape, 0) >> q_shift
        return jnp.concatenate([jnp.where(row_g == g, p_t, 0.0).astype(BF16) for g in range(groups)], axis=1)

    def stage_and_score(j):
        ring = j % PAGE_RING
        slot = j & 1
        for c in page_copies(b, j, ring):
            c.wait()
        for i in range(PAGES_PER_STEP):
            g, r = divmod(i, ppg)
            x = cbuf[ring, i * page:(i + 1) * page, :].astype(BF16)
            kv_ref[slot, i * page:(i + 1) * page, :] = x
            kq_ref[slot, r * page:(r + 1) * page, g * KV_LORA:(g + 1) * KV_LORA] = x
            kr_ref[slot, g * QK_ROPE:(g + 1) * QK_ROPE, r * page:(r + 1) * page] = rbuf[ring, i].astype(BF16)
        if j + PAGE_RING < n_steps:
            nxt = page_copies(b, j + PAGE_RING, ring)
        else:
            nxt = page_copies(jnp.minimum(b + 1, nb - 1), j + PAGE_RING - n_steps, ring)
        for c in nxt:
            c.start()
        return (_dot(kq_ref[slot], wq_ref[...]) + _dot_tn(kr_ref[slot], wr_ref[...])) * ATTN_EXP2_SCALE

    s_cur = stage_and_score(0)
    for j in range(n_steps):
        s_next = stage_and_score(j + 1) if j + 1 < n_steps else None
        update(s_cur, block_diag, kv_ref[j & 1])
        s_cur = s_next

    @pl.when(b == nb - 1)
    def _():
        for step in range(PAGE_RING):
            for c in page_copies(b, step, step):
                c.wait()

    nc = nc_ref[...]
    s2 = (_dot(nc, wq_ref[:KV_LORA, :]) + _dot(nr_ref[...], wr_ref[:QK_ROPE, :])) * ATTN_EXP2_SCALE
    t_k = lax.broadcasted_iota(jnp.int32, s2.shape, 0)
    lane = lax.broadcasted_iota(jnp.int32, s2.shape, 1)
    ok = (lane < n_q) & (t_k <= (lane & (t_new - 1)))
    update(jnp.where(ok, s2, NEG_BIG), lambda p_t: p_t.astype(BF16), nc)

    grp = lambda a, g: a[g * n_q:(g + 1) * n_q]
    total = lambda a: functools.reduce(lambda x, y: x + y, [grp(a, g) for g in range(groups)])
    m_t = lanes_to_rows(m_ref[...])
    l_t = lanes_to_rows(l_ref[...])
    m_q = functools.reduce(jnp.maximum, [grp(m_t, g) for g in range(groups)])
    w = jnp.exp2(m_t - jnp.tile(m_q, (groups, 1)))
    l_q = total(w * l_t)
    o_lat = total(jnp.tile(w, (1, lat_rep)) * acc_ref[...]) / jnp.tile(l_q, (1, lat_rep))
    o_lat = o_lat.astype(BF16)
    head = lax.broadcasted_iota(jnp.int32, (n_q, V_DIM), 0) >> (t_new.bit_length() - 1)
    out = jnp.zeros((n_q, V_DIM), F32)
    for h in range(MLA_HEADS):
        out = out + jnp.where(head == h, _dot(o_lat, wuv_ref[h]), 0.0)
    o_ref[...] = out.astype(o_ref.dtype)


def _attn_sample(ql_t, qr_t, new_c, new_r, w_uv, cache_c, cache_r_t, page_table, t_new):
    bsz, _, n_q = ql_t.shape
    n_pages = page_table.shape[1]
    groups = LANES // n_q
    assert LANES % n_q == 0 and PAGES_PER_STEP % groups == 0 and n_pages % PAGES_PER_STEP == 0
    assert t_new & (t_new - 1) == 0 and n_q & (n_q - 1) == 0 and t_new <= NEW_KEY_ROWS and n_q >= SUBLANES
    page = cache_c.shape[1]
    keys = PAGES_PER_STEP * page
    n_steps = n_pages // PAGES_PER_STEP
    assert n_steps % PAGE_RING == 0

    per_b = lambda a: pl.BlockSpec((None,) + a.shape[1:], lambda b, pt: (b, 0, 0))
    grid_spec = pltpu.PrefetchScalarGridSpec(
        num_scalar_prefetch=1,
        grid=(bsz,),
        in_specs=[per_b(ql_t), per_b(qr_t), per_b(new_c), per_b(new_r),
                  pl.BlockSpec(w_uv.shape, lambda b, pt: (0, 0, 0)),
                  pl.BlockSpec(memory_space=pl.ANY), pl.BlockSpec(memory_space=pl.ANY)],
        out_specs=pl.BlockSpec((None, n_q, V_DIM), lambda b, pt: (b, 0, 0)),
        scratch_shapes=[pltpu.VMEM((PAGE_RING, keys, KV_LORA), cache_c.dtype),
                        pltpu.VMEM((PAGE_RING, PAGES_PER_STEP, QK_ROPE, page), cache_r_t.dtype),
                        pltpu.SemaphoreType.DMA((PAGE_RING,)),
                        pltpu.VMEM((groups * KV_LORA, LANES), BF16),
                        pltpu.VMEM((groups * QK_ROPE, LANES), BF16),
                        pltpu.VMEM((1, LANES), F32), pltpu.VMEM((1, LANES), F32),
                        pltpu.VMEM((LANES, KV_LORA), F32),
                        pltpu.VMEM((2, keys, KV_LORA), BF16),
                        pltpu.VMEM((2, keys // groups, groups * KV_LORA), BF16),
                        pltpu.VMEM((2, groups * QK_ROPE, keys // groups), BF16)],
    )
    return pl.pallas_call(
        functools.partial(_attn_sample_kernel, t_new=t_new, n_q=n_q, n_steps=n_steps),
        grid_spec=grid_spec,
        out_shape=jax.ShapeDtypeStruct((bsz, n_q, V_DIM), BF16),
        compiler_params=_params("arbitrary"),
        name="attn_sample",
    )(page_table, ql_t, qr_t, new_c, new_r, w_uv, cache_c, cache_r_t)


def _rot_cols(w):
    half = w.shape[-1] // 2
    return jnp.concatenate([-w[..., half:], w[..., :half]], axis=-1)


def _prep_weights(norm_g, w_ffn_gate, w_ffn_up, w_ffn_down, w_in, q_norm_g, w_q_up, kv_norm_g,
                  w_kv_up, hg_norm_g, w_out):
    bf = lambda a: a.astype(BF16)
    w_h, w_cq, w_ckv, w_kr = jnp.split(w_in, [4 * HG_W, 4 * HG_W + Q_LORA, 4 * HG_W + Q_LORA + KV_LORA], axis=-1)
    w_q_rope = w_q_up[..., QK_NOPE:]
    return {
        "norm_g": norm_g.reshape(-1, 1, D_MODEL),
        "wg": bf(w_ffn_gate), "wu": bf(w_ffn_up), "wd": bf(w_ffn_down),
        "w_h": bf(w_h), "w_cq": bf(w_cq), "w_ckv": bf(w_ckv),
        "w_kr": bf(jnp.concatenate([w_kr, _rot_cols(w_kr)], axis=-1)),
        "q_norm_g": q_norm_g.reshape(1, Q_LORA), "kv_norm_g": kv_norm_g.reshape(1, KV_LORA),
        "w_qn": bf(w_q_up[..., :QK_NOPE].reshape(Q_LORA, MLA_HEADS * QK_NOPE)),
        "w_qr": bf(w_q_rope.reshape(Q_LORA, MLA_HEADS * QK_ROPE)),
        "w_qrr": bf(_rot_cols(w_q_rope).reshape(Q_LORA, MLA_HEADS * QK_ROPE)),
        "w_uk": bf(jnp.transpose(w_kv_up[..., :QK_NOPE], (1, 2, 0))),
        "w_uv": bf(jnp.transpose(w_kv_up[..., QK_NOPE:], (1, 0, 2))),
        "hg_norm_g": hg_norm_g.reshape(1, HG_DV),
        "w_out1": bf(w_out[:HG_W]), "w_out2": bf(w_out[HG_W:]),
    }


def _rope_tables(pos):
    half = QK_ROPE // 2
    inv = ROPE_THETA ** (-jnp.arange(half, dtype=F32) / half)
    ang = pos.astype(F32)[:, None] * inv[None, :]
    cos = jnp.tile(jnp.cos(ang), (1, 2))
    sin = jnp.tile(jnp.sin(ang), (1, 2))
    return (jnp.concatenate([cos, sin], axis=-1), jnp.tile(cos, (1, MLA_HEADS)), jnp.tile(sin, (1, MLA_HEADS)))


def kernel(x_prompt, x_sample, cache_kv_latent, cache_k_rope, state_hgrn, page_table, hgrn_lb_logits,
           norm_g, w_ffn_gate, w_ffn_up, w_ffn_down, w_in, q_norm_g, w_q_up, kv_norm_g, w_kv_up,
           hg_norm_g, w_out):
    bp, tp, _ = x_prompt.shape
    bs, ts, _ = x_sample.shape
    depth = norm_g.shape[0]
    assert depth == 1
    past_len = page_table.shape[1] * cache_kv_latent.shape[2]
    w = _prep_weights(norm_g[0], w_ffn_gate[0], w_ffn_up[0], w_ffn_down[0], w_in[0], q_norm_g[0],
                      w_q_up[0], kv_norm_g[0], w_kv_up[0], hg_norm_g[0], w_out[0])
    ng = w["norm_g"]
    lb_logits = hgrn_lb_logits.astype(F32)

    def pre(x, tables, with_h):
        h1, u = _ffn_block(x, ng[0], ng[1], w["wg"], w["wu"], w["wd"], 0, g_next=ng[2])
        return (h1, u) + tuple(_in_proj(u, *tables, w, with_h))

    def post(h1, o_h, o_m):
        return _ffn_block(h1, ng[4], ng[5], w["wg"], w["wu"], w["wd"], 1,
                          mix=(o_h, o_m, w["w_out1"], w["w_out2"], ng[3]))

    h1, u_p, ckv, kr, ckv_b, kr_b, qlat, qrope = pre(
        x_prompt.reshape(bp * tp, D_MODEL), _rope_tables(jnp.arange(tp, dtype=jnp.int32)), False)
    o_h, st_p = _hgrn_proj(u_p, w["w_h"], lb_logits, w["hg_norm_g"], bp, tp, chunk=HGRN_CHUNK)
    o_m = _attn_prompt(qlat, qrope, ckv_b, kr_b, w["w_uv"], bp, tp)
    y_p = post(h1, o_h, o_m).reshape(bp, tp, D_MODEL)
    ckv_p = ckv.reshape(1, bp, tp, KV_LORA)
    kr_p = kr.reshape(1, bp, tp, QK_ROPE)

    pos_s = past_len + jnp.arange(ts, dtype=jnp.int32)
    tabs = tuple(jnp.tile(a, (bs, 1)) for a in _rope_tables(pos_s))
    h1s, _, ckvs, krs, ckvs_b, krs_b, qlats, qropes, zhs = pre(x_sample.reshape(bs * ts, D_MODEL), tabs, True)
    t_pad = -(-ts // SUBLANES) * SUBLANES
    zhs_pad = jnp.pad(zhs.reshape(bs, ts, 4 * HG_W), ((0, 0), (0, t_pad - ts), (0, 0)))
    o_hs, st_s = _hgrn(zhs_pad, lb_logits, w["hg_norm_g"], state_hgrn[0], chunk=SUBLANES,
                       t_valid=ts if t_pad != ts else None)
    q_cols = lambda a, wd: a.reshape(bs, ts, MLA_HEADS, wd).transpose(0, 3, 2, 1).reshape(bs, wd, MLA_HEADS * ts)
    pad_new = lambda a, wd: jnp.pad(a.reshape(bs, ts, wd), ((0, 0), (0, NEW_KEY_ROWS - ts), (0, 0)))
    o_ms = _attn_sample(q_cols(qlats, KV_LORA), q_cols(qropes, QK_ROPE),
                        pad_new(ckvs_b, KV_LORA), pad_new(krs_b, QK_ROPE), w["w_uv"],
                        cache_kv_latent[0], jnp.swapaxes(cache_k_rope[0], 1, 2), page_table, ts)
    o_ms = o_ms.reshape(bs, MLA_HEADS, ts, V_DIM).transpose(0, 2, 1, 3).reshape(bs * ts, MLA_HEADS * V_DIM)
    y_s = post(h1s, o_hs[:, :ts].reshape(bs * ts, HG_W), o_ms).reshape(bs, ts, D_MODEL)

    return (y_p, y_s, ckv_p, kr_p, st_p[None], ckvs.reshape(1, bs, ts, KV_LORA),
            krs.reshape(1, bs, ts, QK_ROPE), st_s[None])
```

```python
import functools
import math

import jax
import jax.numpy as jnp
from jax import lax
from jax.experimental import pallas as pl
from jax.experimental.pallas import tpu as pltpu

D_MODEL = 1024
D_FF = 2816
HG_HEADS = 4
HG_DK = 128
HG_DV = 128
MLA_HEADS = 4
QK_NOPE = 128
QK_ROPE = 64
V_DIM = 128
Q_LORA = 768
KV_LORA = 256
ROPE_THETA = 10000.0
EPS = 1e-6
HG_W = HG_HEADS * HG_DK

VMEM_LIMIT_BYTES = 56 * 1024 * 1024
SUBLANES = 8
LANES = 128
IN_PROJ_ROWS = 512
HGRN_CHUNK = 64
HGRN_TILE_ROWS = 512
ATTN_Q_ROWS = 256
NEG_BIG = -1e30
LOG2E = math.log2(math.e)

F32 = jnp.float32
BF16 = jnp.bfloat16


def _dot(a, b):
    return jnp.dot(a, b, preferred_element_type=F32)


def _dot_nt(a, b):
    return lax.dot_general(a, b, (((1,), (1,)), ((), ())), preferred_element_type=F32)


def _dot_tn(a, b):
    return lax.dot_general(a, b, (((0,), (0,)), ((), ())), preferred_element_type=F32)


def _rms(x, g):
    return x * lax.rsqrt(jnp.mean(x * x, axis=-1, keepdims=True) + EPS) * g


def _sigmoid(x):
    return 1.0 / (1.0 + jnp.exp(-x))


def _silu(x):
    return x * _sigmoid(x)


def _params(*sem):
    return pltpu.CompilerParams(dimension_semantics=sem, vmem_limit_bytes=VMEM_LIMIT_BYTES)


FFN_CHUNK = 256
FFN_SUB_ROWS = 512


def _ffn_kernel(*refs, mix_in):
    if mix_in:
        h_ref, oh_ref, om_ref, w1_ref, w2_ref, gmix_ref = refs[:6]
        refs = refs[6:]
    else:
        h_ref = refs[0]
        refs = refs[1:]
    gpre_ref, gpost_ref, wg_ref, wu_ref, wd_ref, o_ref = refs
    tm = h_ref.shape[0]
    sub = min(tm, FFN_SUB_ROWS)
    tiles = [slice(r * sub, (r + 1) * sub) for r in range(tm // sub)]
    if mix_in:
        mixed = [_dot(oh_ref[rows, :], w1_ref[...]) + _dot(om_ref[rows, :], w2_ref[...]) for rows in tiles]
        xs = [h_ref[rows, :] + _rms(m, gmix_ref[...]) for rows, m in zip(tiles, mixed)]
    else:
        xs = [h_ref[rows, :] for rows in tiles]
    xns = [_rms(x, gpre_ref[...]).astype(BF16) for x in xs]
    for rows, x, xn in zip(tiles, xs, xns):
        acc = None
        for f in range(D_FF // FFN_CHUNK):
            cols = slice(f * FFN_CHUNK, (f + 1) * FFN_CHUNK)
            a = _dot(xn, wg_ref[:, cols])
            b = _dot(xn, wu_ref[:, cols])
            part = _dot((_silu(a) * b).astype(BF16), wd_ref[cols, :])
            acc = part if acc is None else acc + part
        o_ref[rows, :] = x + 0.5 * _rms(acc, gpost_ref[...])


def _ffn_block(x, g_pre, g_post, wg, wu, wd, which, mix=None):
    m = x.shape[0]
    tm = min(m, 2 * FFN_SUB_ROWS)

    def const(a):
        if a.ndim == 2:
            return pl.BlockSpec(a.shape, lambda i: (0, 0), pipeline_mode=pl.Buffered(1))
        return pl.BlockSpec((None,) + a.shape[1:], lambda i: (which, 0, 0), pipeline_mode=pl.Buffered(1))

    row = lambda a: pl.BlockSpec((tm, a.shape[1]), lambda i: (i, 0))
    args, in_specs = [x], [row(x)]
    if mix is not None:
        o_h, o_m, w1, w2, g_mix = mix
        args += [o_h, o_m, w1, w2, g_mix]
        in_specs += [row(o_h), row(o_m), const(w1), const(w2), const(g_mix)]
    args += [g_pre, g_post, wg, wu, wd]
    in_specs += [const(g_pre), const(g_post), const(wg), const(wu), const(wd)]
    return pl.pallas_call(
        functools.partial(_ffn_kernel, mix_in=mix is not None),
        grid=(m // tm,),
        in_specs=in_specs,
        out_specs=row(x),
        out_shape=jax.ShapeDtypeStruct((m, D_MODEL), F32),
        compiler_params=_params("parallel"),
        name="ffn_mix_block" if mix is not None else "ffn_block",
    )(*args)


def _inproj_kernel(x_ref, g_ref, csk_ref, cosq_ref, sinq_ref, wcq_ref, wckv_ref, wkr_ref,
                   qg_ref, kvg_ref, wqn_ref, wqr_ref, wqrr_ref, wuk_ref, *rest, with_h):
    if with_h:
        wh_ref, ckv_ref, kr_ref, ckvb_ref, krb_ref, qlat_ref, qrope_ref, zh_ref = rest
        half = wh_ref.shape[1] // 2
    else:
        ckv_ref, kr_ref, ckvb_ref, krb_ref, qlat_ref, qrope_ref, uout_ref = rest
    u = _rms(x_ref[...], g_ref[...]).astype(BF16)
    if not with_h:
        uout_ref[...] = u
    cq = _dot(u, wcq_ref[...])
    ckv = _dot(u, wckv_ref[...])
    kz = _dot(u, wkr_ref[...])
    if with_h:
        zh_ref[:, :half] = _dot(u, wh_ref[:, :half])
    cqn = _rms(cq, qg_ref[...]).astype(BF16)
    qn = _dot(cqn, wqn_ref[...])
    qr_a = _dot(cqn, wqr_ref[...])
    qr_b = _dot(cqn, wqrr_ref[...])
    if with_h:
        zh_ref[:, half:] = _dot(u, wh_ref[:, half:])
    qn = qn.astype(BF16)
    for h in range(MLA_HEADS):
        qlat_ref[:, h * KV_LORA:(h + 1) * KV_LORA] = _dot(
            qn[:, h * QK_NOPE:(h + 1) * QK_NOPE], wuk_ref[h]).astype(BF16)
    ckv = _rms(ckv, kvg_ref[...])
    ckv_ref[...] = ckv
    ckvb_ref[...] = ckv.astype(BF16)
    csk = csk_ref[...]
    kr = kz[:, :QK_ROPE] * csk[:, :QK_ROPE] + kz[:, QK_ROPE:] * csk[:, QK_ROPE:]
    kr_ref[...] = kr
    krb_ref[...] = kr.astype(BF16)
    qrope_ref[...] = (qr_a * cosq_ref[...] + qr_b * sinq_ref[...]).astype(BF16)


def _in_proj(x, g, csk, cosq, sinq, w, with_h):
    m = x.shape[0]
    tm = min(m, IN_PROJ_ROWS)
    nrep = csk.shape[0] // tm
    full = lambda a: pl.BlockSpec(a.shape, lambda i: (0,) * a.ndim, pipeline_mode=pl.Buffered(1))
    row = lambda width: pl.BlockSpec((tm, width), lambda i: (i, 0))
    tab = lambda width: pl.BlockSpec((tm, width), lambda i: (i % nrep, 0))
    rope_w = MLA_HEADS * QK_ROPE
    names = ["w_cq", "w_ckv", "w_kr", "q_norm_g", "kv_norm_g", "w_qn", "w_qr", "w_qrr", "w_uk"]
    names += ["w_h"] if with_h else []
    outs = [(KV_LORA, F32), (QK_ROPE, F32), (KV_LORA, BF16), (QK_ROPE, BF16),
            (MLA_HEADS * KV_LORA, BF16), (rope_w, BF16)]
    outs += [(4 * HG_W, F32)] if with_h else [(D_MODEL, BF16)]
    return pl.pallas_call(
        functools.partial(_inproj_kernel, with_h=with_h),
        grid=(m // tm,),
        in_specs=[row(D_MODEL), full(g), tab(2 * QK_ROPE), tab(rope_w), tab(rope_w)]
        + [full(w[n]) for n in names],
        out_specs=[row(width) for width, _ in outs],
        out_shape=[jax.ShapeDtypeStruct((m, width), dt) for width, dt in outs],
        compiler_params=_params("parallel"),
        name="in_proj",
    )(x, g, csk, cosq, sinq, *[w[n] for n in names])


def _split3(g):
    g1 = g.astype(BF16).astype(F32)
    r1 = g - g1
    g2 = r1.astype(BF16).astype(F32)
    g3 = (r1 - g2).astype(BF16).astype(F32)
    return g1, g2, g3


HG_SUB = 16
HGRN_ROWS_PER_STEP = 32


def _hgrn_chunk_fn(load, store, st_ref, c_ref, v_ref, lbl_ref, ng_ref, units, chunk, valid_rows):
    n_tiles = chunk // SUBLANES

    lbl = lbl_ref[...]
    e = jnp.exp(lbl - jnp.max(lbl, axis=0, keepdims=True))
    lb_all = e[0:1, :] / jnp.sum(e, axis=0, keepdims=True)

    r_i = lax.broadcasted_iota(jnp.int32, (chunk, chunk), 0)
    c_i = lax.broadcasted_iota(jnp.int32, (chunk, chunk), 1)
    tril = (c_i <= r_i).astype(F32)
    row8 = lax.broadcasted_iota(jnp.int32, (SUBLANES, HG_DK), 0)
    ng = ng_ref[...]
    ones = jnp.ones((HG_DK, HG_DV), BF16)
    sub = min(chunk, HG_SUB)
    n_sub = chunk // sub
    pairs = [(s, i) for s in range(chunk) for i in range(s // SUBLANES, (s // sub + 1) * sub // SUBLANES)]
    if n_sub > 1:
        assert sub & (sub - 1) == 0 and chunk & (chunk - 1) == 0
        rows_sub = lax.broadcasted_iota(jnp.int32, (chunk, (n_sub - 1) * chunk), 0) >> (sub.bit_length() - 1)
        cols_blk = lax.broadcasted_iota(jnp.int32, (chunk, (n_sub - 1) * chunk), 1) >> (chunk.bit_length() - 1)
        sub_mask = cols_blk == rows_sub - 1

    def gates(u, start):
        h = units[u][1]
        lb = lb_all[:, h * HG_DK:(h + 1) * HG_DK]
        q = _silu(load(0, u, start))
        f = lb + (1.0 - lb) * _sigmoid(load(1, u, start))
        g = jnp.log(f)
        k = 1.0 - f
        v = load(2, u, start)
        valid = valid_rows(start)
        if valid is not None:
            g = jnp.where(valid, g, 0.0)
            k = jnp.where(valid, k, 0.0)
            v = jnp.where(valid, v, 0.0)
        g1, g2, g3 = _split3(g)
        b = _dot(tril, g1) + _dot(tril, g2) + _dot(tril, g3)
        return q, k, v, b * LOG2E

    def products(u, q, k, v, b2):
        c_ref[u] = b2 - jnp.log(k) * LOG2E
        v_ref[u] = v
        q_t = [q[i * SUBLANES:(i + 1) * SUBLANES, :] for i in range(n_tiles)]
        b_t = [b2[i * SUBLANES:(i + 1) * SUBLANES, :] for i in range(n_tiles)]
        p_tiles = []
        for s, i in pairs:
            d = b_t[i] - c_ref[u, s:s + 1, :]
            if i == s // SUBLANES and s % SUBLANES:
                d = jnp.where(row8 >= s % SUBLANES, d, NEG_BIG)
            p_tiles.append(q_t[i] * jnp.exp2(d))
        a_rep = _dot(jnp.concatenate(p_tiles, axis=0).astype(BF16), ones)

        a_sub = None
        if n_sub > 1:
            zeros_sub = jnp.zeros((sub, HG_DK), F32)
            q_parts, k_parts = [zeros_sub], []
            for j in range(1, n_sub):
                lo = j * sub
                r_j = b2[lo - 1:lo, :]
                q_parts.append(q[lo:lo + sub, :] * jnp.exp2(b2[lo:lo + sub, :] - r_j))
                k_parts += [k[:lo, :] * jnp.exp2(r_j - b2[:lo, :])] + [zeros_sub] * (n_sub - j)
            a_sub = _dot_nt(jnp.concatenate(q_parts, axis=0).astype(BF16),
                            jnp.concatenate(k_parts, axis=0).astype(BF16))

        st = st_ref[u]
        o_state = _dot_nt((q * jnp.exp2(b2)).astype(BF16), st.astype(BF16))
        bl = b2[chunk - 1:chunk, :]
        v_b = v.astype(BF16)
        st_ref[u] = st * jnp.exp2(bl) + _dot_tn(v_b, (k * jnp.exp2(bl - b2)).astype(BF16))
        return a_rep, a_sub, o_state, v_b

    def outputs(u, start, a_rep, a_sub, o_state, v_b):
        o_t = [o_state[i * SUBLANES:(i + 1) * SUBLANES, :] for i in range(n_tiles)]
        for j, (s, i) in enumerate(pairs):
            o_t[i] = o_t[i] + a_rep[j * SUBLANES:(j + 1) * SUBLANES, :] * v_ref[u, s:s + 1, :]
        o = jnp.concatenate(o_t, axis=0) if n_tiles > 1 else o_t[0]
        if a_sub is not None:
            o = o + _dot(jnp.where(sub_mask, a_sub, 0.0).astype(BF16),
                         jnp.concatenate([v_b] * (n_sub - 1), axis=0))
        store(u, start, _rms(o, ng) * _silu(load(3, u, start)))

    def chunk_fn(start, after_gates=None):
        gated = [gates(u, start) for u in range(len(units))]
        if after_gates is not None:
            after_gates()
        prods = [products(u, *gated[u]) for u in range(len(units))]
        for u in range(len(units)):
            outputs(u, start, *prods[u])

    return chunk_fn


def _hgrn_kernel(*refs, chunk, t_valid, has_s0):
    if has_s0:
        hq_ref, hf_ref, hi_ref, hg_ref, lbl_ref, ng_ref, s0_ref = refs[:7]
        refs = refs[7:]
    else:
        hq_ref, hf_ref, hi_ref, hg_ref, lbl_ref, ng_ref = refs[:6]
        s0_ref = None
        refs = refs[6:]
    o_ref, sfin_ref, st_ref, c_ref, v_ref = refs
    ti = pl.program_id(1)
    n_batch, t_tile = hq_ref.shape[:2]
    units = [divmod(u, HG_HEADS) for u in range(n_batch * HG_HEADS)]
    n_chunks = t_tile // chunk
    fields = (hq_ref, hf_ref, hi_ref, hg_ref)

    @pl.when(ti == 0)
    def _():
        for u, (e, h) in enumerate(units):
            st_ref[u] = s0_ref[e, h].T if has_s0 else jnp.zeros((HG_DV, HG_DK), F32)

    def where(u, start):
        e, h = units[u]
        return e, pl.ds(start, chunk), slice(h * HG_DK, (h + 1) * HG_DK)

    def load(field, u, start):
        return fields[field][where(u, start)]

    def store(u, start, value):
        o_ref[where(u, start)] = value.astype(o_ref.dtype)

    def valid_rows(start):
        if t_valid is None:
            return None
        pos = lax.broadcasted_iota(jnp.int32, (chunk, HG_DK), 0) + (ti * t_tile + start)
        return pos < t_valid

    chunk_fn = _hgrn_chunk_fn(load, store, st_ref, c_ref, v_ref, lbl_ref, ng_ref, units, chunk, valid_rows)

    def chunk_body(c, carry):
        chunk_fn(pl.multiple_of(c * chunk, chunk))
        return carry

    lax.fori_loop(0, n_chunks, chunk_body, 0, unroll=4 if n_chunks % 4 == 0 else 1)

    @pl.when(ti == pl.num_programs(1) - 1)
    def _():
        for u, (e, h) in enumerate(units):
            sfin_ref[e, h] = st_ref[u].T


def _hgrn(zh, lb_logits, ng, s0, chunk, t_valid):
    bsz, t, _ = zh.shape
    has_s0 = s0 is not None
    t_tile = min(t, HGRN_TILE_ROWS)
    assert t % t_tile == 0 and t_tile % chunk == 0
    n_batch = math.gcd(bsz, max(1, HGRN_ROWS_PER_STEP // t_tile))
    n_units = n_batch * HG_HEADS
    col = lambda j: pl.BlockSpec((n_batch, t_tile, HG_W), lambda b, i: (b, i, j))
    in_specs = [col(0), col(1), col(2), col(3),
                pl.BlockSpec(lb_logits.shape, lambda b, i: (0, 0)),
                pl.BlockSpec((1, HG_DV), lambda b, i: (0, 0))]
    args = [zh, zh, zh, zh, lb_logits, ng]
    st_spec = pl.BlockSpec((n_batch, HG_HEADS, HG_DK, HG_DV), lambda b, i: (b, 0, 0, 0))
    if has_s0:
        in_specs.append(st_spec)
        args.append(s0)
    return pl.pallas_call(
        functools.partial(_hgrn_kernel, chunk=chunk, t_valid=t_valid, has_s0=has_s0),
        grid=(bsz // n_batch, t // t_tile),
        in_specs=in_specs,
        out_specs=[pl.BlockSpec((n_batch, t_tile, HG_W), lambda b, i: (b, i, 0)), st_spec],
        out_shape=[jax.ShapeDtypeStruct((bsz, t, HG_W), BF16),
                   jax.ShapeDtypeStruct((bsz, HG_HEADS, HG_DK, HG_DV), F32)],
        scratch_shapes=[pltpu.VMEM((n_units, HG_DV, HG_DK), F32),
                        pltpu.VMEM((n_units, chunk, HG_DK), F32),
                        pltpu.VMEM((n_units, chunk, HG_DV), F32)],
        compiler_params=_params("parallel", "arbitrary"),
        name="hgrn",
    )(*args)


HGRN_PIECES = 8


def _hgrn_proj_kernel(ua_ref, ub_ref, u0_ref, wh_ref, lbl_ref, ng_ref, o_ref, sfin_ref,
                      zh_ref, st_ref, c_ref, v_ref, *, chunk, tiles_per_seq):
    g = pl.program_id(0)
    t_tile = ua_ref.shape[0]
    n_chunks = t_tile // chunk
    assert n_chunks == HGRN_PIECES and tiles_per_seq % 2 == 0
    piece = wh_ref.shape[1] // HGRN_PIECES
    units = [(0, h) for h in range(HG_HEADS)]

    def project(u_ref, slot, k):
        cols = slice(k * piece, (k + 1) * piece)
        zh_ref[slot, :, cols] = _dot(u_ref[...], wh_ref[:, cols])

    @pl.when(g == 0)
    def _():
        st_ref[...] = jnp.zeros_like(st_ref)
        for k in range(HGRN_PIECES):
            project(u0_ref, 0, k)

    def run_tile(slot, u_next_ref, row0, new_seq):
        if new_seq is not None:
            for u in range(len(units)):
                st_ref[u] = jnp.where(new_seq, 0.0, st_ref[u])

        def load(field, u, start):
            col = (field * HG_HEADS + units[u][1]) * HG_DK
            return zh_ref[slot, start:start + chunk, col:col + HG_DK]

        def store(u, start, value):
            h = units[u][1]
            o_ref[row0 + start:row0 + start + chunk, h * HG_DV:(h + 1) * HG_DV] = value.astype(o_ref.dtype)

        chunk_fn = _hgrn_chunk_fn(load, store, st_ref, c_ref, v_ref, lbl_ref, ng_ref, units, chunk,
                                  lambda start: None)
        for c in range(n_chunks):
            chunk_fn(c * chunk, functools.partial(project, u_next_ref, 1 - slot, c))

    tile_a = 2 * g
    run_tile(0, ua_ref, 0, (tile_a % tiles_per_seq) == 0)
    run_tile(1, ub_ref, t_tile, None)

    @pl.when((tile_a + 1) % tiles_per_seq == tiles_per_seq - 1)
    def _():
        for u, (_, h) in enumerate(units):
            sfin_ref[h] = st_ref[u].T


def _hgrn_proj(u, w_h, lb_logits, ng, bsz, t, chunk):
    t_tile = chunk * HGRN_PIECES
    assert t % t_tile == 0
    tiles_per_seq = t // t_tile
    n_tiles = bsz * tiles_per_seq
    assert n_tiles % 2 == 0
    const = lambda a: pl.BlockSpec(a.shape, lambda i: (0, 0), pipeline_mode=pl.Buffered(1))
    tile = lambda f: pl.BlockSpec((t_tile, D_MODEL), lambda i: (jnp.minimum(f(i), n_tiles - 1), 0))
    return pl.pallas_call(
        functools.partial(_hgrn_proj_kernel, chunk=chunk, tiles_per_seq=tiles_per_seq),
        grid=(n_tiles // 2,),
        in_specs=[tile(lambda i: 2 * i + 1), tile(lambda i: 2 * i + 2), tile(lambda i: 0),
                  const(w_h), const(lb_logits), const(ng)],
        out_specs=[pl.BlockSpec((2 * t_tile, HG_W), lambda i: (i, 0)),
                   pl.BlockSpec((None, HG_HEADS, HG_DK, HG_DV), lambda i: ((2 * i + 1) // tiles_per_seq, 0, 0, 0))],
        out_shape=[jax.ShapeDtypeStruct((bsz * t, HG_W), BF16),
                   jax.ShapeDtypeStruct((bsz, HG_HEADS, HG_DK, HG_DV), F32)],
        scratch_shapes=[pltpu.VMEM((2, t_tile, 4 * HG_W), F32),
                        pltpu.VMEM((HG_HEADS, HG_DV, HG_DK), F32),
                        pltpu.VMEM((HG_HEADS, chunk, HG_DK), F32),
                        pltpu.VMEM((HG_HEADS, chunk, HG_DV), F32)],
        compiler_params=_params("arbitrary"),
        name="hgrn_proj",
    )(u, u, u, w_h, lb_logits, ng)


ATTN_SCALE = 1.0 / math.sqrt(QK_NOPE + QK_ROPE)
ATTN_EXP2_SCALE = ATTN_SCALE * LOG2E


def _attn_prompt_kernel(ql_ref, qr_ref, kc_ref, kr_ref, wuv_ref, o_ref,
                        m_ref, l_ref, a_ref, acc_ref, s_ref, p_ref, *, tq):
    qi = pl.program_id(1)
    ql = jnp.concatenate([ql_ref[:, h * KV_LORA:(h + 1) * KV_LORA] for h in range(MLA_HEADS)], axis=0)
    qr = jnp.concatenate([qr_ref[:, h * QK_ROPE:(h + 1) * QK_ROPE] for h in range(MLA_HEADS)], axis=0)
    m_ref[...] = jnp.full_like(m_ref, NEG_BIG)
    l_ref[...] = jnp.zeros_like(l_ref)
    a_ref[...] = jnp.zeros_like(a_ref)
    acc_ref[...] = jnp.zeros_like(acc_ref)
    p_ref[1] = jnp.zeros(p_ref.shape[1:], BF16)

    def keys(kb):
        return pl.ds(pl.multiple_of(kb * tq, tq), tq)

    def scores(kb):
        return (_dot_nt(ql, kc_ref[keys(kb), :]) + _dot_nt(qr, kr_ref[keys(kb), :])) * ATTN_EXP2_SCALE

    def add_values(kb, slot):
        acc_ref[...] = (jnp.tile(a_ref[...], (1, KV_LORA // LANES)) * acc_ref[...]
                        + _dot(p_ref[slot], kc_ref[keys(kb), :]))

    def softmax(slot, masked):
        s = s_ref[slot]
        if masked:
            rows = lax.broadcasted_iota(jnp.int32, s.shape, 0) & (tq - 1)
            cols = lax.broadcasted_iota(jnp.int32, s.shape, 1)
            s = jnp.where(cols <= rows, s, NEG_BIG)
        m_prev = m_ref[...]
        m_new = jnp.maximum(m_prev, jnp.max(s, axis=-1, keepdims=True))
        alpha = jnp.exp2(m_prev - m_new)
        p = jnp.exp2(s - jnp.tile(m_new, (1, tq // LANES)))
        l_ref[...] = alpha * l_ref[...] + jnp.sum(p, axis=-1, keepdims=True)
        m_ref[...] = m_new
        a_ref[...] = alpha
        p_ref[slot] = p.astype(BF16)

    s_ref[0] = scores(0)

    def iteration(i, cur):
        s_ref[1 - cur] = scores(i + 1)
        add_values(jnp.maximum(i - 1, 0), 1 - cur)
        softmax(cur, False)

    def last(cur):
        add_values(jnp.maximum(qi - 1, 0), 1 - cur)
        softmax(cur, True)
        add_values(qi, cur)

    def pair(j, carry):
        iteration(2 * j, 0)
        iteration(2 * j + 1, 1)
        return carry

    lax.fori_loop(0, qi >> 1, pair, 0)

    @pl.when((qi & 1) == 1)
    def _():
        iteration(qi - 1, 0)
        last(1)

    @pl.when((qi & 1) == 0)
    def _():
        last(0)

    o_lat =(acc_ref[...] / jnp.tile(l_ref[...], (1, KV_LORA // LANES))).astype(BF16)
    for h in range(MLA_HEADS):
        o_ref[:, h * V_DIM:(h + 1) * V_DIM] = _dot(o_lat[h * tq:(h + 1) * tq, :], wuv_ref[h]).astype(o_ref.dtype)


def _attn_prompt(qlat, qrope, ckv_b, kr_b, w_uv, bsz, t):
    tq = ATTN_Q_ROWS
    assert tq & (tq - 1) == 0 and t % tq == 0
    nq = t // tq
    rows = MLA_HEADS * tq
    return pl.pallas_call(
        functools.partial(_attn_prompt_kernel, tq=tq),
        grid=(bsz, nq),
        in_specs=[
            pl.BlockSpec((tq, MLA_HEADS * KV_LORA), lambda b, i: (b * nq + i, 0)),
            pl.BlockSpec((tq, MLA_HEADS * QK_ROPE), lambda b, i: (b * nq + i, 0)),
            pl.BlockSpec((t, KV_LORA), lambda b, i: (b, 0)),
            pl.BlockSpec((t, QK_ROPE), lambda b, i: (b, 0)),
            pl.BlockSpec(w_uv.shape, lambda b, i: (0, 0, 0)),
        ],
        out_specs=pl.BlockSpec((tq, MLA_HEADS * V_DIM), lambda b, i: (b * nq + i, 0)),
        out_shape=jax.ShapeDtypeStruct((bsz * t, MLA_HEADS * V_DIM), BF16),
        scratch_shapes=[pltpu.VMEM((rows, LANES), F32), pltpu.VMEM((rows, LANES), F32),
                        pltpu.VMEM((rows, LANES), F32), pltpu.VMEM((rows, KV_LORA), F32),
                        pltpu.VMEM((2, rows, tq), F32), pltpu.VMEM((2, rows, tq), BF16)],
        compiler_params=_params("parallel", "parallel"),
        name="attn_prompt",
    )(qlat, qrope, ckv_b, kr_b, w_uv)


PAGES_PER_STEP = 16
PAGE_RING = 4
NEW_KEY_ROWS = 128


def _attn_sample_kernel(pt_ref, ql_ref, qr_ref, nc_ref, nr_ref, wuv_ref, cc_hbm, cr_hbm, o_ref,
                        cbuf, rbuf, sem, wq_ref, wr_ref, m_ref, l_ref, acc_ref, kv_ref, kq_ref, kr_ref,
                        *, t_new, n_q, n_steps):
    b = pl.program_id(0)
    nb = pl.num_programs(0)
    groups = LANES // n_q
    page = cbuf.shape[1] // PAGES_PER_STEP
    ppg = PAGES_PER_STEP // groups
    lat_rep = KV_LORA // LANES

    def page_copies(bb, step, slot):
        copies = []
        for i in range(PAGES_PER_STEP):
            pid = pt_ref[bb, step * PAGES_PER_STEP + i]
            copies.append(pltpu.make_async_copy(cc_hbm.at[pid], cbuf.at[slot, pl.ds(i * page, page), :],
                                                sem.at[slot]))
            copies.append(pltpu.make_async_copy(cr_hbm.at[pid], rbuf.at[slot, i], sem.at[slot]))
        return copies

    @pl.when(b == 0)
    def _():
        for step in range(PAGE_RING):
            for c in page_copies(0, step, step):
                c.start()

    q_shift = n_q.bit_length() - 1
    spread = (lax.broadcasted_iota(jnp.int32, (n_q, LANES), 1) & (n_q - 1)) == lax.broadcasted_iota(
        jnp.int32, (n_q, LANES), 0)
    spread = jnp.where(spread, 1.0, 0.0).astype(BF16)
    for src, dst in ((ql_ref, wq_ref), (qr_ref, wr_ref)):
        feat = src.shape[0]
        tiled = _dot(src[...], spread)
        lane_g = lax.broadcasted_iota(jnp.int32, tiled.shape, 1) >> q_shift
        for g in range(groups):
            dst[g * feat:(g + 1) * feat, :] = jnp.where(lane_g == g, tiled, 0.0).astype(BF16)

    m_ref[...] = jnp.full_like(m_ref, NEG_BIG)
    l_ref[...] = jnp.zeros_like(l_ref)
    acc_ref[...] = jnp.zeros_like(acc_ref)

    def lanes_to_rows(row):
        return jnp.broadcast_to(row, (LANES, LANES)).T

    def update(s, place, values):
        m_prev = m_ref[...]
        m_new = jnp.maximum(m_prev, jnp.max(s, axis=0, keepdims=True))
        alpha = jnp.exp2(m_prev - m_new)
        p = jnp.exp2(s - m_new)
        l_ref[...] = alpha * l_ref[...] + jnp.sum(p, axis=0, keepdims=True)
        m_ref[...] = m_new
        pv = _dot(place(p.T), values)
        acc_ref[...] = jnp.tile(lanes_to_rows(alpha), (1, lat_rep)) * acc_ref[...] + pv

    def block_diag(p_t):
        row_g = lax.broadcasted_iota(jnp.int32, p_t.shape, 0) >> q_shift
        return jnp.concatenate([jnp.where(row_g == g, p_t, 0.0).astype(BF16) for g in range(groups)], axis=1)

    def stage_and_score(j):
        ring = j % PAGE_RING
        slot = j & 1
        for c in page_copies(b, j, ring):
            c.wait()
        for i in range(PAGES_PER_STEP):
            g, r = divmod(i, ppg)
            x = cbuf[ring, i * page:(i + 1) * page, :].astype(BF16)
            kv_ref[slot, i * page:(i + 1) * page, :] = x
            kq_ref[slot, r * page:(r + 1) * page, g * KV_LORA:(g + 1) * KV_LORA] = x
            kr_ref[slot, g * QK_ROPE:(g + 1) * QK_ROPE, r * page:(r + 1) * page] = rbuf[ring, i].astype(BF16)
        if j + PAGE_RING < n_steps:
            nxt = page_copies(b, j + PAGE_RING, ring)
        else:
            nxt = page_copies(jnp.minimum(b + 1, nb - 1), j + PAGE_RING - n_steps, ring)
        for c in nxt:
            c.start()
        return (_dot(kq_ref[slot], wq_ref[...]) + _dot_tn(kr_ref[slot], wr_ref[...])) * ATTN_EXP2_SCALE

    s_cur = stage_and_score(0)
    for j in range(n_steps):
        s_next = stage_and_score(j + 1) if j + 1 < n_steps else None
        update(s_cur, block_diag, kv_ref[j & 1])
        s_cur = s_next

    @pl.when(b == nb - 1)
    def _():
        for step in range(PAGE_RING):
            for c in page_copies(b, step, step):
                c.wait()

    nc = nc_ref[...]
    s2 = (_dot(nc, wq_ref[:KV_LORA, :]) + _dot(nr_ref[...], wr_ref[:QK_ROPE, :])) * ATTN_EXP2_SCALE
    t_k = lax.broadcasted_iota(jnp.int32, s2.shape, 0)
    lane = lax.broadcasted_iota(jnp.int32, s2.shape, 1)
    ok = (lane < n_q) & (t_k <= (lane & (t_new - 1)))
    update(jnp.where(ok, s2, NEG_BIG), lambda p_t: p_t.astype(BF16), nc)

    grp = lambda a, g: a[g * n_q:(g + 1) * n_q]
    total = lambda a: functools.reduce(lambda x, y: x + y, [grp(a, g) for g in range(groups)])
    m_t = lanes_to_rows(m_ref[...])
    l_t = lanes_to_rows(l_ref[...])
    m_q = functools.reduce(jnp.maximum, [grp(m_t, g) for g in range(groups)])
    w = jnp.exp2(m_t - jnp.tile(m_q, (groups, 1)))
    l_q = total(w * l_t)
    o_lat = total(jnp.tile(w, (1, lat_rep)) * acc_ref[...]) / jnp.tile(l_q, (1, lat_rep))
    o_lat = o_lat.astype(BF16)
    head = lax.broadcasted_iota(jnp.int32, (n_q, V_DIM), 0) >> (t_new.bit_length() - 1)
    out = jnp.zeros((n_q, V_DIM), F32)
    for h in range(MLA_HEADS):
        out = out + jnp.where(head == h, _dot(o_lat, wuv_ref[h]), 0.0)
    o_ref[...] = out.astype(o_ref.dtype)


def _attn_sample(ql_t, qr_t, new_c, new_r, w_uv, cache_c, cache_r_t, page_table, t_new):
    bsz, _, n_q = ql_t.shape
    n_pages = page_table.shape[1]
    groups = LANES // n_q
    assert LANES % n_q == 0 and PAGES_PER_STEP % groups == 0 and n_pages % PAGES_PER_STEP == 0
    assert t_new & (t_new - 1) == 0 and n_q & (n_q - 1) == 0 and t_new <= NEW_KEY_ROWS and n_q >= SUBLANES
    page = cache_c.shape[1]
    keys = PAGES_PER_STEP * page
    n_steps = n_pages // PAGES_PER_STEP
    assert n_steps % PAGE_RING == 0

    per_b = lambda a: pl.BlockSpec((None,) + a.shape[1:], lambda b, pt: (b, 0, 0))
    grid_spec = pltpu.PrefetchScalarGridSpec(
        num_scalar_prefetch=1,
        grid=(bsz,),
        in_specs=[per_b(ql_t), per_b(qr_t), per_b(new_c), per_b(new_r),
                  pl.BlockSpec(w_uv.shape, lambda b, pt: (0, 0, 0)),
                  pl.BlockSpec(memory_space=pl.ANY), pl.BlockSpec(memory_space=pl.ANY)],
        out_specs=pl.BlockSpec((None, n_q, V_DIM), lambda b, pt: (b, 0, 0)),
        scratch_shapes=[pltpu.VMEM((PAGE_RING, keys, KV_LORA), cache_c.dtype),
                        pltpu.VMEM((PAGE_RING, PAGES_PER_STEP, QK_ROPE, page), cache_r_t.dtype),
                        pltpu.SemaphoreType.DMA((PAGE_RING,)),
                        pltpu.VMEM((groups * KV_LORA, LANES), BF16),
                        pltpu.VMEM((groups * QK_ROPE, LANES), BF16),
                        pltpu.VMEM((1, LANES), F32), pltpu.VMEM((1, LANES), F32),
                        pltpu.VMEM((LANES, KV_LORA), F32),
                        pltpu.VMEM((2, keys, KV_LORA), BF16),
                        pltpu.VMEM((2, keys // groups, groups * KV_LORA), BF16),
                        pltpu.VMEM((2, groups * QK_ROPE, keys // groups), BF16)],
    )
    return pl.pallas_call(
        functools.partial(_attn_sample_kernel, t_new=t_new, n_q=n_q, n_steps=n_steps),
        grid_spec=grid_spec,
        out_shape=jax.ShapeDtypeStruct((bsz, n_q, V_DIM), BF16),
        compiler_params=_params("arbitrary"),
        name="attn_sample",
    )(page_table, ql_t, qr_t, new_c, new_r, w_uv, cache_c, cache_r_t)


def _rot_cols(w):
    half = w.shape[-1] // 2
    return jnp.concatenate([-w[..., half:], w[..., :half]], axis=-1)


def _prep_weights(norm_g, w_ffn_gate, w_ffn_up, w_ffn_down, w_in, q_norm_g, w_q_up, kv_norm_g,
                  w_kv_up, hg_norm_g, w_out):
    bf = lambda a: a.astype(BF16)
    w_h, w_cq, w_ckv, w_kr = jnp.split(w_in, [4 * HG_W, 4 * HG_W + Q_LORA, 4 * HG_W + Q_LORA + KV_LORA], axis=-1)
    w_q_rope = w_q_up[..., QK_NOPE:]
    return {
        "norm_g": norm_g.reshape(-1, 1, D_MODEL),
        "wg": bf(w_ffn_gate), "wu": bf(w_ffn_up), "wd": bf(w_ffn_down),
        "w_h": bf(w_h), "w_cq": bf(w_cq), "w_ckv": bf(w_ckv),
        "w_kr": bf(jnp.concatenate([w_kr, _rot_cols(w_kr)], axis=-1)),
        "q_norm_g": q_norm_g.reshape(1, Q_LORA), "kv_norm_g": kv_norm_g.reshape(1, KV_LORA),
        "w_qn": bf(w_q_up[..., :QK_NOPE].reshape(Q_LORA, MLA_HEADS * QK_NOPE)),
        "w_qr": bf(w_q_rope.reshape(Q_LORA, MLA_HEADS * QK_ROPE)),
        "w_qrr": bf(_rot_cols(w_q_rope).reshape(Q_LORA, MLA_HEADS * QK_ROPE)),
        "w_uk": bf(jnp.transpose(w_kv_up[..., :QK_NOPE], (1, 2, 0))),
        "w_uv": bf(jnp.transpose(w_kv_up[..., QK_NOPE:], (1, 0, 2))),
        "hg_norm_g": hg_norm_g.reshape(1, HG_DV),
        "w_out1": bf(w_out[:HG_W]), "w_out2": bf(w_out[HG_W:]),
    }


def _rope_tables(pos):
    half = QK_ROPE // 2
    inv = ROPE_THETA ** (-jnp.arange(half, dtype=F32) / half)
    ang = pos.astype(F32)[:, None] * inv[None, :]
    cos = jnp.tile(jnp.cos(ang), (1, 2))
    sin = jnp.tile(jnp.sin(ang), (1, 2))
    return (jnp.concatenate([cos, sin], axis=-1), jnp.tile(cos, (1, MLA_HEADS)), jnp.tile(sin, (1, MLA_HEADS)))


def kernel(x_prompt, x_sample, cache_kv_latent, cache_k_rope, state_hgrn, page_table, hgrn_lb_logits,
           norm_g, w_ffn_gate, w_ffn_up, w_ffn_down, w_in, q_norm_g, w_q_up, kv_norm_g, w_kv_up,
           hg_norm_g, w_out):
    bp, tp, _ = x_prompt.shape
    bs, ts, _ = x_sample.shape
    depth = norm_g.shape[0]
    assert depth == 1
    past_len = page_table.shape[1] * cache_kv_latent.shape[2]
    w = _prep_weights(norm_g[0], w_ffn_gate[0], w_ffn_up[0], w_ffn_down[0], w_in[0], q_norm_g[0],
                      w_q_up[0], kv_norm_g[0], w_kv_up[0], hg_norm_g[0], w_out[0])
    ng = w["norm_g"]
    lb_logits = hgrn_lb_logits.astype(F32)

    def pre(x, tables, with_h):
        h1 = _ffn_block(x, ng[0], ng[1], w["wg"], w["wu"], w["wd"], 0)
        return (h1,) + tuple(_in_proj(h1, ng[2], *tables, w, with_h))

    def post(h1, o_h, o_m):
        return _ffn_block(h1, ng[4], ng[5], w["wg"], w["wu"], w["wd"], 1,
                          mix=(o_h, o_m, w["w_out1"], w["w_out2"], ng[3]))

    h1, ckv, kr, ckv_b, kr_b, qlat, qrope, u_p = pre(
        x_prompt.reshape(bp * tp, D_MODEL), _rope_tables(jnp.arange(tp, dtype=jnp.int32)), False)
    o_h, st_p = _hgrn_proj(u_p, w["w_h"], lb_logits, w["hg_norm_g"], bp, tp, chunk=HGRN_CHUNK)
    o_m = _attn_prompt(qlat, qrope, ckv_b, kr_b, w["w_uv"], bp, tp)
    y_p = post(h1, o_h, o_m).reshape(bp, tp, D_MODEL)
    ckv_p = ckv.reshape(1, bp, tp, KV_LORA)
    kr_p = kr.reshape(1, bp, tp, QK_ROPE)

    pos_s = past_len + jnp.arange(ts, dtype=jnp.int32)
    tabs = tuple(jnp.tile(a, (bs, 1)) for a in _rope_tables(pos_s))
    h1s, ckvs, krs, ckvs_b, krs_b, qlats, qropes, zhs = pre(x_sample.reshape(bs * ts, D_MODEL), tabs, True)
    t_pad = -(-ts // SUBLANES) * SUBLANES
    zhs_pad = jnp.pad(zhs.reshape(bs, ts, 4 * HG_W), ((0, 0), (0, t_pad - ts), (0, 0)))
    o_hs, st_s = _hgrn(zhs_pad, lb_logits, w["hg_norm_g"], state_hgrn[0], chunk=SUBLANES,
                       t_valid=ts if t_pad != ts else None)
    q_cols = lambda a, wd: a.reshape(bs, ts, MLA_HEADS, wd).transpose(0, 3, 2, 1).reshape(bs, wd, MLA_HEADS * ts)
    pad_new = lambda a, wd: jnp.pad(a.reshape(bs, ts, wd), ((0, 0), (0, NEW_KEY_ROWS - ts), (0, 0)))
    o_ms = _attn_sample(q_cols(qlats, KV_LORA), q_cols(qropes, QK_ROPE),
                        pad_new(ckvs_b, KV_LORA), pad_new(krs_b, QK_ROPE), w["w_uv"],
                        cache_kv_latent[0], jnp.swapaxes(cache_k_rope[0], 1, 2), page_table, ts)
    o_ms = o_ms.reshape(bs, MLA_HEADS, ts, V_DIM).transpose(0, 2, 1, 3).reshape(bs * ts, MLA_HEADS * V_DIM)
    y_s = post(h1s, o_hs[:, :ts].reshape(bs * ts, HG_W), o_ms).reshape(bs, ts, D_MODEL)

    return (y_p, y_s, ckv_p, kr_p, st_p[None], ckvs.reshape(1, bs, ts, KV_LORA),
            krs.reshape(1, bs, ts, QK_ROPE), st_s[None])
```

```python
import functools
import math

import jax
import jax.numpy as jnp
from jax import lax
from jax.experimental import pallas as pl
from jax.experimental.pallas import tpu as pltpu

D_MODEL = 1024
D_FF = 2816
HG_HEADS = 4
HG_DK = 128
HG_DV = 128
MLA_HEADS = 4
QK_NOPE = 128
QK_ROPE = 64
V_DIM = 128
Q_LORA = 768
KV_LORA = 256
ROPE_THETA = 10000.0
EPS = 1e-6
HG_W = HG_HEADS * HG_DK

VMEM_LIMIT_BYTES = 56 * 1024 * 1024
SUBLANES = 8
LANES = 128
IN_PROJ_ROWS = 1024
HGRN_CHUNK = 64
HGRN_TILE_ROWS = 512
ATTN_Q_ROWS = 256
NEG_BIG = -1e30
LOG2E = math.log2(math.e)

F32 = jnp.float32
BF16 = jnp.bfloat16


def _dot(a, b):
    return jnp.dot(a, b, preferred_element_type=F32)


def _dot_nt(a, b):
    return lax.dot_general(a, b, (((1,), (1,)), ((), ())), preferred_element_type=F32)


def _dot_tn(a, b):
    return lax.dot_general(a, b, (((0,), (0,)), ((), ())), preferred_element_type=F32)


def _rms(x, g):
    return x * lax.rsqrt(jnp.mean(x * x, axis=-1, keepdims=True) + EPS) * g


def _sigmoid(x):
    return 1.0 / (1.0 + jnp.exp(-x))


def _silu(x):
    return x * _sigmoid(x)


def _params(*sem):
    return pltpu.CompilerParams(dimension_semantics=sem, vmem_limit_bytes=VMEM_LIMIT_BYTES)


FFN_CHUNK = 256
FFN_SUB_ROWS = 512


def _ffn_kernel(*refs, mix_in):
    if mix_in:
        h_ref, oh_ref, om_ref, w1_ref, w2_ref, gmix_ref = refs[:6]
        refs = refs[6:]
    else:
        h_ref = refs[0]
        refs = refs[1:]
    gpre_ref, gpost_ref, wg_ref, wu_ref, wd_ref, o_ref = refs
    tm = h_ref.shape[0]
    sub = min(tm, FFN_SUB_ROWS)
    tiles = [slice(r * sub, (r + 1) * sub) for r in range(tm // sub)]
    if mix_in:
        mixed = [_dot(oh_ref[rows, :], w1_ref[...]) + _dot(om_ref[rows, :], w2_ref[...]) for rows in tiles]
        xs = [h_ref[rows, :] + _rms(m, gmix_ref[...]) for rows, m in zip(tiles, mixed)]
    else:
        xs = [h_ref[rows, :] for rows in tiles]
    xns = [_rms(x, gpre_ref[...]).astype(BF16) for x in xs]
    for rows, x, xn in zip(tiles, xs, xns):
        acc = None
        for f in range(D_FF // FFN_CHUNK):
            cols = slice(f * FFN_CHUNK, (f + 1) * FFN_CHUNK)
            a = _dot(xn, wg_ref[:, cols])
            b = _dot(xn, wu_ref[:, cols])
            part = _dot((_silu(a) * b).astype(BF16), wd_ref[cols, :])
            acc = part if acc is None else acc + part
        o_ref[rows, :] = x + 0.5 * _rms(acc, gpost_ref[...])


def _ffn_block(x, g_pre, g_post, wg, wu, wd, which, mix=None):
    m = x.shape[0]
    tm = min(m, 2 * FFN_SUB_ROWS)

    def const(a):
        if a.ndim == 2:
            return pl.BlockSpec(a.shape, lambda i: (0, 0), pipeline_mode=pl.Buffered(1))
        return pl.BlockSpec((None,) + a.shape[1:], lambda i: (which, 0, 0), pipeline_mode=pl.Buffered(1))

    row = lambda a: pl.BlockSpec((tm, a.shape[1]), lambda i: (i, 0))
    args, in_specs = [x], [row(x)]
    if mix is not None:
        o_h, o_m, w1, w2, g_mix = mix
        args += [o_h, o_m, w1, w2, g_mix]
        in_specs += [row(o_h), row(o_m), const(w1), const(w2), const(g_mix)]
    args += [g_pre, g_post, wg, wu, wd]
    in_specs += [const(g_pre), const(g_post), const(wg), const(wu), const(wd)]
    return pl.pallas_call(
        functools.partial(_ffn_kernel, mix_in=mix is not None),
        grid=(m // tm,),
        in_specs=in_specs,
        out_specs=row(x),
        out_shape=jax.ShapeDtypeStruct((m, D_MODEL), F32),
        compiler_params=_params("parallel"),
        name="ffn_mix_block" if mix is not None else "ffn_block",
    )(*args)


def _inproj_kernel(x_ref, g_ref, csk_ref, cosq_ref, sinq_ref, wcq_ref, wckv_ref, wkr_ref,
                   qg_ref, kvg_ref, wqn_ref, wqr_ref, wqrr_ref, wuk_ref, *rest, with_h):
    if with_h:
        wh_ref, ckv_ref, kr_ref, ckvb_ref, krb_ref, qlat_ref, qrope_ref, zh_ref = rest
        half = wh_ref.shape[1] // 2
    else:
        ckv_ref, kr_ref, ckvb_ref, krb_ref, qlat_ref, qrope_ref, uout_ref = rest
    u = _rms(x_ref[...], g_ref[...]).astype(BF16)
    if not with_h:
        uout_ref[...] = u
    cq = _dot(u, wcq_ref[...])
    ckv = _dot(u, wckv_ref[...])
    kz = _dot(u, wkr_ref[...])
    if with_h:
        zh_ref[:, :half] = _dot(u, wh_ref[:, :half])
    cqn = _rms(cq, qg_ref[...]).astype(BF16)
    qn = _dot(cqn, wqn_ref[...])
    qr_a = _dot(cqn, wqr_ref[...])
    qr_b = _dot(cqn, wqrr_ref[...])
    if with_h:
        zh_ref[:, half:] = _dot(u, wh_ref[:, half:])
    qn = qn.astype(BF16)
    for h in range(MLA_HEADS):
        qlat_ref[:, h * KV_LORA:(h + 1) * KV_LORA] = _dot(
            qn[:, h * QK_NOPE:(h + 1) * QK_NOPE], wuk_ref[h]).astype(BF16)
    ckv = _rms(ckv, kvg_ref[...])
    ckv_ref[...] = ckv
    ckvb_ref[...] = ckv.astype(BF16)
    csk = csk_ref[...]
    kr = kz[:, :QK_ROPE] * csk[:, :QK_ROPE] + kz[:, QK_ROPE:] * csk[:, QK_ROPE:]
    kr_ref[...] = kr
    krb_ref[...] = kr.astype(BF16)
    qrope_ref[...] = (qr_a * cosq_ref[...] + qr_b * sinq_ref[...]).astype(BF16)


def _in_proj(x, g, csk, cosq, sinq, w, with_h):
    m = x.shape[0]
    tm = min(m, IN_PROJ_ROWS)
    nrep = csk.shape[0] // tm
    full = lambda a: pl.BlockSpec(a.shape, lambda i: (0,) * a.ndim, pipeline_mode=pl.Buffered(1))
    row = lambda width: pl.BlockSpec((tm, width), lambda i: (i, 0))
    tab = lambda width: pl.BlockSpec((tm, width), lambda i: (i % nrep, 0))
    rope_w = MLA_HEADS * QK_ROPE
    names = ["w_cq", "w_ckv", "w_kr", "q_norm_g", "kv_norm_g", "w_qn", "w_qr", "w_qrr", "w_uk"]
    names += ["w_h"] if with_h else []
    outs = [(KV_LORA, F32), (QK_ROPE, F32), (KV_LORA, BF16), (QK_ROPE, BF16),
            (MLA_HEADS * KV_LORA, BF16), (rope_w, BF16)]
    outs += [(4 * HG_W, F32)] if with_h else [(D_MODEL, BF16)]
    return pl.pallas_call(
        functools.partial(_inproj_kernel, with_h=with_h),
        grid=(m // tm,),
        in_specs=[row(D_MODEL), full(g), tab(2 * QK_ROPE), tab(rope_w), tab(rope_w)]
        + [full(w[n]) for n in names],
        out_specs=[row(width) for width, _ in outs],
        out_shape=[jax.ShapeDtypeStruct((m, width), dt) for width, dt in outs],
        compiler_params=_params("parallel"),
        name="in_proj",
    )(x, g, csk, cosq, sinq, *[w[n] for n in names])


def _split3(g):
    g1 = g.astype(BF16).astype(F32)
    r1 = g - g1
    g2 = r1.astype(BF16).astype(F32)
    g3 = (r1 - g2).astype(BF16).astype(F32)
    return g1, g2, g3


HG_SUB = 16
HGRN_ROWS_PER_STEP = 32


def _hgrn_chunk_fn(load, store, st_ref, c_ref, v_ref, lbl_ref, ng_ref, units, chunk, valid_rows):
    n_tiles = chunk // SUBLANES

    lbl = lbl_ref[...]
    e = jnp.exp(lbl - jnp.max(lbl, axis=0, keepdims=True))
    lb_all = e[0:1, :] / jnp.sum(e, axis=0, keepdims=True)

    r_i = lax.broadcasted_iota(jnp.int32, (chunk, chunk), 0)
    c_i = lax.broadcasted_iota(jnp.int32, (chunk, chunk), 1)
    tril = (c_i <= r_i).astype(F32)
    row8 = lax.broadcasted_iota(jnp.int32, (SUBLANES, HG_DK), 0)
    ng = ng_ref[...]
    ones = jnp.ones((HG_DK, HG_DV), BF16)
    sub = min(chunk, HG_SUB)
    n_sub = chunk // sub
    pairs = [(s, i) for s in range(chunk) for i in range(s // SUBLANES, (s // sub + 1) * sub // SUBLANES)]
    if n_sub > 1:
        assert sub & (sub - 1) == 0 and chunk & (chunk - 1) == 0
        rows_sub = lax.broadcasted_iota(jnp.int32, (chunk, (n_sub - 1) * chunk), 0) >> (sub.bit_length() - 1)
        cols_blk = lax.broadcasted_iota(jnp.int32, (chunk, (n_sub - 1) * chunk), 1) >> (chunk.bit_length() - 1)
        sub_mask = cols_blk == rows_sub - 1

    def gates(u, start):
        h = units[u][1]
        lb = lb_all[:, h * HG_DK:(h + 1) * HG_DK]
        q = _silu(load(0, u, start))
        f = lb + (1.0 - lb) * _sigmoid(load(1, u, start))
        g = jnp.log(f)
        k = 1.0 - f
        v = load(2, u, start)
        valid = valid_rows(start)
        if valid is not None:
            g = jnp.where(valid, g, 0.0)
            k = jnp.where(valid, k, 0.0)
            v = jnp.where(valid, v, 0.0)
        g1, g2, g3 = _split3(g)
        b = _dot(tril, g1) + _dot(tril, g2) + _dot(tril, g3)
        return q, k, v, b * LOG2E

    def products(u, q, k, v, b2):
        c_ref[u] = b2 - jnp.log(k) * LOG2E
        v_ref[u] = v
        q_t = [q[i * SUBLANES:(i + 1) * SUBLANES, :] for i in range(n_tiles)]
        b_t = [b2[i * SUBLANES:(i + 1) * SUBLANES, :] for i in range(n_tiles)]
        p_tiles = []
        for s, i in pairs:
            d = b_t[i] - c_ref[u, s:s + 1, :]
            if i == s // SUBLANES and s % SUBLANES:
                d = jnp.where(row8 >= s % SUBLANES, d, NEG_BIG)
            p_tiles.append(q_t[i] * jnp.exp2(d))
        a_rep = _dot(jnp.concatenate(p_tiles, axis=0).astype(BF16), ones)

        a_sub = None
        if n_sub > 1:
            zeros_sub = jnp.zeros((sub, HG_DK), F32)
            q_parts, k_parts = [zeros_sub], []
            for j in range(1, n_sub):
                lo = j * sub
                r_j = b2[lo - 1:lo, :]
                q_parts.append(q[lo:lo + sub, :] * jnp.exp2(b2[lo:lo + sub, :] - r_j))
                k_parts += [k[:lo, :] * jnp.exp2(r_j - b2[:lo, :])] + [zeros_sub] * (n_sub - j)
            a_sub = _dot_nt(jnp.concatenate(q_parts, axis=0).astype(BF16),
                            jnp.concatenate(k_parts, axis=0).astype(BF16))

        st = st_ref[u]
        o_state = _dot_nt((q * jnp.exp2(b2)).astype(BF16), st.astype(BF16))
        bl = b2[chunk - 1:chunk, :]
        v_b = v.astype(BF16)
        st_ref[u] = st * jnp.exp2(bl) + _dot_tn(v_b, (k * jnp.exp2(bl - b2)).astype(BF16))
        return a_rep, a_sub, o_state, v_b

    def outputs(u, start, a_rep, a_sub, o_state, v_b):
        o_t = [o_state[i * SUBLANES:(i + 1) * SUBLANES, :] for i in range(n_tiles)]
        for j, (s, i) in enumerate(pairs):
            o_t[i] = o_t[i] + a_rep[j * SUBLANES:(j + 1) * SUBLANES, :] * v_ref[u, s:s + 1, :]
        o = jnp.concatenate(o_t, axis=0) if n_tiles > 1 else o_t[0]
        if a_sub is not None:
            o = o + _dot(jnp.where(sub_mask, a_sub, 0.0).astype(BF16),
                         jnp.concatenate([v_b] * (n_sub - 1), axis=0))
        store(u, start, _rms(o, ng) * _silu(load(3, u, start)))

    def chunk_fn(start, after_gates=None):
        gated = [gates(u, start) for u in range(len(units))]
        if after_gates is not None:
            after_gates()
        prods = [products(u, *gated[u]) for u in range(len(units))]
        for u in range(len(units)):
            outputs(u, start, *prods[u])

    return chunk_fn


def _hgrn_kernel(*refs, chunk, t_valid, has_s0):
    if has_s0:
        hq_ref, hf_ref, hi_ref, hg_ref, lbl_ref, ng_ref, s0_ref = refs[:7]
        refs = refs[7:]
    else:
        hq_ref, hf_ref, hi_ref, hg_ref, lbl_ref, ng_ref = refs[:6]
        s0_ref = None
        refs = refs[6:]
    o_ref, sfin_ref, st_ref, c_ref, v_ref = refs
    ti = pl.program_id(1)
    n_batch, t_tile = hq_ref.shape[:2]
    units = [divmod(u, HG_HEADS) for u in range(n_batch * HG_HEADS)]
    n_chunks = t_tile // chunk
    fields = (hq_ref, hf_ref, hi_ref, hg_ref)

    @pl.when(ti == 0)
    def _():
        for u, (e, h) in enumerate(units):
            st_ref[u] = s0_ref[e, h].T if has_s0 else jnp.zeros((HG_DV, HG_DK), F32)

    def where(u, start):
        e, h = units[u]
        return e, pl.ds(start, chunk), slice(h * HG_DK, (h + 1) * HG_DK)

    def load(field, u, start):
        return fields[field][where(u, start)]

    def store(u, start, value):
        o_ref[where(u, start)] = value.astype(o_ref.dtype)

    def valid_rows(start):
        if t_valid is None:
            return None
        pos = lax.broadcasted_iota(jnp.int32, (chunk, HG_DK), 0) + (ti * t_tile + start)
        return pos < t_valid

    chunk_fn = _hgrn_chunk_fn(load, store, st_ref, c_ref, v_ref, lbl_ref, ng_ref, units, chunk, valid_rows)

    def chunk_body(c, carry):
        chunk_fn(pl.multiple_of(c * chunk, chunk))
        return carry

    lax.fori_loop(0, n_chunks, chunk_body, 0, unroll=4 if n_chunks % 4 == 0 else 1)

    @pl.when(ti == pl.num_programs(1) - 1)
    def _():
        for u, (e, h) in enumerate(units):
            sfin_ref[e, h] = st_ref[u].T


def _hgrn(zh, lb_logits, ng, s0, chunk, t_valid):
    bsz, t, _ = zh.shape
    has_s0 = s0 is not None
    t_tile = min(t, HGRN_TILE_ROWS)
    assert t % t_tile == 0 and t_tile % chunk == 0
    n_batch = math.gcd(bsz, max(1, HGRN_ROWS_PER_STEP // t_tile))
    n_units = n_batch * HG_HEADS
    col = lambda j: pl.BlockSpec((n_batch, t_tile, HG_W), lambda b, i: (b, i, j))
    in_specs = [col(0), col(1), col(2), col(3),
                pl.BlockSpec(lb_logits.shape, lambda b, i: (0, 0)),
                pl.BlockSpec((1, HG_DV), lambda b, i: (0, 0))]
    args = [zh, zh, zh, zh, lb_logits, ng]
    st_spec = pl.BlockSpec((n_batch, HG_HEADS, HG_DK, HG_DV), lambda b, i: (b, 0, 0, 0))
    if has_s0:
        in_specs.append(st_spec)
        args.append(s0)
    return pl.pallas_call(
        functools.partial(_hgrn_kernel, chunk=chunk, t_valid=t_valid, has_s0=has_s0),
        grid=(bsz // n_batch, t // t_tile),
        in_specs=in_specs,
        out_specs=[pl.BlockSpec((n_batch, t_tile, HG_W), lambda b, i: (b, i, 0)), st_spec],
        out_shape=[jax.ShapeDtypeStruct((bsz, t, HG_W), BF16),
                   jax.ShapeDtypeStruct((bsz, HG_HEADS, HG_DK, HG_DV), F32)],
        scratch_shapes=[pltpu.VMEM((n_units, HG_DV, HG_DK), F32),
                        pltpu.VMEM((n_units, chunk, HG_DK), F32),
                        pltpu.VMEM((n_units, chunk, HG_DV), F32)],
        compiler_params=_params("parallel", "arbitrary"),
        name="hgrn",
    )(*args)


HGRN_PIECES = 8


def _hgrn_proj_kernel(ua_ref, ub_ref, u0_ref, wh_ref, lbl_ref, ng_ref, o_ref, sfin_ref,
                      zh_ref, st_ref, c_ref, v_ref, *, chunk, tiles_per_seq):
    g = pl.program_id(0)
    t_tile = ua_ref.shape[0]
    n_chunks = t_tile // chunk
    assert n_chunks == HGRN_PIECES and tiles_per_seq % 2 == 0
    piece = wh_ref.shape[1] // HGRN_PIECES
    units = [(0, h) for h in range(HG_HEADS)]

    def project(u_ref, slot, k):
        cols = slice(k * piece, (k + 1) * piece)
        zh_ref[slot, :, cols] = _dot(u_ref[...], wh_ref[:, cols])

    @pl.when(g == 0)
    def _():
        st_ref[...] = jnp.zeros_like(st_ref)
        for k in range(HGRN_PIECES):
            project(u0_ref, 0, k)

    def run_tile(slot, u_next_ref, row0, new_seq):
        if new_seq is not None:
            for u in range(len(units)):
                st_ref[u] = jnp.where(new_seq, 0.0, st_ref[u])

        def load(field, u, start):
            col = (field * HG_HEADS + units[u][1]) * HG_DK
            return zh_ref[slot, start:start + chunk, col:col + HG_DK]

        def store(u, start, value):
            h = units[u][1]
            o_ref[row0 + start:row0 + start + chunk, h * HG_DV:(h + 1) * HG_DV] = value.astype(o_ref.dtype)

        chunk_fn = _hgrn_chunk_fn(load, store, st_ref, c_ref, v_ref, lbl_ref, ng_ref, units, chunk,
                                  lambda start: None)
        for c in range(n_chunks):
            chunk_fn(c * chunk, functools.partial(project, u_next_ref, 1 - slot, c))

    tile_a = 2 * g
    run_tile(0, ua_ref, 0, (tile_a % tiles_per_seq) == 0)
    run_tile(1, ub_ref, t_tile, None)

    @pl.when((tile_a + 1) % tiles_per_seq == tiles_per_seq - 1)
    def _():
        for u, (_, h) in enumerate(units):
            sfin_ref[h] = st_ref[u].T


def _hgrn_proj(u, w_h, lb_logits, ng, bsz, t, chunk):
    t_tile = chunk * HGRN_PIECES
    assert t % t_tile == 0
    tiles_per_seq = t // t_tile
    n_tiles = bsz * tiles_per_seq
    assert n_tiles % 2 == 0
    const = lambda a: pl.BlockSpec(a.shape, lambda i: (0, 0), pipeline_mode=pl.Buffered(1))
    tile = lambda f: pl.BlockSpec((t_tile, D_MODEL), lambda i: (jnp.minimum(f(i), n_tiles - 1), 0))
    return pl.pallas_call(
        functools.partial(_hgrn_proj_kernel, chunk=chunk, tiles_per_seq=tiles_per_seq),
        grid=(n_tiles // 2,),
        in_specs=[tile(lambda i: 2 * i + 1), tile(lambda i: 2 * i + 2), tile(lambda i: 0),
                  const(w_h), const(lb_logits), const(ng)],
        out_specs=[pl.BlockSpec((2 * t_tile, HG_W), lambda i: (i, 0)),
                   pl.BlockSpec((None, HG_HEADS, HG_DK, HG_DV), lambda i: ((2 * i + 1) // tiles_per_seq, 0, 0, 0))],
        out_shape=[jax.ShapeDtypeStruct((bsz * t, HG_W), BF16),
                   jax.ShapeDtypeStruct((bsz, HG_HEADS, HG_DK, HG_DV), F32)],
        scratch_shapes=[pltpu.VMEM((2, t_tile, 4 * HG_W), F32),
                        pltpu.VMEM((HG_HEADS, HG_DV, HG_DK), F32),
                        pltpu.VMEM((HG_HEADS, chunk, HG_DK), F32),
                        pltpu.VMEM((HG_HEADS, chunk, HG_DV), F32)],
        compiler_params=_params("arbitrary"),
        name="hgrn_proj",
    )(u, u, u, w_h, lb_logits, ng)


ATTN_SCALE = 1.0 / math.sqrt(QK_NOPE + QK_ROPE)
ATTN_EXP2_SCALE = ATTN_SCALE * LOG2E


def _attn_prompt_kernel(ql_ref, qr_ref, kc_ref, kr_ref, wuv_ref, o_ref,
                        m_ref, l_ref, a_ref, acc_ref, s_ref, p_ref, *, tq):
    qi = pl.program_id(1)
    ql = jnp.concatenate([ql_ref[:, h * KV_LORA:(h + 1) * KV_LORA] for h in range(MLA_HEADS)], axis=0)
    qr = jnp.concatenate([qr_ref[:, h * QK_ROPE:(h + 1) * QK_ROPE] for h in range(MLA_HEADS)], axis=0)
    m_ref[...] = jnp.full_like(m_ref, NEG_BIG)
    l_ref[...] = jnp.zeros_like(l_ref)
    a_ref[...] = jnp.zeros_like(a_ref)
    acc_ref[...] = jnp.zeros_like(acc_ref)
    p_ref[1] = jnp.zeros(p_ref.shape[1:], BF16)

    def keys(kb):
        return pl.ds(pl.multiple_of(kb * tq, tq), tq)

    def scores(kb):
        return (_dot_nt(ql, kc_ref[keys(kb), :]) + _dot_nt(qr, kr_ref[keys(kb), :])) * ATTN_EXP2_SCALE

    def add_values(kb, slot):
        acc_ref[...] = (jnp.tile(a_ref[...], (1, KV_LORA // LANES)) * acc_ref[...]
                        + _dot(p_ref[slot], kc_ref[keys(kb), :]))

    def softmax(slot, masked):
        s = s_ref[slot]
        if masked:
            rows = lax.broadcasted_iota(jnp.int32, s.shape, 0) & (tq - 1)
            cols = lax.broadcasted_iota(jnp.int32, s.shape, 1)
            s = jnp.where(cols <= rows, s, NEG_BIG)
        m_prev = m_ref[...]
        m_new = jnp.maximum(m_prev, jnp.max(s, axis=-1, keepdims=True))
        alpha = jnp.exp2(m_prev - m_new)
        p = jnp.exp2(s - jnp.tile(m_new, (1, tq // LANES)))
        l_ref[...] = alpha * l_ref[...] + jnp.sum(p, axis=-1, keepdims=True)
        m_ref[...] = m_new
        a_ref[...] = alpha
        p_ref[slot] = p.astype(BF16)

    s_ref[0] = scores(0)

    def iteration(i, cur):
        s_ref[1 - cur] = scores(i + 1)
        add_values(jnp.maximum(i - 1, 0), 1 - cur)
        softmax(cur, False)

    def last(cur):
        add_values(jnp.maximum(qi - 1, 0), 1 - cur)
        softmax(cur, True)
        add_values(qi, cur)

    def pair(j, carry):
        iteration(2 * j, 0)
        iteration(2 * j + 1, 1)
        return carry

    lax.fori_loop(0, qi >> 1, pair, 0)

    @pl.when((qi & 1) == 1)
    def _():
        iteration(qi - 1, 0)
        last(1)

    @pl.when((qi & 1) == 0)
    def _():
        last(0)

    o_lat =(acc_ref[...] / jnp.tile(l_ref[...], (1, KV_LORA // LANES))).astype(BF16)
    for h in range(MLA_HEADS):
        o_ref[:, h * V_DIM:(h + 1) * V_DIM] = _dot(o_lat[h * tq:(h + 1) * tq, :], wuv_ref[h]).astype(o_ref.dtype)


def _attn_prompt(qlat, qrope, ckv_b, kr_b, w_uv, bsz, t):
    tq = ATTN_Q_ROWS
    assert tq & (tq - 1) == 0 and t % tq == 0
    nq = t // tq
    rows = MLA_HEADS * tq
    return pl.pallas_call(
        functools.partial(_attn_prompt_kernel, tq=tq),
        grid=(bsz, nq),
        in_specs=[
            pl.BlockSpec((tq, MLA_HEADS * KV_LORA), lambda b, i: (b * nq + i, 0)),
            pl.BlockSpec((tq, MLA_HEADS * QK_ROPE), lambda b, i: (b * nq + i, 0)),
            pl.BlockSpec((t, KV_LORA), lambda b, i: (b, 0)),
            pl.BlockSpec((t, QK_ROPE), lambda b, i: (b, 0)),
            pl.BlockSpec(w_uv.shape, lambda b, i: (0, 0, 0)),
        ],
        out_specs=pl.BlockSpec((tq, MLA_HEADS * V_DIM), lambda b, i: (b * nq + i, 0)),
        out_shape=jax.ShapeDtypeStruct((bsz * t, MLA_HEADS * V_DIM), BF16),
        scratch_shapes=[pltpu.VMEM((rows, LANES), F32), pltpu.VMEM((rows, LANES), F32),
                        pltpu.VMEM((rows, LANES), F32), pltpu.VMEM((rows, KV_LORA), F32),
                        pltpu.VMEM((2, rows, tq), F32), pltpu.VMEM((2, rows, tq), BF16)],
        compiler_params=_params("parallel", "parallel"),
        name="attn_prompt",
    )(qlat, qrope, ckv_b, kr_b, w_uv)


PAGES_PER_STEP = 16
PAGE_RING = 4
NEW_KEY_ROWS = 128


def _attn_sample_kernel(pt_ref, ql_ref, qr_ref, nc_ref, nr_ref, wuv_ref, cc_hbm, cr_hbm, o_ref,
                        cbuf, rbuf, sem, wq_ref, wr_ref, m_ref, l_ref, acc_ref, kv_ref, kq_ref, kr_ref,
                        *, t_new, n_q, n_steps):
    b = pl.program_id(0)
    nb = pl.num_programs(0)
    groups = LANES // n_q
    page = cbuf.shape[1] // PAGES_PER_STEP
    ppg = PAGES_PER_STEP // groups
    lat_rep = KV_LORA // LANES

    def page_copies(bb, step, slot):
        copies = []
        for i in range(PAGES_PER_STEP):
            pid = pt_ref[bb, step * PAGES_PER_STEP + i]
            copies.append(pltpu.make_async_copy(cc_hbm.at[pid], cbuf.at[slot, pl.ds(i * page, page), :],
                                                sem.at[slot]))
            copies.append(pltpu.make_async_copy(cr_hbm.at[pid], rbuf.at[slot, i], sem.at[slot]))
        return copies

    @pl.when(b == 0)
    def _():
        for step in range(PAGE_RING):
            for c in page_copies(0, step, step):
                c.start()

    q_shift = n_q.bit_length() - 1
    spread = (lax.broadcasted_iota(jnp.int32, (n_q, LANES), 1) & (n_q - 1)) == lax.broadcasted_iota(
        jnp.int32, (n_q, LANES), 0)
    spread = jnp.where(spread, 1.0, 0.0).astype(BF16)
    for src, dst in ((ql_ref, wq_ref), (qr_ref, wr_ref)):
        feat = src.shape[0]
        tiled = _dot(src[...], spread)
        lane_g = lax.broadcasted_iota(jnp.int32, tiled.shape, 1) >> q_shift
        for g in range(groups):
            dst[g * feat:(g + 1) * feat, :] = jnp.where(lane_g == g, tiled, 0.0).astype(BF16)

    m_ref[...] = jnp.full_like(m_ref, NEG_BIG)
    l_ref[...] = jnp.zeros_like(l_ref)
    acc_ref[...] = jnp.zeros_like(acc_ref)

    def lanes_to_rows(row):
        return jnp.broadcast_to(row, (LANES, LANES)).T

    def update(s, place, values):
        m_prev = m_ref[...]
        m_new = jnp.maximum(m_prev, jnp.max(s, axis=0, keepdims=True))
        alpha = jnp.exp2(m_prev - m_new)
        p = jnp.exp2(s - m_new)
        l_ref[...] = alpha * l_ref[...] + jnp.sum(p, axis=0, keepdims=True)
        m_ref[...] = m_new
        pv = _dot(place(p.T), values)
        acc_ref[...] = jnp.tile(lanes_to_rows(alpha), (1, lat_rep)) * acc_ref[...] + pv

    def block_diag(p_t):
        row_g = lax.broadcasted_iota(jnp.int32, p_t.shape, 0) >> q_shift
        return jnp.concatenate([jnp.where(row_g == g, p_t, 0.0).astype(BF16) for g in range(groups)], axis=1)

    def stage_and_score(j):
        ring = j % PAGE_RING
        slot = j & 1
        for c in page_copies(b, j, ring):
            c.wait()
        for i in range(PAGES_PER_STEP):
            g, r = divmod(i, ppg)
            x = cbuf[ring, i * page:(i + 1) * page, :].astype(BF16)
            kv_ref[slot, i * page:(i + 1) * page, :] = x
            kq_ref[slot, r * page:(r + 1) * page, g * KV_LORA:(g + 1) * KV_LORA] = x
            kr_ref[slot, g * QK_ROPE:(g + 1) * QK_ROPE, r * page:(r + 1) * page] = rbuf[ring, i].astype(BF16)
        if j + PAGE_RING < n_steps:
            nxt = page_copies(b, j + PAGE_RING, ring)
        else:
            nxt = page_copies(jnp.minimum(b + 1, nb - 1), j + PAGE_RING - n_steps, ring)
        for c in nxt:
            c.start()
        return (_dot(kq_ref[slot], wq_ref[...]) + _dot_tn(kr_ref[slot], wr_ref[...])) * ATTN_EXP2_SCALE

    s_cur = stage_and_score(0)
    for j in range(n_steps):
        s_next = stage_and_score(j + 1) if j + 1 < n_steps else None
        update(s_cur, block_diag, kv_ref[j & 1])
        s_cur = s_next

    @pl.when(b == nb - 1)
    def _():
        for step in range(PAGE_RING):
            for c in page_copies(b, step, step):
                c.wait()

    nc = nc_ref[...]
    s2 = (_dot(nc, wq_ref[:KV_LORA, :]) + _dot(nr_ref[...], wr_ref[:QK_ROPE, :])) * ATTN_EXP2_SCALE
    t_k = lax.broadcasted_iota(jnp.int32, s2.shape, 0)
    lane = lax.broadcasted_iota(jnp.int32, s2.shape, 1)
    ok = (lane < n_q) & (t_k <= (lane & (t_new - 1)))
    update(jnp.where(ok, s2, NEG_BIG), lambda p_t: p_t.astype(BF16), nc)

    grp = lambda a, g: a[g * n_q:(g + 1) * n_q]
    total = lambda a: functools.reduce(lambda x, y: x + y, [grp(a, g) for g in range(groups)])
    m_t = lanes_to_rows(m_ref[...])
    l_t = lanes_to_rows(l_ref[...])
    m_q = functools.reduce(jnp.maximum, [grp(m_t, g) for g in range(groups)])
    w = jnp.exp2(m_t - jnp.tile(m_q, (groups, 1)))
    l_q = total(w * l_t)
    o_lat = total(jnp.tile(w, (1, lat_rep)) * acc_ref[...]) / jnp.tile(l_q, (1, lat_rep))
    o_lat = o_lat.astype(BF16)
    head = lax.broadcasted_iota(jnp.int32, (n_q, V_DIM), 0) >> (t_new.bit_length() - 1)
    out = jnp.zeros((n_q, V_DIM), F32)
    for h in range(MLA_HEADS):
        out = out + jnp.where(head == h, _dot(o_lat, wuv_ref[h]), 0.0)
    o_ref[...] = out.astype(o_ref.dtype)


def _attn_sample(ql_t, qr_t, new_c, new_r, w_uv, cache_c, cache_r_t, page_table, t_new):
    bsz, _, n_q = ql_t.shape
    n_pages = page_table.shape[1]
    groups = LANES // n_q
    assert LANES % n_q == 0 and PAGES_PER_STEP % groups == 0 and n_pages % PAGES_PER_STEP == 0
    assert t_new & (t_new - 1) == 0 and n_q & (n_q - 1) == 0 and t_new <= NEW_KEY_ROWS and n_q >= SUBLANES
    page = cache_c.shape[1]
    keys = PAGES_PER_STEP * page
    n_steps = n_pages // PAGES_PER_STEP
    assert n_steps % PAGE_RING == 0

    per_b = lambda a: pl.BlockSpec((None,) + a.shape[1:], lambda b, pt: (b, 0, 0))
    grid_spec = pltpu.PrefetchScalarGridSpec(
        num_scalar_prefetch=1,
        grid=(bsz,),
        in_specs=[per_b(ql_t), per_b(qr_t), per_b(new_c), per_b(new_r),
                  pl.BlockSpec(w_uv.shape, lambda b, pt: (0, 0, 0)),
                  pl.BlockSpec(memory_space=pl.ANY), pl.BlockSpec(memory_space=pl.ANY)],
        out_specs=pl.BlockSpec((None, n_q, V_DIM), lambda b, pt: (b, 0, 0)),
        scratch_shapes=[pltpu.VMEM((PAGE_RING, keys, KV_LORA), cache_c.dtype),
                        pltpu.VMEM((PAGE_RING, PAGES_PER_STEP, QK_ROPE, page), cache_r_t.dtype),
                        pltpu.SemaphoreType.DMA((PAGE_RING,)),
                        pltpu.VMEM((groups * KV_LORA, LANES), BF16),
                        pltpu.VMEM((groups * QK_ROPE, LANES), BF16),
                        pltpu.VMEM((1, LANES), F32), pltpu.VMEM((1, LANES), F32),
                        pltpu.VMEM((LANES, KV_LORA), F32),
                        pltpu.VMEM((2, keys, KV_LORA), BF16),
                        pltpu.VMEM((2, keys // groups, groups * KV_LORA), BF16),
                        pltpu.VMEM((2, groups * QK_ROPE, keys // groups), BF16)],
    )
    return pl.pallas_call(
        functools.partial(_attn_sample_kernel, t_new=t_new, n_q=n_q, n_steps=n_steps),
        grid_spec=grid_spec,
        out_shape=jax.ShapeDtypeStruct((bsz, n_q, V_DIM), BF16),
        compiler_params=_params("arbitrary"),
        name="attn_sample",
    )(page_table, ql_t, qr_t, new_c, new_r, w_uv, cache_c, cache_r_t)


def _rot_cols(w):
    half = w.shape[-1] // 2
    return jnp.concatenate([-w[..., half:], w[..., :half]], axis=-1)


def _prep_weights(norm_g, w_ffn_gate, w_ffn_up, w_ffn_down, w_in, q_norm_g, w_q_up, kv_norm_g,
                  w_kv_up, hg_norm_g, w_out):
    bf = lambda a: a.astype(BF16)
    w_h, w_cq, w_ckv, w_kr = jnp.split(w_in, [4 * HG_W, 4 * HG_W + Q_LORA, 4 * HG_W + Q_LORA + KV_LORA], axis=-1)
    w_q_rope = w_q_up[..., QK_NOPE:]
    return {
        "norm_g": norm_g.reshape(-1, 1, D_MODEL),
        "wg": bf(w_ffn_gate), "wu": bf(w_ffn_up), "wd": bf(w_ffn_down),
        "w_h": bf(w_h), "w_cq": bf(w_cq), "w_ckv": bf(w_ckv),
        "w_kr": bf(jnp.concatenate([w_kr, _rot_cols(w_kr)], axis=-1)),
        "q_norm_g": q_norm_g.reshape(1, Q_LORA), "kv_norm_g": kv_norm_g.reshape(1, KV_LORA),
        "w_qn": bf(w_q_up[..., :QK_NOPE].reshape(Q_LORA, MLA_HEADS * QK_NOPE)),
        "w_qr": bf(w_q_rope.reshape(Q_LORA, MLA_HEADS * QK_ROPE)),
        "w_qrr": bf(_rot_cols(w_q_rope).reshape(Q_LORA, MLA_HEADS * QK_ROPE)),
        "w_uk": bf(jnp.transpose(w_kv_up[..., :QK_NOPE], (1, 2, 0))),
        "w_uv": bf(jnp.transpose(w_kv_up[..., QK_NOPE:], (1, 0, 2))),
        "hg_norm_g": hg_norm_g.reshape(1, HG_DV),
        "w_out1": bf(w_out[:HG_W]), "w_out2": bf(w_out[HG_W:]),
    }


def _rope_tables(pos):
    half = QK_ROPE // 2
    inv = ROPE_THETA ** (-jnp.arange(half, dtype=F32) / half)
    ang = pos.astype(F32)[:, None] * inv[None, :]
    cos = jnp.tile(jnp.cos(ang), (1, 2))
    sin = jnp.tile(jnp.sin(ang), (1, 2))
    return (jnp.concatenate([cos, sin], axis=-1), jnp.tile(cos, (1, MLA_HEADS)), jnp.tile(sin, (1, MLA_HEADS)))


def kernel(x_prompt, x_sample, cache_kv_latent, cache_k_rope, state_hgrn, page_table, hgrn_lb_logits,
           norm_g, w_ffn_gate, w_ffn_up, w_ffn_down, w_in, q_norm_g, w_q_up, kv_norm_g, w_kv_up,
           hg_norm_g, w_out):
    bp, tp, _ = x_prompt.shape
    bs, ts, _ = x_sample.shape
    depth = norm_g.shape[0]
    assert depth == 1
    past_len = page_table.shape[1] * cache_kv_latent.shape[2]
    w = _prep_weights(norm_g[0], w_ffn_gate[0], w_ffn_up[0], w_ffn_down[0], w_in[0], q_norm_g[0],
                      w_q_up[0], kv_norm_g[0], w_kv_up[0], hg_norm_g[0], w_out[0])
    ng = w["norm_g"]
    lb_logits = hgrn_lb_logits.astype(F32)

    def pre(x, tables, with_h):
        h1 = _ffn_block(x, ng[0], ng[1], w["wg"], w["wu"], w["wd"], 0)
        return (h1,) + tuple(_in_proj(h1, ng[2], *tables, w, with_h))

    def post(h1, o_h, o_m):
        return _ffn_block(h1, ng[4], ng[5], w["wg"], w["wu"], w["wd"], 1,
                          mix=(o_h, o_m, w["w_out1"], w["w_out2"], ng[3]))

    h1, ckv, kr, ckv_b, kr_b, qlat, qrope, u_p = pre(
        x_prompt.reshape(bp * tp, D_MODEL), _rope_tables(jnp.arange(tp, dtype=jnp.int32)), False)
    o_h, st_p = _hgrn_proj(u_p, w["w_h"], lb_logits, w["hg_norm_g"], bp, tp, chunk=HGRN_CHUNK)
    o_m = _attn_prompt(qlat, qrope, ckv_b, kr_b, w["w_uv"], bp, tp)
    y_p = post(h1, o_h, o_m).reshape(bp, tp, D_MODEL)
    ckv_p = ckv.reshape(1, bp, tp, KV_LORA)
    kr_p = kr.reshape(1, bp, tp, QK_ROPE)

    pos_s = past_len + jnp.arange(ts, dtype=jnp.int32)
    tabs = tuple(jnp.tile(a, (bs, 1)) for a in _rope_tables(pos_s))
    h1s, ckvs, krs, ckvs_b, krs_b, qlats, qropes, zhs = pre(x_sample.reshape(bs * ts, D_MODEL), tabs, True)
    t_pad = -(-ts // SUBLANES) * SUBLANES
    zhs_pad = jnp.pad(zhs.reshape(bs, ts, 4 * HG_W), ((0, 0), (0, t_pad - ts), (0, 0)))
    o_hs, st_s = _hgrn(zhs_pad, lb_logits, w["hg_norm_g"], state_hgrn[0], chunk=SUBLANES,
                       t_valid=ts if t_pad != ts else None)
    q_cols = lambda a, wd: a.reshape(bs, ts, MLA_HEADS, wd).transpose(0, 3, 2, 1).reshape(bs, wd, MLA_HEADS * ts)
    pad_new = lambda a, wd: jnp.pad(a.reshape(bs, ts, wd), ((0, 0), (0, NEW_KEY_ROWS - ts), (0, 0)))
    o_ms = _attn_sample(q_cols(qlats, KV_LORA), q_cols(qropes, QK_ROPE),
                        pad_new(ckvs_b, KV_LORA), pad_new(krs_b, QK_ROPE), w["w_uv"],
                        cache_kv_latent[0], jnp.swapaxes(cache_k_rope[0], 1, 2), page_table, ts)
    o_ms = o_ms.reshape(bs, MLA_HEADS, ts, V_DIM).transpose(0, 2, 1, 3).reshape(bs * ts, MLA_HEADS * V_DIM)
    y_s = post(h1s, o_hs[:, :ts].reshape(bs * ts, HG_W), o_ms).reshape(bs, ts, D_MODEL)

    return (y_p, y_s, ckv_p, kr_p, st_p[None], ckvs.reshape(1, bs, ts, KV_LORA),
            krs.reshape(1, bs, ts, QK_ROPE), st_s[None])
```

```python
import functools
import math

import jax
import jax.numpy as jnp
from jax import lax
from jax.experimental import pallas as pl
from jax.experimental.pallas import tpu as pltpu

D_MODEL = 1024
D_FF = 2816
HG_HEADS = 4
HG_DK = 128
HG_DV = 128
MLA_HEADS = 4
QK_NOPE = 128
QK_ROPE = 64
V_DIM = 128
Q_LORA = 768
KV_LORA = 256
ROPE_THETA = 10000.0
EPS = 1e-6
HG_W = HG_HEADS * HG_DK

VMEM_LIMIT_BYTES = 56 * 1024 * 1024
SUBLANES = 8
LANES = 128
IN_PROJ_ROWS = 1024
HGRN_CHUNK = 64
HGRN_TILE_ROWS = 512
ATTN_Q_ROWS = 256
NEG_BIG = -1e30
LOG2E = math.log2(math.e)

F32 = jnp.float32
BF16 = jnp.bfloat16


def _dot(a, b):
    return jnp.dot(a, b, preferred_element_type=F32)


def _dot_nt(a, b):
    return lax.dot_general(a, b, (((1,), (1,)), ((), ())), preferred_element_type=F32)


def _dot_tn(a, b):
    return lax.dot_general(a, b, (((0,), (0,)), ((), ())), preferred_element_type=F32)


def _rms(x, g):
    return x * lax.rsqrt(jnp.mean(x * x, axis=-1, keepdims=True) + EPS) * g


def _sigmoid(x):
    return 1.0 / (1.0 + jnp.exp(-x))


def _silu(x):
    return x * _sigmoid(x)


def _params(*sem):
    return pltpu.CompilerParams(dimension_semantics=sem, vmem_limit_bytes=VMEM_LIMIT_BYTES)


FFN_CHUNK = 256
FFN_SUB_ROWS = 512


def _ffn_kernel(*refs, mix_in):
    if mix_in:
        h_ref, oh_ref, om_ref, w1_ref, w2_ref, gmix_ref = refs[:6]
        refs = refs[6:]
    else:
        h_ref = refs[0]
        refs = refs[1:]
    gpre_ref, gpost_ref, wg_ref, wu_ref, wd_ref, o_ref = refs
    tm = h_ref.shape[0]
    sub = min(tm, FFN_SUB_ROWS)
    tiles = [slice(r * sub, (r + 1) * sub) for r in range(tm // sub)]
    if mix_in:
        mixed = [_dot(oh_ref[rows, :], w1_ref[...]) + _dot(om_ref[rows, :], w2_ref[...]) for rows in tiles]
        xs = [h_ref[rows, :] + _rms(m, gmix_ref[...]) for rows, m in zip(tiles, mixed)]
    else:
        xs = [h_ref[rows, :] for rows in tiles]
    xns = [_rms(x, gpre_ref[...]).astype(BF16) for x in xs]
    for rows, x, xn in zip(tiles, xs, xns):
        acc = None
        for f in range(D_FF // FFN_CHUNK):
            cols = slice(f * FFN_CHUNK, (f + 1) * FFN_CHUNK)
            a = _dot(xn, wg_ref[:, cols])
            b = _dot(xn, wu_ref[:, cols])
            part = _dot((_silu(a) * b).astype(BF16), wd_ref[cols, :])
            acc = part if acc is None else acc + part
        o_ref[rows, :] = x + 0.5 * _rms(acc, gpost_ref[...])


def _ffn_block(x, g_pre, g_post, wg, wu, wd, which, mix=None):
    m = x.shape[0]
    tm = min(m, 2 * FFN_SUB_ROWS)

    def const(a):
        if a.ndim == 2:
            return pl.BlockSpec(a.shape, lambda i: (0, 0), pipeline_mode=pl.Buffered(1))
        return pl.BlockSpec((None,) + a.shape[1:], lambda i: (which, 0, 0), pipeline_mode=pl.Buffered(1))

    row = lambda a: pl.BlockSpec((tm, a.shape[1]), lambda i: (i, 0))
    args, in_specs = [x], [row(x)]
    if mix is not None:
        o_h, o_m, w1, w2, g_mix = mix
        args += [o_h, o_m, w1, w2, g_mix]
        in_specs += [row(o_h), row(o_m), const(w1), const(w2), const(g_mix)]
    args += [g_pre, g_post, wg, wu, wd]
    in_specs += [const(g_pre), const(g_post), const(wg), const(wu), const(wd)]
    return pl.pallas_call(
        functools.partial(_ffn_kernel, mix_in=mix is not None),
        grid=(m // tm,),
        in_specs=in_specs,
        out_specs=row(x),
        out_shape=jax.ShapeDtypeStruct((m, D_MODEL), F32),
        compiler_params=_params("parallel"),
        name="ffn_mix_block" if mix is not None else "ffn_block",
    )(*args)


def _inproj_kernel(x_ref, g_ref, csk_ref, cosq_ref, sinq_ref, wcq_ref, wckv_ref, wkr_ref,
                   qg_ref, kvg_ref, wqn_ref, wqr_ref, wqrr_ref, wuk_ref, *rest, with_h):
    if with_h:
        wh_ref, ckv_ref, kr_ref, ckvb_ref, krb_ref, qlat_ref, qrope_ref, zh_ref = rest
        half = wh_ref.shape[1] // 2
    else:
        ckv_ref, kr_ref, ckvb_ref, krb_ref, qlat_ref, qrope_ref, uout_ref = rest
    u = _rms(x_ref[...], g_ref[...]).astype(BF16)
    if not with_h:
        uout_ref[...] = u
    cq = _dot(u, wcq_ref[...])
    ckv = _dot(u, wckv_ref[...])
    kz = _dot(u, wkr_ref[...])
    if with_h:
        zh_ref[:, :half] = _dot(u, wh_ref[:, :half])
    cqn = _rms(cq, qg_ref[...]).astype(BF16)
    qn = _dot(cqn, wqn_ref[...])
    qr_a = _dot(cqn, wqr_ref[...])
    qr_b = _dot(cqn, wqrr_ref[...])
    if with_h:
        zh_ref[:, half:] = _dot(u, wh_ref[:, half:])
    qn = qn.astype(BF16)
    for h in range(MLA_HEADS):
        qlat_ref[:, h * KV_LORA:(h + 1) * KV_LORA] = _dot(
            qn[:, h * QK_NOPE:(h + 1) * QK_NOPE], wuk_ref[h]).astype(BF16)
    ckv = _rms(ckv, kvg_ref[...])
    ckv_ref[...] = ckv
    ckvb_ref[...] = ckv.astype(BF16)
    csk = csk_ref[...]
    kr = kz[:, :QK_ROPE] * csk[:, :QK_ROPE] + kz[:, QK_ROPE:] * csk[:, QK_ROPE:]
    kr_ref[...] = kr
    krb_ref[...] = kr.astype(BF16)
    qrope_ref[...] = (qr_a * cosq_ref[...] + qr_b * sinq_ref[...]).astype(BF16)


def _in_proj(x, g, csk, cosq, sinq, w, with_h):
    m = x.shape[0]
    tm = min(m, IN_PROJ_ROWS)
    nrep = csk.shape[0] // tm
    full = lambda a: pl.BlockSpec(a.shape, lambda i: (0,) * a.ndim, pipeline_mode=pl.Buffered(1))
    row = lambda width: pl.BlockSpec((tm, width), lambda i: (i, 0))
    tab = lambda width: pl.BlockSpec((tm, width), lambda i: (i % nrep, 0))
    rope_w = MLA_HEADS * QK_ROPE
    names = ["w_cq", "w_ckv", "w_kr", "q_norm_g", "kv_norm_g", "w_qn", "w_qr", "w_qrr", "w_uk"]
    names += ["w_h"] if with_h else []
    outs = [(KV_LORA, F32), (QK_ROPE, F32), (KV_LORA, BF16), (QK_ROPE, BF16),
            (MLA_HEADS * KV_LORA, BF16), (rope_w, BF16)]
    outs += [(4 * HG_W, F32)] if with_h else [(D_MODEL, BF16)]
    return pl.pallas_call(
        functools.partial(_inproj_kernel, with_h=with_h),
        grid=(m // tm,),
        in_specs=[row(D_MODEL), full(g), tab(2 * QK_ROPE), tab(rope_w), tab(rope_w)]
        + [full(w[n]) for n in names],
        out_specs=[row(width) for width, _ in outs],
        out_shape=[jax.ShapeDtypeStruct((m, width), dt) for width, dt in outs],
        compiler_params=_params("parallel"),
        name="in_proj",
    )(x, g, csk, cosq, sinq, *[w[n] for n in names])


def _split3(g):
    g1 = g.astype(BF16).astype(F32)
    r1 = g - g1
    g2 = r1.astype(BF16).astype(F32)
    g3 = (r1 - g2).astype(BF16).astype(F32)
    return g1, g2, g3


HG_SUB = 16
HGRN_ROWS_PER_STEP = 32


def _hgrn_chunk_fn(load, store, st_ref, c_ref, v_ref, lbl_ref, ng_ref, units, chunk, valid_rows):
    n_tiles = chunk // SUBLANES

    lbl = lbl_ref[...]
    e = jnp.exp(lbl - jnp.max(lbl, axis=0, keepdims=True))
    lb_all = e[0:1, :] / jnp.sum(e, axis=0, keepdims=True)

    r_i = lax.broadcasted_iota(jnp.int32, (chunk, chunk), 0)
    c_i = lax.broadcasted_iota(jnp.int32, (chunk, chunk), 1)
    tril = (c_i <= r_i).astype(F32)
    row8 = lax.broadcasted_iota(jnp.int32, (SUBLANES, HG_DK), 0)
    ng = ng_ref[...]
    ones = jnp.ones((HG_DK, HG_DV), BF16)
    sub = min(chunk, HG_SUB)
    n_sub = chunk // sub
    pairs = [(s, i) for s in range(chunk) for i in range(s // SUBLANES, (s // sub + 1) * sub // SUBLANES)]
    if n_sub > 1:
        assert sub & (sub - 1) == 0 and chunk & (chunk - 1) == 0
        rows_sub = lax.broadcasted_iota(jnp.int32, (chunk, (n_sub - 1) * chunk), 0) >> (sub.bit_length() - 1)
        cols_blk = lax.broadcasted_iota(jnp.int32, (chunk, (n_sub - 1) * chunk), 1) >> (chunk.bit_length() - 1)
        sub_mask = cols_blk == rows_sub - 1

    def gates(u, start):
        h = units[u][1]
        lb = lb_all[:, h * HG_DK:(h + 1) * HG_DK]
        q = _silu(load(0, u, start))
        f = lb + (1.0 - lb) * _sigmoid(load(1, u, start))
        g = jnp.log(f)
        k = 1.0 - f
        v = load(2, u, start)
        valid = valid_rows(start)
        if valid is not None:
            g = jnp.where(valid, g, 0.0)
            k = jnp.where(valid, k, 0.0)
            v = jnp.where(valid, v, 0.0)
        g1, g2, g3 = _split3(g)
        b = _dot(tril, g1) + _dot(tril, g2) + _dot(tril, g3)
        return q, k, v, b * LOG2E

    def products(u, q, k, v, b2):
        c_ref[u] = b2 - jnp.log(k) * LOG2E
        v_ref[u] = v
        q_t = [q[i * SUBLANES:(i + 1) * SUBLANES, :] for i in range(n_tiles)]
        b_t = [b2[i * SUBLANES:(i + 1) * SUBLANES, :] for i in range(n_tiles)]
        p_tiles = []
        for s, i in pairs:
            d = b_t[i] - c_ref[u, s:s + 1, :]
            if i == s // SUBLANES and s % SUBLANES:
                d = jnp.where(row8 >= s % SUBLANES, d, NEG_BIG)
            p_tiles.append(q_t[i] * jnp.exp2(d))
        a_rep = _dot(jnp.concatenate(p_tiles, axis=0).astype(BF16), ones)

        a_sub = None
        if n_sub > 1:
            zeros_sub = jnp.zeros((sub, HG_DK), F32)
            q_parts, k_parts = [zeros_sub], []
            for j in range(1, n_sub):
                lo = j * sub
                r_j = b2[lo - 1:lo, :]
                q_parts.append(q[lo:lo + sub, :] * jnp.exp2(b2[lo:lo + sub, :] - r_j))
                k_parts += [k[:lo, :] * jnp.exp2(r_j - b2[:lo, :])] + [zeros_sub] * (n_sub - j)
            a_sub = _dot_nt(jnp.concatenate(q_parts, axis=0).astype(BF16),
                            jnp.concatenate(k_parts, axis=0).astype(BF16))

        st = st_ref[u]
        o_state = _dot_nt((q * jnp.exp2(b2)).astype(BF16), st.astype(BF16))
        bl = b2[chunk - 1:chunk, :]
        v_b = v.astype(BF16)
        st_ref[u] = st * jnp.exp2(bl) + _dot_tn(v_b, (k * jnp.exp2(bl - b2)).astype(BF16))
        return a_rep, a_sub, o_state, v_b

    def outputs(u, start, a_rep, a_sub, o_state, v_b):
        o_t = [o_state[i * SUBLANES:(i + 1) * SUBLANES, :] for i in range(n_tiles)]
        for j, (s, i) in enumerate(pairs):
            o_t[i] = o_t[i] + a_rep[j * SUBLANES:(j + 1) * SUBLANES, :] * v_ref[u, s:s + 1, :]
        o = jnp.concatenate(o_t, axis=0) if n_tiles > 1 else o_t[0]
        if a_sub is not None:
            o = o + _dot(jnp.where(sub_mask, a_sub, 0.0).astype(BF16),
                         jnp.concatenate([v_b] * (n_sub - 1), axis=0))
        store(u, start, _rms(o, ng) * _silu(load(3, u, start)))

    def chunk_fn(start, after_gates=None):
        gated = [gates(u, start) for u in range(len(units))]
        if after_gates is not None:
            after_gates()
        prods = [products(u, *gated[u]) for u in range(len(units))]
        for u in range(len(units)):
            outputs(u, start, *prods[u])

    return chunk_fn


def _hgrn_kernel(*refs, chunk, t_valid, has_s0):
    if has_s0:
        hq_ref, hf_ref, hi_ref, hg_ref, lbl_ref, ng_ref, s0_ref = refs[:7]
        refs = refs[7:]
    else:
        hq_ref, hf_ref, hi_ref, hg_ref, lbl_ref, ng_ref = refs[:6]
        s0_ref = None
        refs = refs[6:]
    o_ref, sfin_ref, st_ref, c_ref, v_ref = refs
    ti = pl.program_id(1)
    n_batch, t_tile = hq_ref.shape[:2]
    units = [divmod(u, HG_HEADS) for u in range(n_batch * HG_HEADS)]
    n_chunks = t_tile // chunk
    fields = (hq_ref, hf_ref, hi_ref, hg_ref)

    @pl.when(ti == 0)
    def _():
        for u, (e, h) in enumerate(units):
            st_ref[u] = s0_ref[e, h].T if has_s0 else jnp.zeros((HG_DV, HG_DK), F32)

    def where(u, start):
        e, h = units[u]
        return e, pl.ds(start, chunk), slice(h * HG_DK, (h + 1) * HG_DK)

    def load(field, u, start):
        return fields[field][where(u, start)]

    def store(u, start, value):
        o_ref[where(u, start)] = value.astype(o_ref.dtype)

    def valid_rows(start):
        if t_valid is None:
            return None
        pos = lax.broadcasted_iota(jnp.int32, (chunk, HG_DK), 0) + (ti * t_tile + start)
        return pos < t_valid

    chunk_fn = _hgrn_chunk_fn(load, store, st_ref, c_ref, v_ref, lbl_ref, ng_ref, units, chunk, valid_rows)

    def chunk_body(c, carry):
        chunk_fn(pl.multiple_of(c * chunk, chunk))
        return carry

    lax.fori_loop(0, n_chunks, chunk_body, 0, unroll=4 if n_chunks % 4 == 0 else 1)

    @pl.when(ti == pl.num_programs(1) - 1)
    def _():
        for u, (e, h) in enumerate(units):
            sfin_ref[e, h] = st_ref[u].T


def _hgrn(zh, lb_logits, ng, s0, chunk, t_valid):
    bsz, t, _ = zh.shape
    has_s0 = s0 is not None
    t_tile = min(t, HGRN_TILE_ROWS)
    assert t % t_tile == 0 and t_tile % chunk == 0
    n_batch = math.gcd(bsz, max(1, HGRN_ROWS_PER_STEP // t_tile))
    n_units = n_batch * HG_HEADS
    col = lambda j: pl.BlockSpec((n_batch, t_tile, HG_W), lambda b, i: (b, i, j))
    in_specs = [col(0), col(1), col(2), col(3),
                pl.BlockSpec(lb_logits.shape, lambda b, i: (0, 0)),
                pl.BlockSpec((1, HG_DV), lambda b, i: (0, 0))]
    args = [zh, zh, zh, zh, lb_logits, ng]
    st_spec = pl.BlockSpec((n_batch, HG_HEADS, HG_DK, HG_DV), lambda b, i: (b, 0, 0, 0))
    if has_s0:
        in_specs.append(st_spec)
        args.append(s0)
    return pl.pallas_call(
        functools.partial(_hgrn_kernel, chunk=chunk, t_valid=t_valid, has_s0=has_s0),
        grid=(bsz // n_batch, t // t_tile),
        in_specs=in_specs,
        out_specs=[pl.BlockSpec((n_batch, t_tile, HG_W), lambda b, i: (b, i, 0)), st_spec],
        out_shape=[jax.ShapeDtypeStruct((bsz, t, HG_W), BF16),
                   jax.ShapeDtypeStruct((bsz, HG_HEADS, HG_DK, HG_DV), F32)],
        scratch_shapes=[pltpu.VMEM((n_units, HG_DV, HG_DK), F32),
                        pltpu.VMEM((n_units, chunk, HG_DK), F32),
                        pltpu.VMEM((n_units, chunk, HG_DV), F32)],
        compiler_params=_params("parallel", "arbitrary"),
        name="hgrn",
    )(*args)


HGRN_PIECES = 8


def _hgrn_proj_kernel(ua_ref, ub_ref, u0_ref, wh_ref, lbl_ref, ng_ref, o_ref, sfin_ref,
                      zh_ref, st_ref, c_ref, v_ref, *, chunk, tiles_per_seq):
    g = pl.program_id(0)
    t_tile = ua_ref.shape[0]
    n_chunks = t_tile // chunk
    assert n_chunks == HGRN_PIECES and tiles_per_seq % 2 == 0
    piece = wh_ref.shape[1] // HGRN_PIECES
    units = [(0, h) for h in range(HG_HEADS)]

    def project(u_ref, slot, k):
        cols = slice(k * piece, (k + 1) * piece)
        zh_ref[slot, :, cols] = _dot(u_ref[...], wh_ref[:, cols])

    @pl.when(g == 0)
    def _():
        st_ref[...] = jnp.zeros_like(st_ref)
        for k in range(HGRN_PIECES):
            project(u0_ref, 0, k)

    def run_tile(slot, u_next_ref, row0, new_seq):
        if new_seq is not None:
            for u in range(len(units)):
                st_ref[u] = jnp.where(new_seq, 0.0, st_ref[u])

        def load(field, u, start):
            col = (field * HG_HEADS + units[u][1]) * HG_DK
            return zh_ref[slot, start:start + chunk, col:col + HG_DK]

        def store(u, start, value):
            h = units[u][1]
            o_ref[row0 + start:row0 + start + chunk, h * HG_DV:(h + 1) * HG_DV] = value.astype(o_ref.dtype)

        chunk_fn = _hgrn_chunk_fn(load, store, st_ref, c_ref, v_ref, lbl_ref, ng_ref, units, chunk,
                                  lambda start: None)
        for c in range(n_chunks):
            chunk_fn(c * chunk, functools.partial(project, u_next_ref, 1 - slot, c))

    tile_a = 2 * g
    run_tile(0, ua_ref, 0, (tile_a % tiles_per_seq) == 0)
    run_tile(1, ub_ref, t_tile, None)

    @pl.when((tile_a + 1) % tiles_per_seq == tiles_per_seq - 1)
    def _():
        for u, (_, h) in enumerate(units):
            sfin_ref[h] = st_ref[u].T


def _hgrn_proj(u, w_h, lb_logits, ng, bsz, t, chunk):
    t_tile = chunk * HGRN_PIECES
    assert t % t_tile == 0
    tiles_per_seq = t // t_tile
    n_tiles = bsz * tiles_per_seq
    assert n_tiles % 2 == 0
    const = lambda a: pl.BlockSpec(a.shape, lambda i: (0, 0), pipeline_mode=pl.Buffered(1))
    tile = lambda f: pl.BlockSpec((t_tile, D_MODEL), lambda i: (jnp.minimum(f(i), n_tiles - 1), 0))
    return pl.pallas_call(
        functools.partial(_hgrn_proj_kernel, chunk=chunk, tiles_per_seq=tiles_per_seq),
        grid=(n_tiles // 2,),
        in_specs=[tile(lambda i: 2 * i + 1), tile(lambda i: 2 * i + 2), tile(lambda i: 0),
                  const(w_h), const(lb_logits), const(ng)],
        out_specs=[pl.BlockSpec((2 * t_tile, HG_W), lambda i: (i, 0)),
                   pl.BlockSpec((None, HG_HEADS, HG_DK, HG_DV), lambda i: ((2 * i + 1) // tiles_per_seq, 0, 0, 0))],
        out_shape=[jax.ShapeDtypeStruct((bsz * t, HG_W), BF16),
                   jax.ShapeDtypeStruct((bsz, HG_HEADS, HG_DK, HG_DV), F32)],
        scratch_shapes=[pltpu.VMEM((2, t_tile, 4 * HG_W), F32),
                        pltpu.VMEM((HG_HEADS, HG_DV, HG_DK), F32),
                        pltpu.VMEM((HG_HEADS, chunk, HG_DK), F32),
                        pltpu.VMEM((HG_HEADS, chunk, HG_DV), F32)],
        compiler_params=_params("arbitrary"),
        name="hgrn_proj",
    )(u, u, u, w_h, lb_logits, ng)


ATTN_SCALE = 1.0 / math.sqrt(QK_NOPE + QK_ROPE)
ATTN_EXP2_SCALE = ATTN_SCALE * LOG2E


def _attn_prompt_kernel(ql_ref, qr_ref, kc_ref, kr_ref, wuv_ref, o_ref,
                        m_ref, l_ref, a_ref, acc_ref, s_ref, p_ref, *, tq):
    qi = pl.program_id(1)
    ql = jnp.concatenate([ql_ref[:, h * KV_LORA:(h + 1) * KV_LORA] for h in range(MLA_HEADS)], axis=0)
    qr = jnp.concatenate([qr_ref[:, h * QK_ROPE:(h + 1) * QK_ROPE] for h in range(MLA_HEADS)], axis=0)
    m_ref[...] = jnp.full_like(m_ref, NEG_BIG)
    l_ref[...] = jnp.zeros_like(l_ref)
    a_ref[...] = jnp.zeros_like(a_ref)
    acc_ref[...] = jnp.zeros_like(acc_ref)
    p_ref[1] = jnp.zeros(p_ref.shape[1:], BF16)

    def keys(kb):
        return pl.ds(pl.multiple_of(kb * tq, tq), tq)

    def scores(kb):
        return (_dot_nt(ql, kc_ref[keys(kb), :]) + _dot_nt(qr, kr_ref[keys(kb), :])) * ATTN_EXP2_SCALE

    def add_values(kb, slot):
        acc_ref[...] = (jnp.tile(a_ref[...], (1, KV_LORA // LANES)) * acc_ref[...]
                        + _dot(p_ref[slot], kc_ref[keys(kb), :]))

    def softmax(slot, masked):
        s = s_ref[slot]
        if masked:
            rows = lax.broadcasted_iota(jnp.int32, s.shape, 0) & (tq - 1)
            cols = lax.broadcasted_iota(jnp.int32, s.shape, 1)
            s = jnp.where(cols <= rows, s, NEG_BIG)
        m_prev = m_ref[...]
        m_new = jnp.maximum(m_prev, jnp.max(s, axis=-1, keepdims=True))
        alpha = jnp.exp2(m_prev - m_new)
        p = jnp.exp2(s - jnp.tile(m_new, (1, tq // LANES)))
        l_ref[...] = alpha * l_ref[...] + jnp.sum(p, axis=-1, keepdims=True)
        m_ref[...] = m_new
        a_ref[...] = alpha
        p_ref[slot] = p.astype(BF16)

    s_ref[0] = scores(0)

    def iteration(i, cur):
        s_ref[1 - cur] = scores(i + 1)
        add_values(jnp.maximum(i - 1, 0), 1 - cur)
        softmax(cur, False)

    def last(cur):
        add_values(jnp.maximum(qi - 1, 0), 1 - cur)
        softmax(cur, True)
        add_values(qi, cur)

    def pair(j, carry):
        iteration(2 * j, 0)
        iteration(2 * j + 1, 1)
        return carry

    lax.fori_loop(0, qi >> 1, pair, 0)

    @pl.when((qi & 1) == 1)
    def _():
        iteration(qi - 1, 0)
        last(1)

    @pl.when((qi & 1) == 0)
    def _():
        last(0)

    o_lat =(acc_ref[...] / jnp.tile(l_ref[...], (1, KV_LORA // LANES))).astype(BF16)
    for h in range(MLA_HEADS):
        o_ref[:, h * V_DIM:(h + 1) * V_DIM] = _dot(o_lat[h * tq:(h + 1) * tq, :], wuv_ref[h]).astype(o_ref.dtype)


def _attn_prompt(qlat, qrope, ckv_b, kr_b, w_uv, bsz, t):
    tq = ATTN_Q_ROWS
    assert tq & (tq - 1) == 0 and t % tq == 0
    nq = t // tq
    rows = MLA_HEADS * tq
    return pl.pallas_call(
        functools.partial(_attn_prompt_kernel, tq=tq),
        grid=(bsz, nq),
        in_specs=[
            pl.BlockSpec((tq, MLA_HEADS * KV_LORA), lambda b, i: (b * nq + i, 0)),
            pl.BlockSpec((tq, MLA_HEADS * QK_ROPE), lambda b, i: (b * nq + i, 0)),
            pl.BlockSpec((t, KV_LORA), lambda b, i: (b, 0)),
            pl.BlockSpec((t, QK_ROPE), lambda b, i: (b, 0)),
            pl.BlockSpec(w_uv.shape, lambda b, i: (0, 0, 0)),
        ],
        out_specs=pl.BlockSpec((tq, MLA_HEADS * V_DIM), lambda b, i: (b * nq + i, 0)),
        out_shape=jax.ShapeDtypeStruct((bsz * t, MLA_HEADS * V_DIM), BF16),
        scratch_shapes=[pltpu.VMEM((rows, LANES), F32), pltpu.VMEM((rows, LANES), F32),
                        pltpu.VMEM((rows, LANES), F32), pltpu.VMEM((rows, KV_LORA), F32),
                        pltpu.VMEM((2, rows, tq), F32), pltpu.VMEM((2, rows, tq), BF16)],
        compiler_params=_params("parallel", "parallel"),
        name="attn_prompt",
    )(qlat, qrope, ckv_b, kr_b, w_uv)


PAGES_PER_STEP = 16
PAGE_RING = 4
NEW_KEY_ROWS = 128


def _attn_sample_kernel(pt_ref, ql_ref, qr_ref, nc_ref, nr_ref, wuv_ref, cc_hbm, cr_hbm, o_ref,
                        cbuf, rbuf, sem, wq_ref, wr_ref, m_ref, l_ref, acc_ref, kv_ref, kq_ref, kr_ref,
                        *, t_new, n_q, n_steps):
    b = pl.program_id(0)
    nb = pl.num_programs(0)
    groups = LANES // n_q
    page = cbuf.shape[1] // PAGES_PER_STEP
    ppg = PAGES_PER_STEP // groups
    lat_rep = KV_LORA // LANES

    def page_copies(bb, step, slot):
        copies = []
        for i in range(PAGES_PER_STEP):
            pid = pt_ref[bb, step * PAGES_PER_STEP + i]
            copies.append(pltpu.make_async_copy(cc_hbm.at[pid], cbuf.at[slot, pl.ds(i * page, page), :],
                                                sem.at[slot]))
            copies.append(pltpu.make_async_copy(cr_hbm.at[pid], rbuf.at[slot, i], sem.at[slot]))
        return copies

    @pl.when(b == 0)
    def _():
        for step in range(PAGE_RING):
            for c in page_copies(0, step, step):
                c.start()

    q_shift = n_q.bit_length() - 1
    spread = (lax.broadcasted_iota(jnp.int32, (n_q, LANES), 1) & (n_q - 1)) == lax.broadcasted_iota(
        jnp.int32, (n_q, LANES), 0)
    spread = jnp.where(spread, 1.0, 0.0).astype(BF16)
    for src, dst in ((ql_ref, wq_ref), (qr_ref, wr_ref)):
        feat = src.shape[0]
        tiled = _dot(src[...], spread)
        lane_g = lax.broadcasted_iota(jnp.int32, tiled.shape, 1) >> q_shift
        for g in range(groups):
            dst[g * feat:(g + 1) * feat, :] = jnp.where(lane_g == g, tiled, 0.0).astype(BF16)

    m_ref[...] = jnp.full_like(m_ref, NEG_BIG)
    l_ref[...] = jnp.zeros_like(l_ref)
    acc_ref[...] = jnp.zeros_like(acc_ref)

    def lanes_to_rows(row):
        return jnp.broadcast_to(row, (LANES, LANES)).T

    def update(s, place, values):
        m_prev = m_ref[...]
        m_new = jnp.maximum(m_prev, jnp.max(s, axis=0, keepdims=True))
        alpha = jnp.exp2(m_prev - m_new)
        p = jnp.exp2(s - m_new)
        l_ref[...] = alpha * l_ref[...] + jnp.sum(p, axis=0, keepdims=True)
        m_ref[...] = m_new
        pv = _dot(place(p.T), values)
        acc_ref[...] = jnp.tile(lanes_to_rows(alpha), (1, lat_rep)) * acc_ref[...] + pv

    def block_diag(p_t):
        row_g = lax.broadcasted_iota(jnp.int32, p_t.shape, 0) >> q_shift
        return jnp.concatenate([jnp.where(row_g == g, p_t, 0.0).astype(BF16) for g in range(groups)], axis=1)

    def stage_and_score(j):
        ring = j % PAGE_RING
        slot = j & 1
        for c in page_copies(b, j, ring):
            c.wait()
        for i in range(PAGES_PER_STEP):
            g, r = divmod(i, ppg)
            x = cbuf[ring, i * page:(i + 1) * page, :].astype(BF16)
            kv_ref[slot, i * page:(i + 1) * page, :] = x
            kq_ref[slot, r * page:(r + 1) * page, g * KV_LORA:(g + 1) * KV_LORA] = x
            kr_ref[slot, g * QK_ROPE:(g + 1) * QK_ROPE, r * page:(r + 1) * page] = rbuf[ring, i].astype(BF16)
        if j + PAGE_RING < n_steps:
            nxt = page_copies(b, j + PAGE_RING, ring)
        else:
            nxt = page_copies(jnp.minimum(b + 1, nb - 1), j + PAGE_RING - n_steps, ring)
        for n, c in enumerate(nxt):
            c.start(priority=(n // 2) % 2)
        return (_dot(kq_ref[slot], wq_ref[...]) + _dot_tn(kr_ref[slot], wr_ref[...])) * ATTN_EXP2_SCALE

    s_cur = stage_and_score(0)
    for j in range(n_steps):
        s_next = stage_and_score(j + 1) if j + 1 < n_steps else None
        update(s_cur, block_diag, kv_ref[j & 1])
        s_cur = s_next

    @pl.when(b == nb - 1)
    def _():
        for step in range(PAGE_RING):
            for c in page_copies(b, step, step):
                c.wait()

    nc = nc_ref[...]
    s2 = (_dot(nc, wq_ref[:KV_LORA, :]) + _dot(nr_ref[...], wr_ref[:QK_ROPE, :])) * ATTN_EXP2_SCALE
    t_k = lax.broadcasted_iota(jnp.int32, s2.shape, 0)
    lane = lax.broadcasted_iota(jnp.int32, s2.shape, 1)
    ok = (lane < n_q) & (t_k <= (lane & (t_new - 1)))
    update(jnp.where(ok, s2, NEG_BIG), lambda p_t: p_t.astype(BF16), nc)

    grp = lambda a, g: a[g * n_q:(g + 1) * n_q]
    total = lambda a: functools.reduce(lambda x, y: x + y, [grp(a, g) for g in range(groups)])
    m_t = lanes_to_rows(m_ref[...])
    l_t = lanes_to_rows(l_ref[...])
    m_q = functools.reduce(jnp.maximum, [grp(m_t, g) for g in range(groups)])
    w = jnp.exp2(m_t - jnp.tile(m_q, (groups, 1)))
    l_q = total(w * l_t)
    o_lat = total(jnp.tile(w, (1, lat_rep)) * acc_ref[...]) / jnp.tile(l_q, (1, lat_rep))
    o_lat = o_lat.astype(BF16)
    head = lax.broadcasted_iota(jnp.int32, (n_q, V_DIM), 0) >> (t_new.bit_length() - 1)
    out = jnp.zeros((n_q, V_DIM), F32)
    for h in range(MLA_HEADS):
        out = out + jnp.where(head == h, _dot(o_lat, wuv_ref[h]), 0.0)
    o_ref[...] = out.astype(o_ref.dtype)


def _attn_sample(ql_t, qr_t, new_c, new_r, w_uv, cache_c, cache_r_t, page_table, t_new):
    bsz, _, n_q = ql_t.shape
    n_pages = page_table.shape[1]
    groups = LANES // n_q
    assert LANES % n_q == 0 and PAGES_PER_STEP % groups == 0 and n_pages % PAGES_PER_STEP == 0
    assert t_new & (t_new - 1) == 0 and n_q & (n_q - 1) == 0 and t_new <= NEW_KEY_ROWS and n_q >= SUBLANES
    page = cache_c.shape[1]
    keys = PAGES_PER_STEP * page
    n_steps = n_pages // PAGES_PER_STEP
    assert n_steps % PAGE_RING == 0

    per_b = lambda a: pl.BlockSpec((None,) + a.shape[1:], lambda b, pt: (b, 0, 0))
    grid_spec = pltpu.PrefetchScalarGridSpec(
        num_scalar_prefetch=1,
        grid=(bsz,),
        in_specs=[per_b(ql_t), per_b(qr_t), per_b(new_c), per_b(new_r),
                  pl.BlockSpec(w_uv.shape, lambda b, pt: (0, 0, 0)),
                  pl.BlockSpec(memory_space=pl.ANY), pl.BlockSpec(memory_space=pl.ANY)],
        out_specs=pl.BlockSpec((None, n_q, V_DIM), lambda b, pt: (b, 0, 0)),
        scratch_shapes=[pltpu.VMEM((PAGE_RING, keys, KV_LORA), cache_c.dtype),
                        pltpu.VMEM((PAGE_RING, PAGES_PER_STEP, QK_ROPE, page), cache_r_t.dtype),
                        pltpu.SemaphoreType.DMA((PAGE_RING,)),
                        pltpu.VMEM((groups * KV_LORA, LANES), BF16),
                        pltpu.VMEM((groups * QK_ROPE, LANES), BF16),
                        pltpu.VMEM((1, LANES), F32), pltpu.VMEM((1, LANES), F32),
                        pltpu.VMEM((LANES, KV_LORA), F32),
                        pltpu.VMEM((2, keys, KV_LORA), BF16),
                        pltpu.VMEM((2, keys // groups, groups * KV_LORA), BF16),
                        pltpu.VMEM((2, groups * QK_ROPE, keys // groups), BF16)],
    )
    return pl.pallas_call(
        functools.partial(_attn_sample_kernel, t_new=t_new, n_q=n_q, n_steps=n_steps),
        grid_spec=grid_spec,
        out_shape=jax.ShapeDtypeStruct((bsz, n_q, V_DIM), BF16),
        compiler_params=_params("arbitrary"),
        name="attn_sample",
    )(page_table, ql_t, qr_t, new_c, new_r, w_uv, cache_c, cache_r_t)


def _rot_cols(w):
    half = w.shape[-1] // 2
    return jnp.concatenate([-w[..., half:], w[..., :half]], axis=-1)


def _prep_weights(norm_g, w_ffn_gate, w_ffn_up, w_ffn_down, w_in, q_norm_g, w_q_up, kv_norm_g,
                  w_kv_up, hg_norm_g, w_out):
    bf = lambda a: a.astype(BF16)
    w_h, w_cq, w_ckv, w_kr = jnp.split(w_in, [4 * HG_W, 4 * HG_W + Q_LORA, 4 * HG_W + Q_LORA + KV_LORA], axis=-1)
    w_q_rope = w_q_up[..., QK_NOPE:]
    return {
        "norm_g": norm_g.reshape(-1, 1, D_MODEL),
        "wg": bf(w_ffn_gate), "wu": bf(w_ffn_up), "wd": bf(w_ffn_down),
        "w_h": bf(w_h), "w_cq": bf(w_cq), "w_ckv": bf(w_ckv),
        "w_kr": bf(jnp.concatenate([w_kr, _rot_cols(w_kr)], axis=-1)),
        "q_norm_g": q_norm_g.reshape(1, Q_LORA), "kv_norm_g": kv_norm_g.reshape(1, KV_LORA),
        "w_qn": bf(w_q_up[..., :QK_NOPE].reshape(Q_LORA, MLA_HEADS * QK_NOPE)),
        "w_qr": bf(w_q_rope.reshape(Q_LORA, MLA_HEADS * QK_ROPE)),
        "w_qrr": bf(_rot_cols(w_q_rope).reshape(Q_LORA, MLA_HEADS * QK_ROPE)),
        "w_uk": bf(jnp.transpose(w_kv_up[..., :QK_NOPE], (1, 2, 0))),
        "w_uv": bf(jnp.transpose(w_kv_up[..., QK_NOPE:], (1, 0, 2))),
        "hg_norm_g": hg_norm_g.reshape(1, HG_DV),
        "w_out1": bf(w_out[:HG_W]), "w_out2": bf(w_out[HG_W:]),
    }


def _rope_tables(pos):
    half = QK_ROPE // 2
    inv = ROPE_THETA ** (-jnp.arange(half, dtype=F32) / half)
    ang = pos.astype(F32)[:, None] * inv[None, :]
    cos = jnp.tile(jnp.cos(ang), (1, 2))
    sin = jnp.tile(jnp.sin(ang), (1, 2))
    return (jnp.concatenate([cos, sin], axis=-1), jnp.tile(cos, (1, MLA_HEADS)), jnp.tile(sin, (1, MLA_HEADS)))


def kernel(x_prompt, x_sample, cache_kv_latent, cache_k_rope, state_hgrn, page_table, hgrn_lb_logits,
           norm_g, w_ffn_gate, w_ffn_up, w_ffn_down, w_in, q_norm_g, w_q_up, kv_norm_g, w_kv_up,
           hg_norm_g, w_out):
    bp, tp, _ = x_prompt.shape
    bs, ts, _ = x_sample.shape
    depth = norm_g.shape[0]
    assert depth == 1
    past_len = page_table.shape[1] * cache_kv_latent.shape[2]
    w = _prep_weights(norm_g[0], w_ffn_gate[0], w_ffn_up[0], w_ffn_down[0], w_in[0], q_norm_g[0],
                      w_q_up[0], kv_norm_g[0], w_kv_up[0], hg_norm_g[0], w_out[0])
    ng = w["norm_g"]
    lb_logits = hgrn_lb_logits.astype(F32)

    def pre(x, tables, with_h):
        h1 = _ffn_block(x, ng[0], ng[1], w["wg"], w["wu"], w["wd"], 0)
        return (h1,) + tuple(_in_proj(h1, ng[2], *tables, w, with_h))

    def post(h1, o_h, o_m):
        return _ffn_block(h1, ng[4], ng[5], w["wg"], w["wu"], w["wd"], 1,
                          mix=(o_h, o_m, w["w_out1"], w["w_out2"], ng[3]))

    h1, ckv, kr, ckv_b, kr_b, qlat, qrope, u_p = pre(
        x_prompt.reshape(bp * tp, D_MODEL), _rope_tables(jnp.arange(tp, dtype=jnp.int32)), False)
    o_h, st_p = _hgrn_proj(u_p, w["w_h"], lb_logits, w["hg_norm_g"], bp, tp, chunk=HGRN_CHUNK)
    o_m = _attn_prompt(qlat, qrope, ckv_b, kr_b, w["w_uv"], bp, tp)
    y_p = post(h1, o_h, o_m).reshape(bp, tp, D_MODEL)
    ckv_p = ckv.reshape(1, bp, tp, KV_LORA)
    kr_p = kr.reshape(1, bp, tp, QK_ROPE)

    pos_s = past_len + jnp.arange(ts, dtype=jnp.int32)
    tabs = tuple(jnp.tile(a, (bs, 1)) for a in _rope_tables(pos_s))
    h1s, ckvs, krs, ckvs_b, krs_b, qlats, qropes, zhs = pre(x_sample.reshape(bs * ts, D_MODEL), tabs, True)
    t_pad = -(-ts // SUBLANES) * SUBLANES
    zhs_pad = jnp.pad(zhs.reshape(bs, ts, 4 * HG_W), ((0, 0), (0, t_pad - ts), (0, 0)))
    o_hs, st_s = _hgrn(zhs_pad, lb_logits, w["hg_norm_g"], state_hgrn[0], chunk=SUBLANES,
                       t_valid=ts if t_pad != ts else None)
    q_cols = lambda a, wd: a.reshape(bs, ts, MLA_HEADS, wd).transpose(0, 3, 2, 1).reshape(bs, wd, MLA_HEADS * ts)
    pad_new = lambda a, wd: jnp.pad(a.reshape(bs, ts, wd), ((0, 0), (0, NEW_KEY_ROWS - ts), (0, 0)))
    o_ms = _attn_sample(q_cols(qlats, KV_LORA), q_cols(qropes, QK_ROPE),
                        pad_new(ckvs_b, KV_LORA), pad_new(krs_b, QK_ROPE), w["w_uv"],
                        cache_kv_latent[0], jnp.swapaxes(cache_k_rope[0], 1, 2), page_table, ts)
    o_ms = o_ms.reshape(bs, MLA_HEADS, ts, V_DIM).transpose(0, 2, 1, 3).reshape(bs * ts, MLA_HEADS * V_DIM)
    y_s = post(h1s, o_hs[:, :ts].reshape(bs * ts, HG_W), o_ms).reshape(bs, ts, D_MODEL)

    return (y_p, y_s, ckv_p, kr_p, st_p[None], ckvs.reshape(1, bs, ts, KV_LORA),
            krs.reshape(1, bs, ts, QK_ROPE), st_s[None])
```
